```python
import math
import jax, jax.numpy as jnp
from jax import lax
import numpy as np

D_MODEL = 1024
BATCH = 16
SEQ = 4096
DEPTH = 4

CTX_LEN = 256
GRID_W = 64

ATT_QK_DIM = 64
ATT_V_DIM = 2 * ATT_QK_DIM
ATT_W = D_MODEL // 2
ATT_HEADS = ATT_W // ATT_V_DIM
QK_W = ATT_HEADS * 2 * ATT_QK_DIM
LRU_W = D_MODEL // 4
LRU_BLOCKS = 4
LRU_BLOCK_W = LRU_W // LRU_BLOCKS
LRU_CONV_W = 4
LRU_C = 8.0
FNET_W = D_MODEL - ATT_W - LRU_W
FNET_GROUPS = 4
FNET_GROUP_W = FNET_W // FNET_GROUPS
IN_SPLITS = [QK_W, 2 * QK_W, 2 * QK_W + ATT_W, 2 * QK_W + ATT_W + LRU_W, 2 * QK_W + ATT_W + 2 * LRU_W]
IN_W = 2 * QK_W + ATT_W + 2 * LRU_W + FNET_W
ROPE_BASE = 10000.0
Q_BLOCK = 128
N_EXPERTS = 16
CAPACITY_FACTOR = 2
D_EXPERT = 2 * D_MODEL
DEEPNORM_ALPHA = (2 * DEPTH) ** 0.25
DEEPNORM_BETA = (8 * DEPTH) ** -0.25
LN_EPS = 1e-5
RMS_EPS = 1e-6

kernel_name = "hymba_style_diffusion_hybrid_ec_moe"

F32 = jnp.float32


def _ln(x):
    xf = x.astype(F32)
    mu = jnp.mean(xf, -1, keepdims=True)
    var = jnp.mean(jnp.square(xf - mu), -1, keepdims=True)
    return ((xf - mu) * lax.rsqrt(var + LN_EPS)).astype(x.dtype)


def _ln_affine(x, g, b):
    return _ln(x) * g + b


def _rope_1d(x, pos):
    half = x.shape[-1] // 2
    inv = ROPE_BASE ** (-jnp.arange(half, dtype=F32) / half)
    ang = pos.astype(F32)[:, None] * inv
    ang = ang.reshape((ang.shape[0],) + (1,) * (x.ndim - 3) + (half,))
    cos, sin = jnp.cos(ang), jnp.sin(ang)
    xf = x.astype(F32)
    x1, x2 = xf[..., :half], xf[..., half:]
    return jnp.concatenate([x1 * cos - x2 * sin, x1 * sin + x2 * cos], -1).astype(x.dtype)


def _axial_rope(x, row, col):
    h = x.shape[-1] // 2
    return jnp.concatenate([_rope_1d(x[..., :h], row), _rope_1d(x[..., h:], col)], -1)


def _split_proj(z):
    b, t, _ = z.shape
    q, k, v, ux, ug, uf = jnp.split(z, IN_SPLITS, axis=-1)
    q = q.reshape(b, t, ATT_HEADS, 2, ATT_QK_DIM)
    k = k.reshape(b, t, ATT_HEADS, 2, ATT_QK_DIM)
    v = v.reshape(b, t, ATT_HEADS, ATT_V_DIM)
    return q, k, v, ux, ug, uf


def _diff_attend(q, k, v, lam):
    s = jnp.einsum('bqhmd,bkhmd->bhmqk', q, k).astype(F32) * (ATT_QK_DIM ** -0.5)
    p = jax.nn.softmax(s, axis=-1)
    w = p[:, :, 0] - lam * p[:, :, 1]
    return jnp.einsum('bhqk,bkhd->bqhd', w.astype(v.dtype), v)


def _latent_diff_attention(q, k, v, lam):
    b, n = q.shape[:2]
    nb = n // Q_BLOCK
    qb = q.reshape((b, nb, Q_BLOCK) + q.shape[2:]).swapaxes(0, 1)
    ob = lax.map(lambda blk: _diff_attend(blk, k, v, lam), qb)
    return ob.swapaxes(0, 1).reshape((b, n) + ob.shape[3:])


def _head_rms(o, g, lam_init):
    of = o.astype(F32)
    r = of * lax.rsqrt(jnp.mean(jnp.square(of), -1, keepdims=True) + RMS_EPS)
    r = r * g * (1.0 - lam_init)
    return r.reshape(o.shape[0], o.shape[1], ATT_W).astype(o.dtype)


def _centred_dwconv(u, w, b):
    left = LRU_CONV_W // 2
    right = LRU_CONV_W - 1 - left
    t = u.shape[1]
    up = jnp.pad(u, ((0, 0), (left, right), (0, 0)))
    return sum((up[:, j:j + t] * w[j] for j in range(LRU_CONV_W)), b)


def _lin_combine(e1, e2):
    a1, b1 = e1
    a2, b2 = e2
    return a1 * a2, a2 * b1 + b2


def _rglru(u, wa, ba, wx, bx, lam, h0, reverse):
    b, t, w = u.shape
    ub = u.reshape(b, t, LRU_BLOCKS, LRU_BLOCK_W)
    r = jax.nn.sigmoid((jnp.einsum('btnc,ncd->btnd', ub, wa).reshape(b, t, w) + ba).astype(F32))
    i = jax.nn.sigmoid((jnp.einsum('btnc,ncd->btnd', ub, wx).reshape(b, t, w) + bx).astype(F32))
    log_a = -LRU_C * r * jax.nn.softplus(-lam.astype(F32))
    a = jnp.exp(log_a)
    inp = jnp.sqrt(-jnp.expm1(2.0 * log_a)) * i * u.astype(F32)
    a_cum, h = lax.associative_scan(_lin_combine, (a, inp), axis=1, reverse=reverse)
    return h + a_cum * h0[:, None, :]


def _fourier(uf):
    b, t, _ = uf.shape
    g = uf.astype(F32).reshape(b, t, FNET_GROUPS, FNET_GROUP_W)
    y = jnp.fft.fft2(g, axes=(1, 3), norm='ortho').real
    return y.reshape(b, t, FNET_W).astype(uf.dtype)


def _expert_choice(h, w_router, w_gate, w_up, w_down):
    b, t, d = h.shape
    cap = CAPACITY_FACTOR * t // N_EXPERTS
    s = jax.nn.softmax(jnp.einsum('btd,de->bte', h, w_router).astype(F32), axis=-1)
    g, idx = lax.top_k(jnp.swapaxes(s, 1, 2), cap)
    xs = jax.vmap(lambda hb, ib: hb[ib])(h, idx)
    a = jnp.einsum('becd,edf->becf', xs, w_gate)
    u = jnp.einsum('becd,edf->becf', xs, w_up)
    y = jnp.einsum('becf,efd->becd', jax.nn.silu(a) * u, w_down) * g[..., None].astype(h.dtype)
    return jax.vmap(lambda ib, yb: jnp.zeros((t, d), yb.dtype).at[ib.reshape(-1)].add(yb.reshape(-1, d)))(idx, y)


def setup_inputs(seed: int = 0) -> dict:
    key = jax.random.key(seed)
    ks = jax.random.split(key, 32)
    D = D_MODEL

    def nrm(k, shape, scale):
        return jax.random.normal(k, shape, F32) * scale

    u = jax.random.uniform(ks[20], (DEPTH, 2, LRU_W), F32, 0.9, 0.999)
    a = u ** (1.0 / LRU_C)
    return {
        "x": nrm(ks[0], (BATCH, SEQ, D), 1.0),
        "c": nrm(ks[1], (BATCH, D), 1.0),
        "ctx": nrm(ks[2], (BATCH, CTX_LEN, D), 1.0),
        "c_ctx": nrm(ks[3], (D,), 1.0),
        "w_mod": nrm(ks[4], (DEPTH, D, 6 * D), 0.5 * D ** -0.5),
        "b_mod": nrm(ks[5], (DEPTH, 6 * D), 0.02),
        "w_in": nrm(ks[6], (DEPTH, D, IN_W), D ** -0.5),
        "lam_q1": nrm(ks[7], (DEPTH, ATT_QK_DIM), 0.1),
        "lam_k1": nrm(ks[8], (DEPTH, ATT_QK_DIM), 0.1),
        "lam_q2": nrm(ks[9], (DEPTH, ATT_QK_DIM), 0.1),
        "lam_k2": nrm(ks[10], (DEPTH, ATT_QK_DIM), 0.1),
        "attn_norm_g": 1.0 + nrm(ks[11], (DEPTH, ATT_HEADS, ATT_V_DIM), 0.02),
        "conv_w": nrm(ks[12], (DEPTH, LRU_CONV_W, LRU_W), LRU_CONV_W ** -0.5),
        "conv_b": nrm(ks[13], (DEPTH, LRU_W), 0.02),
        "lru_wa": nrm(ks[14], (DEPTH, 2, LRU_BLOCKS, LRU_BLOCK_W, LRU_BLOCK_W), LRU_BLOCK_W ** -0.5),
        "lru_ba": nrm(ks[15], (DEPTH, 2, LRU_W), 0.02),
        "lru_wx": nrm(ks[16], (DEPTH, 2, LRU_BLOCKS, LRU_BLOCK_W, LRU_BLOCK_W), LRU_BLOCK_W ** -0.5),
        "lru_bx": nrm(ks[17], (DEPTH, 2, LRU_W), 0.02),
        "lru_lam": jnp.log(a) - jnp.log1p(-a),
        "w_out": nrm(ks[18], (DEPTH, D, D), DEEPNORM_BETA * D ** -0.5),
        "ln1_g": 1.0 + nrm(ks[19], (DEPTH, D), 0.02),
        "ln1_b": nrm(ks[21], (DEPTH, D), 0.02),
        "w_router": nrm(ks[22], (DEPTH, D, N_EXPERTS), D ** -0.5),
        "w_gate": nrm(ks[23], (DEPTH, N_EXPERTS, D, D_EXPERT), D ** -0.5),
        "w_up": nrm(ks[24], (DEPTH, N_EXPERTS, D, D_EXPERT), D ** -0.5),
        "w_down": nrm(ks[25], (DEPTH, N_EXPERTS, D_EXPERT, D), DEEPNORM_BETA * D_EXPERT ** -0.5),
        "ln2_g": 1.0 + nrm(ks[26], (DEPTH, D), 0.02),
        "ln2_b": nrm(ks[27], (DEPTH, D), 0.02),
    }


def reference(x, c, ctx, c_ctx, w_mod, b_mod, w_in, lam_q1, lam_k1, lam_q2, lam_k2, attn_norm_g,
              conv_w, conv_b, lru_wa, lru_ba, lru_wx, lru_bx, lru_lam, w_out, ln1_g, ln1_b,
              w_router, w_gate, w_up, w_down, ln2_g, ln2_b):
    b, n, _ = x.shape
    rows = n // GRID_W
    pos = jnp.arange(rows * GRID_W)
    row = (pos // GRID_W).astype(F32)
    col = (pos % GRID_W).astype(F32)
    silu_c = jax.nn.silu(c)
    silu_cc = jax.nn.silu(c_ctx)

    for l in range(DEPTH):
        last = l == DEPTH - 1
        lam_init = 0.8 - 0.6 * math.exp(-0.3 * l)
        lam = (jnp.exp(jnp.sum(lam_q1[l].astype(F32) * lam_k1[l].astype(F32)))
               - jnp.exp(jnp.sum(lam_q2[l].astype(F32) * lam_k2[l].astype(F32))) + lam_init)

        sh1, sc1, g1, sh2, sc2, g2 = jnp.split((silu_c @ w_mod[l] + b_mod[l])[:, None, :], 6, axis=-1)
        csh1, csc1, cg1, csh2, csc2, cg2 = jnp.split(silu_cc @ w_mod[l] + b_mod[l], 6)

        hl = _ln(x) * (1.0 + sc1) + sh1
        hc = _ln(ctx) * (1.0 + csc1) + csh1
        ql, kl, vl, xl, gl, fl = _split_proj(hl @ w_in[l])
        qc, kc, vc, xc, gc, fc = _split_proj(hc @ w_in[l])

        ql = _axial_rope(ql, row, col)
        kl = _axial_rope(kl, row, col)
        k_all = jnp.concatenate([kl, kc], axis=1)
        v_all = jnp.concatenate([vl, vc], axis=1)
        att_l = _head_rms(_latent_diff_attention(ql, k_all, v_all, lam), attn_norm_g[l], lam_init)

        ul = _centred_dwconv(xl, conv_w[l], conv_b[l])
        uc = _centred_dwconv(xc, conv_w[l], conv_b[l])
        lru_l = jnp.zeros((b, n, LRU_W), F32)
        lru_c = jnp.zeros((b, ctx.shape[1], LRU_W), F32)
        for d, rev in enumerate((False, True)):
            h_c = _rglru(uc, lru_wa[l, d], lru_ba[l, d], lru_wx[l, d], lru_bx[l, d], lru_lam[l, d],
                         jnp.zeros((b, LRU_W), F32), rev)
            h_fin = h_c[:, 0] if rev else h_c[:, -1]
            h_l = _rglru(ul, lru_wa[l, d], lru_ba[l, d], lru_wx[l, d], lru_bx[l, d], lru_lam[l, d], h_fin, rev)
            lru_l = lru_l + h_l
            lru_c = lru_c + h_c
        y_l = (lru_l * jax.nn.gelu(gl.astype(F32))).astype(x.dtype)

        mix_l = jnp.concatenate([att_l, y_l, _fourier(fl)], axis=-1) @ w_out[l]
        x_new = _ln_affine(DEEPNORM_ALPHA * x + g1 * mix_l, ln1_g[l], ln1_b[l])
        hl2 = _ln(x_new) * (1.0 + sc2) + sh2
        x_new = _ln_affine(DEEPNORM_ALPHA * x_new + g2 * _expert_choice(hl2, w_router[l], w_gate[l], w_up[l], w_down[l]),
                           ln2_g[l], ln2_b[l])

        if not last:
            att_c = _head_rms(_diff_attend(qc, kc, vc, lam), attn_norm_g[l], lam_init)
            y_c = (lru_c * jax.nn.gelu(gc.astype(F32))).astype(ctx.dtype)
            mix_c = jnp.concatenate([att_c, y_c, _fourier(fc)], axis=-1) @ w_out[l]
            ctx_new = _ln_affine(DEEPNORM_ALPHA * ctx + cg1 * mix_c, ln1_g[l], ln1_b[l])
            hc2 = _ln(ctx_new) * (1.0 + csc2) + csh2
            ctx = _ln_affine(DEEPNORM_ALPHA * ctx_new + cg2 * _expert_choice(hc2, w_router[l], w_gate[l], w_up[l], w_down[l]),
                             ln2_g[l], ln2_b[l])
        x = x_new
    return x
```

```python
import functools
import math

import numpy as np
import jax
import jax.numpy as jnp
from jax import lax
from jax.experimental import pallas as pl
from jax.experimental.pallas import tpu as pltpu

F32 = jnp.float32
BF16 = jnp.bfloat16
I32 = jnp.int32

GRID_W = 64
QK_DIM = 64
N_HEADS = 4
HEAD_V = 128
QK_W = 512
ATT_W = 512
LRU_W = 256
LRU_BLOCKS = 4
LRU_BLOCK_W = 64
LRU_C = 8.0
FNET_W = 256
FNET_GROUP_W = 64
IN_W = 2304
ROPE_BASE = 10000.0
N_EXPERTS = 16
CAPACITY_FACTOR = 2
LN_EPS = 1e-5
RMS_EPS = 1e-6
GELU_C = math.sqrt(2.0 / math.pi)

LANES = 128
SUBLANES = 8
VMEM_LIMIT = 56 << 20


def _params(sem, vmem=VMEM_LIMIT):
    return pltpu.CompilerParams(dimension_semantics=sem, vmem_limit_bytes=vmem)


def _ln(x):
    mu = jnp.mean(x, axis=-1, keepdims=True)
    xc = x - mu
    var = jnp.mean(xc * xc, axis=-1, keepdims=True)
    return xc * lax.rsqrt(var + LN_EPS)


def _largest_divisor(n, candidates):
    for c in candidates:
        if c <= n and n % c == 0:
            return c
    return n


def _mod_kernel(c_ref, w_ref, b_ref, o_ref):
    c = c_ref[...]
    s = c * jax.nn.sigmoid(c)
    o_ref[0] = jnp.dot(s, w_ref[0], precision=lax.Precision.HIGHEST, preferred_element_type=F32) + b_ref[0]


def _modulation(cc, w_mod, b_mod):
    depth, d, d6 = w_mod.shape
    rows = cc.shape[0]
    tn = 1024
    return pl.pallas_call(
        _mod_kernel,
        grid=(depth, d6 // tn),
        in_specs=[
            pl.BlockSpec((rows, d), lambda l, n: (0, 0)),
            pl.BlockSpec((1, d, tn), lambda l, n: (l, 0, n)),
            pl.BlockSpec((1, 1, tn), lambda l, n: (l, 0, n)),
        ],
        out_specs=pl.BlockSpec((1, rows, tn), lambda l, n: (l, 0, n)),
        out_shape=jax.ShapeDtypeStruct((depth, rows, d6), F32),
        compiler_params=_params(("arbitrary", "arbitrary")),
        name="modulation",
    )(cc, w_mod, b_mod.reshape(depth, 1, d6))


def _in_kernel(x_ref, mod_ref, w_ref, cos_ref, sin_ref, cs_ref,
               qx_ref, k_ref, v_ref, ux_ref, gg_ref, gc_ref, gs_ref, *, d):
    x = x_ref[0]
    m = mod_ref[0]
    h = _ln(x) * (1.0 + m[:, d:2 * d]) + m[:, 0:d]
    z = jnp.dot(h.astype(BF16), w_ref[0], preferred_element_type=F32)

    tb = x.shape[0]
    cos = cos_ref[...]
    sin = sin_ref[...]
    lane = lax.broadcasted_iota(I32, (tb, LANES), 1)
    first_half = (lane & 31) < 16
    low_map = lane < QK_DIM

    def rope(t):
        partner = jnp.where(first_half, pltpu.roll(t, LANES - 16, 1), pltpu.roll(t, 16, 1))
        return t * cos + partner * sin

    for p in range(N_HEADS):
        qp = rope(z[:, p * LANES:(p + 1) * LANES] * (QK_DIM ** -0.5))
        qx_ref[0, :, (2 * p) * LANES:(2 * p + 1) * LANES] = jnp.where(low_map, qp, 0.0).astype(BF16)
        qx_ref[0, :, (2 * p + 1) * LANES:(2 * p + 2) * LANES] = jnp.where(low_map, 0.0, qp).astype(BF16)
        kp = rope(z[:, QK_W + p * LANES:QK_W + (p + 1) * LANES])
        k_ref[0, :, p * LANES:(p + 1) * LANES] = kp.astype(BF16)

    o = 2 * QK_W
    v_ref[0] = z[:, o:o + ATT_W].astype(BF16)
    o += ATT_W
    ux_ref[0] = z[:, o:o + LRU_W]
    o += LRU_W
    g = z[:, o:o + LRU_W]
    gg_ref[0] = 0.5 * g * (1.0 + jnp.tanh(GELU_C * (g + 0.044715 * (g * g * g))))
    o += LRU_W
    uf = z[:, o:o + FNET_W].astype(BF16)
    gcs = jnp.dot(uf, cs_ref[...], preferred_element_type=F32)
    gc_ref[...] = gcs[:, :FNET_W].astype(BF16)
    gs_ref[...] = gcs[:, FNET_W:].astype(BF16)


def _in_proj(xc, mod2, w_in_bf, l, cos_t, sin_t, cs, *, nbl, tb):
    b, tt, d = xc.shape
    nbt = tt // tb
    kern = functools.partial(_in_kernel, d=d)
    tok = lambda w: pl.BlockSpec((1, tb, w), lambda i, j: (i, j, 0))
    return pl.pallas_call(
        kern,
        grid=(b, nbt),
        in_specs=[
            tok(d),
            pl.BlockSpec((1, 1, 6 * d), lambda i, j: (2 * i + (j >= nbl).astype(I32), 0, 0)),
            pl.BlockSpec((1, d, IN_W), lambda i, j: (l, 0, 0)),
            pl.BlockSpec((tb, LANES), lambda i, j: (j, 0)),
            pl.BlockSpec((tb, LANES), lambda i, j: (j, 0)),
            pl.BlockSpec((FNET_W, 2 * FNET_W), lambda i, j: (0, 0)),
        ],
        out_specs=[
            tok(2 * QK_W), tok(QK_W), tok(ATT_W), tok(LRU_W), tok(LRU_W),
            pl.BlockSpec((tb, FNET_W), lambda i, j: (j, i)),
            pl.BlockSpec((tb, FNET_W), lambda i, j: (j, i)),
        ],
        out_shape=[
            jax.ShapeDtypeStruct((b, tt, 2 * QK_W), BF16),
            jax.ShapeDtypeStruct((b, tt, QK_W), BF16),
            jax.ShapeDtypeStruct((b, tt, ATT_W), BF16),
            jax.ShapeDtypeStruct((b, tt, LRU_W), F32),
            jax.ShapeDtypeStruct((b, tt, LRU_W), F32),
            jax.ShapeDtypeStruct((tt, b * FNET_W), BF16),
            jax.ShapeDtypeStruct((tt, b * FNET_W), BF16),
        ],
        compiler_params=_params(("arbitrary", "arbitrary")),
        name="in_proj",
    )(xc, mod2, w_in_bf, cos_t, sin_t, cs)


def _attn_kernel(qx_ref, k_ref, v_ref, lamv_ref, lin_ref, g_ref, o_ref, *, n, nbl):
    j = pl.program_id(2)
    lv = lamv_ref[0]
    lam_init = lin_ref[0][:, 0:1]
    lam = (jnp.exp(jnp.sum(lv[0:1] * lv[1:2], axis=1, keepdims=True))
           - jnp.exp(jnp.sum(lv[2:3] * lv[3:4], axis=1, keepdims=True)) + lam_init)
    gain = g_ref[0] * (1.0 - lam_init)
    nt = (((1,), (1,)), ((), ()))

    def attend(kk, vv):
        def one_map(q):
            s = lax.dot_general(q, kk, nt, preferred_element_type=F32)
            p = jnp.exp(s - jnp.max(s, axis=-1, keepdims=True))
            den = jnp.sum(p, axis=-1, keepdims=True)
            return jnp.dot(p.astype(BF16), vv, preferred_element_type=F32) / den

        o = one_map(qx_ref[0, :, :LANES]) - lam * one_map(qx_ref[0, :, LANES:])
        r = o * lax.rsqrt(jnp.mean(o * o, axis=-1, keepdims=True) + RMS_EPS) * gain
        o_ref[0] = r.astype(BF16)

    @pl.when(j < nbl)
    def _():
        attend(k_ref[0], v_ref[0])

    @pl.when(j >= nbl)
    def _():
        attend(k_ref[0, n:, :], v_ref[0, n:, :])


def _attention(qx, k, v, lamv, lin, gain, l, *, n, nbl, tb):
    b, tt, _ = k.shape
    nbt = tt // tb
    kern = functools.partial(_attn_kernel, n=n, nbl=nbl)
    return pl.pallas_call(
        kern,
        grid=(b, N_HEADS, nbt),
        in_specs=[
            pl.BlockSpec((1, tb, 2 * LANES), lambda i, h, j: (i, j, h)),
            pl.BlockSpec((1, tt, LANES), lambda i, h, j: (i, 0, h)),
            pl.BlockSpec((1, tt, LANES), lambda i, h, j: (i, 0, h)),
            pl.BlockSpec((1, 4, QK_DIM), lambda i, h, j: (l, 0, 0)),
            pl.BlockSpec((1, 1, LANES), lambda i, h, j: (l, 0, 0)),
            pl.BlockSpec((1, 1, HEAD_V), lambda i, h, j: (l * N_HEADS + h, 0, 0)),
        ],
        out_specs=pl.BlockSpec((1, tb, LANES), lambda i, h, j: (i, j, h)),
        out_shape=jax.ShapeDtypeStruct((b, tt, ATT_W), BF16),
        compiler_params=_params(("arbitrary", "arbitrary", "arbitrary")),
        name="diff_attention",
    )(qx, k, v, lamv, lin, gain)


def _lru_kernel(ux_ref, gg_ref, cw_ref, cb_ref, wg_ref, bg_ref, lam_ref, y_ref,
                a_f, b_f, a_b, b_b, *, n, cn, r):
    tt = n + cn
    w = LANES
    cw = cw_ref[0]
    cb = cb_ref[0]
    bg = bg_ref[0]
    neg_lam = -lam_ref[0]
    softplus = jnp.maximum(neg_lam, 0.0) + jnp.log(1.0 + jnp.exp(-jnp.abs(neg_lam)))
    row8 = lax.broadcasted_iota(I32, (r, w), 0) & (SUBLANES - 1)
    ext_rows = r + 2 * SUBLANES

    def local_scan(a, bb, reverse):
        for s in (1, 2, 4):
            if reverse:
                a_sh = pltpu.roll(a, r - s, 0)
                b_sh = pltpu.roll(bb, r - s, 0)
                valid = row8 < SUBLANES - s
            else:
                a_sh = pltpu.roll(a, s, 0)
                b_sh = pltpu.roll(bb, s, 0)
                valid = row8 >= s
            bb = jnp.where(valid, a * b_sh + bb, bb)
            a = jnp.where(valid, a * a_sh, a)
        return a, bb

    def gates_chunk(c, carry):
        r0 = pl.multiple_of(c * r, r)
        seg_start = jnp.logical_or(r0 == 0, r0 == n)
        seg_end = jnp.logical_or(r0 + r == n, r0 + r == tt)
        main = ux_ref[0, pl.ds(r0, r), :]
        prev = ux_ref[0, pl.ds(pl.multiple_of(jnp.maximum(r0 - SUBLANES, 0), SUBLANES), SUBLANES), :]
        nxt = ux_ref[0, pl.ds(pl.multiple_of(jnp.minimum(r0 + r, tt - SUBLANES), SUBLANES), SUBLANES), :]
        prev = jnp.where(seg_start, 0.0, prev)
        nxt = jnp.where(seg_end, 0.0, nxt)
        ext = jnp.concatenate([prev, main, nxt], axis=0)
        u = cb
        for t in range(4):
            sh = (2 - t) % ext_rows
            win = ext if sh == 0 else pltpu.roll(ext, sh, 0)
            u = u + cw[t:t + 1, :] * win[SUBLANES:SUBLANES + r, :]
        zz = jnp.dot(u.astype(BF16), wg_ref[0], preferred_element_type=F32) + bg
        for dr, (a_s, b_s) in enumerate(((a_f, b_f), (a_b, b_b))):
            rg = jax.nn.sigmoid(zz[:, (2 * dr) * w:(2 * dr + 1) * w])
            ig = jax.nn.sigmoid(zz[:, (2 * dr + 1) * w:(2 * dr + 2) * w])
            a = jnp.exp(-LRU_C * rg * softplus[dr:dr + 1, :])
            bb = jnp.sqrt(1.0 - a * a) * ig * u
            a, bb = local_scan(a, bb, reverse=(dr == 1))
            a_s[pl.ds(r0, r), :] = a
            b_s[pl.ds(r0, r), :] = bb
        return carry

    lax.fori_loop(0, tt // r, gates_chunk, 0)

    def seg_scan(first_tile, ntiles, cf, cbk):
        def body(i, carry):
            cf, cbk = carry
            rf = pl.multiple_of((first_tile + i) * SUBLANES, SUBLANES)
            hf = b_f[pl.ds(rf, SUBLANES), :] + a_f[pl.ds(rf, SUBLANES), :] * cf
            b_f[pl.ds(rf, SUBLANES), :] = hf
            rb = pl.multiple_of((first_tile + ntiles - 1 - i) * SUBLANES, SUBLANES)
            hb = b_b[pl.ds(rb, SUBLANES), :] + a_b[pl.ds(rb, SUBLANES), :] * cbk
            b_b[pl.ds(rb, SUBLANES), :] = hb
            return hf[SUBLANES - 1:SUBLANES, :], hb[0:1, :]
        return lax.fori_loop(0, ntiles, body, (cf, cbk))

    zero = jnp.zeros((1, w), F32)
    cf, cbk = seg_scan(n // SUBLANES, cn // SUBLANES, zero, zero)
    seg_scan(0, n // SUBLANES, cf, cbk)

    def out_chunk(c, carry):
        r0 = pl.multiple_of(c * r, r)
        y = (b_f[pl.ds(r0, r), :] + b_b[pl.ds(r0, r), :]) * gg_ref[0, pl.ds(r0, r), :]
        y_ref[0, pl.ds(r0, r), :] = y.astype(BF16)
        return carry

    lax.fori_loop(0, tt // r, out_chunk, 0)


def _rglru(ux, gg, conv_w2, conv_b2, wgate, bgate, lam2, l, *, n, cn, tb):
    b, tt, _ = ux.shape
    kern = functools.partial(_lru_kernel, n=n, cn=cn, r=tb)
    half = lambda: pl.BlockSpec((1, tt, LANES), lambda i, hh: (i, 0, hh))
    return pl.pallas_call(
        kern,
        grid=(b, 2),
        in_specs=[
            half(), half(),
            pl.BlockSpec((1, 4, LANES), lambda i, hh: (2 * l + hh, 0, 0)),
            pl.BlockSpec((1, 1, LANES), lambda i, hh: (2 * l + hh, 0, 0)),
            pl.BlockSpec((1, LANES, 4 * LANES), lambda i, hh: (2 * l + hh, 0, 0)),
            pl.BlockSpec((1, 1, 4 * LANES), lambda i, hh: (2 * l + hh, 0, 0)),
            pl.BlockSpec((1, 2, LANES), lambda i, hh: (2 * l + hh, 0, 0)),
        ],
        out_specs=half(),
        out_shape=jax.ShapeDtypeStruct((b, tt, LRU_W), BF16),
        scratch_shapes=[pltpu.VMEM((tt, LANES), F32)] * 4,
        compiler_params=_params(("arbitrary", "arbitrary")),
        name="rglru",
    )(ux, gg, conv_w2, conv_b2, wgate, bgate, lam2)


def _dft_kernel(c_ref, s_ref, gc_ref, gs_ref, o_ref, acc_ref):
    kk = pl.program_id(2)

    @pl.when(kk == 0)
    def _():
        acc_ref[...] = jnp.zeros_like(acc_ref)

    acc_ref[...] += (jnp.dot(c_ref[...], gc_ref[...], preferred_element_type=F32)
                     + jnp.dot(s_ref[...], gs_ref[...], preferred_element_type=F32))

    @pl.when(kk == pl.num_programs(2) - 1)
    def _():
        o_ref[...] = acc_ref[...].astype(BF16)


def _position_dft(cmat, smat, gc, gs):
    tt = cmat.shape[0]
    nn = gc.shape[1]
    tm = _largest_divisor(tt, (1088, 640, 512, 256, 128))
    tk = _largest_divisor(tt, (2176, 640, 512, 256, 128))
    tn = _largest_divisor(nn, (512, 256))
    return pl.pallas_call(
        _dft_kernel,
        grid=(tt // tm, nn // tn, tt // tk),
        in_specs=[
            pl.BlockSpec((tm, tk), lambda i, j, k: (i, k)),
            pl.BlockSpec((tm, tk), lambda i, j, k: (i, k)),
            pl.BlockSpec((tk, tn), lambda i, j, k: (k, j)),
            pl.BlockSpec((tk, tn), lambda i, j, k: (k, j)),
        ],
        out_specs=pl.BlockSpec((tm, tn), lambda i, j, k: (i, j)),
        out_shape=jax.ShapeDtypeStruct((tt, nn), BF16),
        scratch_shapes=[pltpu.VMEM((tm, tn), F32)],
        compiler_params=_params(("arbitrary", "arbitrary", "arbitrary")),
        name="position_dft",
    )(cmat, smat, gc, gs)


def _out_kernel(att_ref, y_ref, f_ref, x_ref, mod_ref, w_ref, g_ref, b_ref, wr_ref,
                x1_ref, h2_ref, lg_ref, *, d, alpha):
    w = w_ref[0]
    mix = (jnp.dot(att_ref[0], w[0:ATT_W], preferred_element_type=F32)
           + jnp.dot(y_ref[0], w[ATT_W:ATT_W + LRU_W], preferred_element_type=F32)
           + jnp.dot(f_ref[...], w[ATT_W + LRU_W:], preferred_element_type=F32))
    m = mod_ref[0]
    x1 = _ln(alpha * x_ref[0] + m[:, 2 * d:3 * d] * mix) * g_ref[0] + b_ref[0]
    x1_ref[0] = x1
    h2 = (_ln(x1) * (1.0 + m[:, 4 * d:5 * d]) + m[:, 3 * d:4 * d]).astype(BF16)
    h2_ref[0] = h2
    lg_ref[0] = lax.dot_general(wr_ref[0], h2, (((1,), (1,)), ((), ())), preferred_element_type=F32)


def _out_proj(att, y, fy, xc, mod2, w_out_bf, ln_g, ln_b, wr_t, l, *, nbl, tb, alpha):
    b, tt, d = xc.shape
    nbt = tt // tb
    kern = functools.partial(_out_kernel, d=d, alpha=alpha)
    tok = lambda w: pl.BlockSpec((1, tb, w), lambda i, j: (i, j, 0))
    return pl.pallas_call(
        kern,
        grid=(b, nbt),
        in_specs=[
            tok(ATT_W), tok(LRU_W),
            pl.BlockSpec((tb, FNET_W), lambda i, j: (j, i)),
            tok(d),
            pl.BlockSpec((1, 1, 6 * d), lambda i, j: (2 * i + (j >= nbl).astype(I32), 0, 0)),
            pl.BlockSpec((1, d, d), lambda i, j: (l, 0, 0)),
            pl.BlockSpec((1, 1, d), lambda i, j: (l, 0, 0)),
            pl.BlockSpec((1, 1, d), lambda i, j: (l, 0, 0)),
            pl.BlockSpec((1, N_EXPERTS, d), lambda i, j: (l, 0, 0)),
        ],
        out_specs=[tok(d), tok(d), pl.BlockSpec((1, N_EXPERTS, tb), lambda i, j: (i, 0, j))],
        out_shape=[
            jax.ShapeDtypeStruct((b, tt, d), F32),
            jax.ShapeDtypeStruct((b, tt, d), BF16),
            jax.ShapeDtypeStruct((b, N_EXPERTS, tt), F32),
        ],
        compiler_params=_params(("arbitrary", "arbitrary")),
        name="out_proj",
    )(att, y, fy, xc, mod2, w_out_bf, ln_g, ln_b, wr_t)


def _route_kernel(lg_ref, pos_ref, gate_ref, *, segments):
    lg = lg_ref[0]
    e = jnp.exp(lg - jnp.max(lg, axis=0, keepdims=True))
    s = e / jnp.sum(e, axis=0, keepdims=True)
    ri = lax.broadcasted_iota(I32, (LANES, LANES), 0)
    ci = lax.broadcasted_iota(I32, (LANES, LANES), 1)
    strict_upper = jnp.where(ri < ci, 1.0, 0.0).astype(BF16)

    for lo, t, cap, base in segments:
        ss = s[:, lo:lo + t]
        bits = pltpu.bitcast(ss, I32)
        capf = float(cap)

        def search(i, thr, bits=bits, capf=capf):
            cand = thr | jnp.left_shift(jnp.int32(1), 30 - i)
            cnt = jnp.sum(jnp.where(bits >= cand, 1.0, 0.0), axis=1, keepdims=True)
            return jnp.where(cnt >= capf, cand, thr)

        thr = lax.fori_loop(0, 31, search, jnp.zeros((N_EXPERTS, 1), I32))
        need = capf - jnp.sum(jnp.where(bits > thr, 1.0, 0.0), axis=1, keepdims=True)
        off_eq = jnp.zeros((N_EXPERTS, 1), F32)
        off_sel = jnp.zeros((N_EXPERTS, 1), F32)
        for c in range(t // LANES):
            sl = slice(c * LANES, (c + 1) * LANES)
            bits_c = bits[:, sl]
            eq = bits_c == thr
            eq_c = jnp.where(eq, 1.0, 0.0)
            rank_eq = jnp.dot(eq_c.astype(BF16), strict_upper, preferred_element_type=F32) + off_eq
            off_eq = off_eq + jnp.sum(eq_c, axis=1, keepdims=True)
            sel = jnp.logical_or(bits_c > thr, jnp.logical_and(eq, rank_eq < need))
            sel_c = jnp.where(sel, 1.0, 0.0)
            slot = jnp.dot(sel_c.astype(BF16), strict_upper, preferred_element_type=F32) + off_sel
            off_sel = off_sel + jnp.sum(sel_c, axis=1, keepdims=True)
            osl = slice(lo + c * LANES, lo + (c + 1) * LANES)
            pos_ref[0, :, osl] = jnp.where(sel, slot.astype(I32) + base, -1)
            gate_ref[0, :, osl] = jnp.where(sel, ss[:, sl], 0.0)


def _route(logits_t, *, segments):
    b, ne, tt = logits_t.shape
    kern = functools.partial(_route_kernel, segments=segments)
    blk = lambda: pl.BlockSpec((1, ne, tt), lambda i: (i, 0, 0))
    return pl.pallas_call(
        kern,
        grid=(b,),
        in_specs=[blk()],
        out_specs=[blk(), blk()],
        out_shape=[jax.ShapeDtypeStruct((b, ne, tt), I32), jax.ShapeDtypeStruct((b, ne, tt), F32)],
        compiler_params=_params(("arbitrary",)),
        name="route",
    )(logits_t)


def _gather_kernel(h_ref, pos_ref, xs_ref, *, slots):
    e = pl.program_id(1)
    prow = pos_ref[0, pl.ds(e, 1), :]
    tt = prow.shape[1]
    slot = lax.broadcasted_iota(I32, (slots, tt), 0)
    onehot = jnp.where(slot == prow, 1.0, 0.0).astype(BF16)
    xs_ref[0] = jnp.dot(onehot, h_ref[0], preferred_element_type=F32).astype(BF16)


def _gather(h2, pos, *, slots):
    b, tt, d = h2.shape
    kern = functools.partial(_gather_kernel, slots=slots)
    return pl.pallas_call(
        kern,
        grid=(b, N_EXPERTS),
        in_specs=[
            pl.BlockSpec((1, tt, d), lambda i, e: (i, 0, 0)),
            pl.BlockSpec((1, N_EXPERTS, tt), lambda i, e: (i, 0, 0)),
        ],
        out_specs=pl.BlockSpec((1, slots, d), lambda i, e: (e, i, 0)),
        out_shape=jax.ShapeDtypeStruct((N_EXPERTS, b * slots, d), BF16),
        compiler_params=_params(("arbitrary", "arbitrary")),
        name="moe_gather",
    )(h2, pos)


def _ffn_kernel(xs_ref, wg_ref, wu_ref, wd_ref, ys_ref, *, fchunk):
    xs = xs_ref[0]
    f = wg_ref.shape[-1]
    acc = None
    for c in range(f // fchunk):
        sl = slice(c * fchunk, (c + 1) * fchunk)
        a = jnp.dot(xs, wg_ref[0, 0, :, sl], preferred_element_type=F32)
        u = jnp.dot(xs, wu_ref[0, 0, :, sl], preferred_element_type=F32)
        hm = (a * jax.nn.sigmoid(a) * u).astype(BF16)
        y = jnp.dot(hm, wd_ref[0, 0, sl, :], preferred_element_type=F32)
        acc = y if acc is None else acc + y
    ys_ref[0] = acc.astype(BF16)


def _expert_ffn(xs, wg, wu, wd, l, *, slots):
    ne, rows, d = xs.shape
    f = wg.shape[-1]
    nb = rows // slots
    kern = functools.partial(_ffn_kernel, fchunk=_largest_divisor(f, (512,)))
    return pl.pallas_call(
        kern,
        grid=(ne, nb),
        in_specs=[
            pl.BlockSpec((1, slots, d), lambda e, i: (e, i, 0)),
            pl.BlockSpec((1, 1, d, f), lambda e, i: (l, e, 0, 0)),
            pl.BlockSpec((1, 1, d, f), lambda e, i: (l, e, 0, 0)),
            pl.BlockSpec((1, 1, f, d), lambda e, i: (l, e, 0, 0)),
        ],
        out_specs=pl.BlockSpec((1, slots, d), lambda e, i: (e, i, 0)),
        out_shape=jax.ShapeDtypeStruct((ne, rows, d), BF16),
        compiler_params=_params(("arbitrary", "arbitrary")),
        name="expert_ffn",
    )(xs, wg, wu, wd)


def _combine_kernel(ys_ref, pos_ref, gate_ref, o_ref, *, slots):
    e = pl.program_id(2)

    @pl.when(e == 0)
    def _():
        o_ref[...] = jnp.zeros_like(o_ref)

    prow = pos_ref[0, pl.ds(e, 1), :]
    grow = gate_ref[0, pl.ds(e, 1), :]
    tm = prow.shape[1]
    slot = lax.broadcasted_iota(I32, (slots, tm), 0)
    pg = jnp.where(slot == prow, grow, 0.0).astype(BF16)
    o_ref[0] += lax.dot_general(pg, ys_ref[0], (((0,), (0,)), ((), ())), preferred_element_type=F32)


def _combine(ys, pos, gate, *, slots, d):
    b, ne, tt = pos.shape
    tm = _largest_divisor(tt, (2176, 640, 512, 256, 128))
    kern = functools.partial(_combine_kernel, slots=slots)
    return pl.pallas_call(
        kern,
        grid=(b, tt // tm, ne),
        in_specs=[
            pl.BlockSpec((1, slots, d), lambda i, t, e: (e, i, 0)),
            pl.BlockSpec((1, ne, tm), lambda i, t, e: (i, 0, t)),
            pl.BlockSpec((1, ne, tm), lambda i, t, e: (i, 0, t)),
        ],
        out_specs=pl.BlockSpec((1, tm, d), lambda i, t, e: (i, t, 0)),
        out_shape=jax.ShapeDtypeStruct((b, tt, d), F32),
        compiler_params=_params(("arbitrary", "arbitrary", "arbitrary")),
        name="moe_combine",
    )(ys, pos, gate)


def _final_kernel(x1_ref, moe_ref, mod_ref, g_ref, b_ref, o_ref, *, d, alpha):
    m = mod_ref[0]
    o_ref[0] = _ln(alpha * x1_ref[0] + m[:, 5 * d:6 * d] * moe_ref[0]) * g_ref[0] + b_ref[0]


def _final_norm(x1, moe, mod2, ln_g, ln_b, l, *, nbl, tb, alpha):
    b, tt, d = x1.shape
    kern = functools.partial(_final_kernel, d=d, alpha=alpha)
    tok = lambda: pl.BlockSpec((1, tb, d), lambda i, j: (i, j, 0))
    return pl.pallas_call(
        kern,
        grid=(b, tt // tb),
        in_specs=[
            tok(), tok(),
            pl.BlockSpec((1, 1, 6 * d), lambda i, j: (2 * i + (j >= nbl).astype(I32), 0, 0)),
            pl.BlockSpec((1, 1, d), lambda i, j: (l, 0, 0)),
            pl.BlockSpec((1, 1, d), lambda i, j: (l, 0, 0)),
        ],
        out_specs=tok(),
        out_shape=jax.ShapeDtypeStruct((b, tt, d), F32),
        compiler_params=_params(("arbitrary", "arbitrary")),
        name="final_norm",
    )(x1, moe, mod2, ln_g, ln_b)


def _rope_tables(n, cn):
    lane = np.arange(LANES)
    within = lane % QK_DIM
    use_col = (within // 32) == 1
    first_half = (within % 32) < 16
    inv = ROPE_BASE ** (-(within % 16).astype(np.float64) / 16.0)
    pos = np.arange(n)
    coord = np.where(use_col[None, :], (pos % GRID_W)[:, None], (pos // GRID_W)[:, None]).astype(np.float32)
    ang = jnp.asarray(coord) * jnp.asarray(inv.astype(np.float32))[None, :]
    cos = jnp.cos(ang)
    sin = jnp.where(jnp.asarray(first_half)[None, :], -jnp.sin(ang), jnp.sin(ang))
    cos = jnp.concatenate([cos, jnp.ones((cn, LANES), F32)], axis=0)
    sin = jnp.concatenate([sin, jnp.zeros((cn, LANES), F32)], axis=0)
    return cos, sin


def _channel_dft():
    idx = np.arange(FNET_W)
    same = (idx[:, None] // FNET_GROUP_W) == (idx[None, :] // FNET_GROUP_W)
    ang = 2.0 * np.pi * ((idx[:, None] % FNET_GROUP_W) * (idx[None, :] % FNET_GROUP_W) % FNET_GROUP_W) / FNET_GROUP_W
    cs = np.concatenate([np.where(same, np.cos(ang), 0.0), np.where(same, np.sin(ang), 0.0)], axis=1)
    return jnp.asarray(cs.astype(np.float32)).astype(BF16)


def _position_dft_mats(n, cn):
    tt = n + cn
    r = jnp.arange(tt, dtype=I32)
    is_ctx = r >= n
    pos = jnp.where(is_ctx, r - n, r)
    length = jnp.where(is_ctx, cn, n)
    same = is_ctx[:, None] == is_ctx[None, :]
    lm = length[:, None]
    phase = ((pos[:, None] * pos[None, :]) % lm).astype(F32) / lm.astype(F32) * (2.0 * math.pi)
    scale = lax.rsqrt(lm.astype(F32) * FNET_GROUP_W)
    cmat = jnp.where(same, jnp.cos(phase) * scale, 0.0).astype(BF16)
    smat = jnp.where(same, -jnp.sin(phase) * scale, 0.0).astype(BF16)
    return cmat, smat


def _block_diag_gates(wa, wx):
    depth = wa.shape[0]

    def dense(wb):
        eye = jnp.eye(LRU_BLOCKS, dtype=wb.dtype)
        return jnp.einsum('lncd,nm->lncmd', wb, eye).reshape(depth, LRU_W, LRU_W)

    halves = []
    for hh in range(2):
        sl = slice(hh * LANES, (hh + 1) * LANES)
        cols = [dense(wmat[:, dr])[:, sl, sl] for dr in range(2) for wmat in (wa, wx)]
        halves.append(jnp.concatenate(cols, axis=-1))
    return jnp.stack(halves, axis=1).reshape(depth * 2, LANES, 4 * LANES).astype(BF16)


def _gate_bias(ba, bx):
    depth = ba.shape[0]
    halves = []
    for hh in range(2):
        sl = slice(hh * LANES, (hh + 1) * LANES)
        halves.append(jnp.concatenate([bvec[:, dr, sl] for dr in range(2) for bvec in (ba, bx)], axis=-1))
    return jnp.stack(halves, axis=1).reshape(depth * 2, 1, 4 * LANES)


def _split_halves(a):
    depth, r, _ = a.shape
    return a.reshape(depth, r, 2, LANES).transpose(0, 2, 1, 3).reshape(depth * 2, r, LANES)


def kernel(x, c, ctx, c_ctx, w_mod, b_mod, w_in, lam_q1, lam_k1, lam_q2, lam_k2, attn_norm_g, conv_w, conv_b, lru_wa, lru_ba, lru_wx, lru_bx, lru_lam, w_out, ln1_g, ln1_b, w_router, w_gate, w_up, w_down, ln2_g, ln2_b):
    b, n, d = x.shape
    cn = ctx.shape[1]
    depth = w_mod.shape[0]
    tt = n + cn
    tb = _largest_divisor(math.gcd(n, cn), (256, 128))
    assert n % tb == 0 and cn % tb == 0 and n % GRID_W == 0 and tt % LANES == 0
    nbl = n // tb
    cap_l = CAPACITY_FACTOR * n // N_EXPERTS
    cap_c = CAPACITY_FACTOR * cn // N_EXPERTS
    slots = cap_l + cap_c
    assert slots % 16 == 0
    alpha = (2 * depth) ** 0.25

    rows = -(-(b + 1) // SUBLANES) * SUBLANES
    cc = jnp.concatenate([c, c_ctx[None, :], jnp.zeros((rows - b - 1, d), F32)], axis=0)
    mod = _modulation(cc, w_mod, b_mod)

    cos_t, sin_t = _rope_tables(n, cn)
    cs = _channel_dft()
    cmat, smat = _position_dft_mats(n, cn)

    w_in_bf = w_in.astype(BF16)
    w_out_bf = w_out.astype(BF16)
    wg_bf = w_gate.astype(BF16)
    wu_bf = w_up.astype(BF16)
    wd_bf = w_down.astype(BF16)
    wr_t = jnp.swapaxes(w_router, 1, 2).astype(BF16)
    lamv = jnp.stack([lam_q1, lam_k1, lam_q2, lam_k2], axis=1).astype(F32)
    lam_init = np.array([0.8 - 0.6 * math.exp(-0.3 * l) for l in range(depth)], np.float32)
    lin = jnp.asarray(np.broadcast_to(lam_init[:, None, None], (depth, 1, LANES)).copy())
    gain = attn_norm_g.reshape(depth * N_HEADS, 1, HEAD_V)
    conv_w2 = _split_halves(conv_w)
    conv_b2 = _split_halves(conv_b[:, None, :])
    lam2 = _split_halves(lru_lam)
    wgate = _block_diag_gates(lru_wa, lru_wx)
    bgate = _gate_bias(lru_ba, lru_bx)
    ln1g, ln1b = ln1_g[:, None, :], ln1_b[:, None, :]
    ln2g, ln2b = ln2_g[:, None, :], ln2_b[:, None, :]
    segments = ((0, n, cap_l, 0), (n, cn, cap_c, cap_l))

    xc = jnp.concatenate([x, ctx], axis=1)
    for l in range(depth):
        ml = mod[l]
        mod2 = jnp.stack([ml[:b], jnp.broadcast_to(ml[b][None, :], (b, 6 * d))], axis=1).reshape(2 * b, 1, 6 * d)
        qx, k, v, ux, gg, gc, gs = _in_proj(xc, mod2, w_in_bf, l, cos_t, sin_t, cs, nbl=nbl, tb=tb)
        att = _attention(qx, k, v, lamv, lin, gain, l, n=n, nbl=nbl, tb=tb)
        y = _rglru(ux, gg, conv_w2, conv_b2, wgate, bgate, lam2, l, n=n, cn=cn, tb=tb)
        fy = _position_dft(cmat, smat, gc, gs)
        x1, h2, logits_t = _out_proj(att, y, fy, xc, mod2, w_out_bf, ln1g, ln1b, wr_t, l,
                                     nbl=nbl, tb=tb, alpha=alpha)
        pos, gate = _route(logits_t, segments=segments)
        xs = _gather(h2, pos, slots=slots)
        ys = _expert_ffn(xs, wg_bf, wu_bf, wd_bf, l, slots=slots)
        moe = _combine(ys, pos, gate, slots=slots, d=d)
        xc = _final_norm(x1, moe, mod2, ln2g, ln2b, l, nbl=nbl, tb=tb, alpha=alpha)
    return xc[:, :n]
```

```python
import functools
import math

import numpy as np
import jax
import jax.numpy as jnp
from jax import lax
from jax.experimental import pallas as pl
from jax.experimental.pallas import tpu as pltpu

F32 = jnp.float32
BF16 = jnp.bfloat16
I32 = jnp.int32

GRID_W = 64
QK_DIM = 64
N_HEADS = 4
HEAD_V = 128
QK_W = 512
ATT_W = 512
LRU_W = 256
LRU_BLOCKS = 4
LRU_BLOCK_W = 64
LRU_C = 8.0
FNET_W = 256
FNET_GROUP_W = 64
IN_W = 2304
ROPE_BASE = 10000.0
N_EXPERTS = 16
CAPACITY_FACTOR = 2
LN_EPS = 1e-5
RMS_EPS = 1e-6
GELU_C = math.sqrt(2.0 / math.pi)
LOG2E = math.log2(math.e)

LANES = 128
SUBLANES = 8
BF16_ROWS = 16
KEY_CHUNK = 512
TOKEN_TILE = 256
SLOT_WINDOW = 128
VMEM_LIMIT = 56 << 20


def _params(sem, vmem=VMEM_LIMIT):
    return pltpu.CompilerParams(dimension_semantics=sem, vmem_limit_bytes=vmem)


def _ln(x):
    mu = jnp.mean(x, axis=-1, keepdims=True)
    xc = x - mu
    var = jnp.mean(xc * xc, axis=-1, keepdims=True)
    return xc * lax.rsqrt(var + LN_EPS)


def _largest_divisor(n, candidates):
    for c in candidates:
        if c <= n and n % c == 0:
            return c
    return n


def _mod_kernel(c_ref, w_ref, b_ref, o_ref):
    c = c_ref[...]
    s = c * jax.nn.sigmoid(c)
    o_ref[0] = jnp.dot(s, w_ref[0], precision=lax.Precision.HIGHEST, preferred_element_type=F32) + b_ref[0]


def _modulation(cc, w_mod, b_mod):
    depth, d, d6 = w_mod.shape
    rows = cc.shape[0]
    tn = 1024
    return pl.pallas_call(
        _mod_kernel,
        grid=(depth, d6 // tn),
        in_specs=[
            pl.BlockSpec((rows, d), lambda l, n: (0, 0)),
            pl.BlockSpec((1, d, tn), lambda l, n: (l, 0, n)),
            pl.BlockSpec((1, 1, tn), lambda l, n: (l, 0, n)),
        ],
        out_specs=pl.BlockSpec((1, rows, tn), lambda l, n: (l, 0, n)),
        out_shape=jax.ShapeDtypeStruct((depth, rows, d6), F32),
        compiler_params=_params(("arbitrary", "arbitrary")),
        name="modulation",
    )(cc, w_mod, b_mod.reshape(depth, 1, d6))


def _in_kernel(x_ref, mod_ref, w_ref, cos_ref, sin_ref, cs_ref,
               qx_ref, k_ref, v_ref, ux_ref, gg_ref, gc_ref, gs_ref, *, d):
    x = x_ref[0]
    m = mod_ref[0]
    h = _ln(x) * (1.0 + m[:, d:2 * d]) + m[:, 0:d]
    z = jnp.dot(h.astype(BF16), w_ref[0], preferred_element_type=F32)

    tb = x.shape[0]
    cos = cos_ref[...]
    sin = sin_ref[...]
    lane = lax.broadcasted_iota(I32, (tb, LANES), 1)
    first_half = (lane & 31) < 16
    low_map = lane < QK_DIM

    def rope(t):
        partner = jnp.where(first_half, pltpu.roll(t, LANES - 16, 1), pltpu.roll(t, 16, 1))
        return t * cos + partner * sin

    for p in range(N_HEADS):
        qp = rope(z[:, p * LANES:(p + 1) * LANES] * (QK_DIM ** -0.5 * LOG2E))
        qx_ref[0, :, (2 * p) * LANES:(2 * p + 1) * LANES] = jnp.where(low_map, qp, 0.0).astype(BF16)
        qx_ref[0, :, (2 * p + 1) * LANES:(2 * p + 2) * LANES] = jnp.where(low_map, 0.0, qp).astype(BF16)
        kp = rope(z[:, QK_W + p * LANES:QK_W + (p + 1) * LANES])
        k_ref[0, :, p * LANES:(p + 1) * LANES] = kp.astype(BF16)

    o = 2 * QK_W
    v_ref[0] = z[:, o:o + ATT_W].astype(BF16)
    o += ATT_W
    ux_ref[0] = z[:, o:o + LRU_W]
    o += LRU_W
    g = z[:, o:o + LRU_W]
    gg_ref[0] = 0.5 * g * (1.0 + jnp.tanh(GELU_C * (g + 0.044715 * (g * g * g))))
    o += LRU_W
    uf = z[:, o:o + FNET_W].astype(BF16)
    gcs = jnp.dot(uf, cs_ref[...], preferred_element_type=F32)
    gc_ref[...] = gcs[:, :FNET_W].astype(BF16)
    gs_ref[...] = gcs[:, FNET_W:].astype(BF16)


def _in_proj(xc, mod2, w_in_bf, l, cos_t, sin_t, cs, *, nbl, tb):
    b, tt, d = xc.shape
    nbt = tt // tb
    kern = functools.partial(_in_kernel, d=d)
    tok = lambda w: pl.BlockSpec((1, tb, w), lambda i, j: (i, j, 0))
    return pl.pallas_call(
        kern,
        grid=(b, nbt),
        in_specs=[
            tok(d),
            pl.BlockSpec((1, 1, 6 * d), lambda i, j: (2 * i + (j >= nbl).astype(I32), 0, 0)),
            pl.BlockSpec((1, d, IN_W), lambda i, j: (l, 0, 0)),
            pl.BlockSpec((tb, LANES), lambda i, j: (j, 0)),
            pl.BlockSpec((tb, LANES), lambda i, j: (j, 0)),
            pl.BlockSpec((FNET_W, 2 * FNET_W), lambda i, j: (0, 0)),
        ],
        out_specs=[
            tok(2 * QK_W), tok(QK_W), tok(ATT_W), tok(LRU_W), tok(LRU_W),
            pl.BlockSpec((tb, FNET_W), lambda i, j: (j, i)),
            pl.BlockSpec((tb, FNET_W), lambda i, j: (j, i)),
        ],
        out_shape=[
            jax.ShapeDtypeStruct((b, tt, 2 * QK_W), BF16),
            jax.ShapeDtypeStruct((b, tt, QK_W), BF16),
            jax.ShapeDtypeStruct((b, tt, ATT_W), BF16),
            jax.ShapeDtypeStruct((b, tt, LRU_W), F32),
            jax.ShapeDtypeStruct((b, tt, LRU_W), F32),
            jax.ShapeDtypeStruct((tt, b * FNET_W), BF16),
            jax.ShapeDtypeStruct((tt, b * FNET_W), BF16),
        ],
        compiler_params=_params(("arbitrary", "arbitrary")),
        name="in_proj",
    )(xc, mod2, w_in_bf, cos_t, sin_t, cs)


def _attn_kernel(qx_ref, k_ref, v_ref, lamv_ref, lin_ref, g_ref, o_ref, vt_ref, s0_ref, s1_ref, p0_ref, p1_ref,
                 *, n, nbl):
    j = pl.program_id(2)
    lv = lamv_ref[0]
    lam_init = lin_ref[0][:, 0:1]
    lam = (jnp.exp(jnp.sum(lv[0:1] * lv[1:2], axis=1, keepdims=True))
           - jnp.exp(jnp.sum(lv[2:3] * lv[3:4], axis=1, keepdims=True)) + lam_init)
    gain = g_ref[0] * (1.0 - lam_init)
    nt = (((1,), (1,)), ((), ()))
    tt = k_ref.shape[1]

    @pl.when(j == 0)
    def _():
        vt_ref[0:HEAD_V, :] = v_ref[0].astype(F32).T.astype(BF16)
        row = lax.broadcasted_iota(I32, (BF16_ROWS, tt), 0)
        vt_ref[HEAD_V:, :] = jnp.where(row == 0, 1.0, 0.0).astype(BF16)

    def attend(lo):
        chunks = [(c, min(KEY_CHUNK, tt - c)) for c in range(lo, tt, KEY_CHUNK)]
        qs = (qx_ref[0, :, :LANES], qx_ref[0, :, LANES:])

        s_refs = (s0_ref, s1_ref)
        p_refs = (p0_ref, p1_ref)

        def scores(mp, c, w, m):
            st = lax.dot_general(k_ref[0, c:c + w, :], qs[mp], nt, preferred_element_type=F32)
            s_refs[mp][c:c + w, :] = st
            mc = jnp.max(st, axis=0, keepdims=True)
            return mc if m is None else jnp.maximum(m, mc)

        def exponentials(mp, c, w, m):
            p_refs[mp][c:c + w, :] = jnp.exp2(s_refs[mp][c:c + w, :] - m).astype(BF16)

        def weighted_values(mp, c, w, acc):
            part = jnp.dot(vt_ref[:, c:c + w], p_refs[mp][c:c + w, :], preferred_element_type=F32)
            return part if acc is None else acc + part

        m0 = m1 = acc0 = acc1 = None
        for c, w in chunks:
            m0 = scores(0, c, w, m0)
        for c, w in chunks:
            exponentials(0, c, w, m0)
            m1 = scores(1, c, w, m1)
        for c, w in chunks:
            acc0 = weighted_values(0, c, w, acc0)
            exponentials(1, c, w, m1)
        for c, w in chunks:
            acc1 = weighted_values(1, c, w, acc1)

        ot = (acc0[0:HEAD_V] / acc0[HEAD_V:HEAD_V + 1]) - lam * (acc1[0:HEAD_V] / acc1[HEAD_V:HEAD_V + 1])
        rt = ot * lax.rsqrt(jnp.mean(ot * ot, axis=0, keepdims=True) + RMS_EPS)
        o_ref[0] = (rt.T * gain).astype(BF16)

    @pl.when(j < nbl)
    def _():
        attend(0)

    @pl.when(j >= nbl)
    def _():
        attend(n)


def _attention(qx, k, v, lamv, lin, gain, l, *, n, nbl, tb):
    b, tt, _ = k.shape
    nbt = tt // tb
    kern = functools.partial(_attn_kernel, n=n, nbl=nbl)
    return pl.pallas_call(
        kern,
        grid=(b, N_HEADS, nbt),
        in_specs=[
            pl.BlockSpec((1, tb, 2 * LANES), lambda i, h, j: (i, j, h)),
            pl.BlockSpec((1, tt, LANES), lambda i, h, j: (i, 0, h)),
            pl.BlockSpec((1, tt, LANES), lambda i, h, j: (i, 0, h)),
            pl.BlockSpec((1, 4, QK_DIM), lambda i, h, j: (l, 0, 0)),
            pl.BlockSpec((1, 1, LANES), lambda i, h, j: (l, 0, 0)),
            pl.BlockSpec((1, 1, HEAD_V), lambda i, h, j: (l * N_HEADS + h, 0, 0)),
        ],
        out_specs=pl.BlockSpec((1, tb, LANES), lambda i, h, j: (i, j, h)),
        out_shape=jax.ShapeDtypeStruct((b, tt, ATT_W), BF16),
        scratch_shapes=[pltpu.VMEM((HEAD_V + BF16_ROWS, tt), BF16),
                        pltpu.VMEM((tt, tb), F32), pltpu.VMEM((tt, tb), F32),
                        pltpu.VMEM((tt, tb), BF16), pltpu.VMEM((tt, tb), BF16)],
        compiler_params=_params(("arbitrary", "arbitrary", "arbitrary")),
        name="diff_attention",
    )(qx, k, v, lamv, lin, gain)


def _lru_kernel(ux_ref, gg_ref, cw_ref, cb_ref, wg_ref, bg_ref, lam_ref, y_ref,
                a_f, b_f, a_b, b_b, *, n, cn, r):
    tt = n + cn
    w = LANES
    cw = cw_ref[0]
    cb = cb_ref[0]
    bg = bg_ref[0]
    neg_lam = -lam_ref[0]
    softplus = jnp.maximum(neg_lam, 0.0) + jnp.log(1.0 + jnp.exp(-jnp.abs(neg_lam)))
    row8 = lax.broadcasted_iota(I32, (r, w), 0) & (SUBLANES - 1)
    ext_rows = r + 2 * SUBLANES

    def local_scan(a, bb, reverse):
        for s in (1, 2, 4):
            if reverse:
                a_sh = pltpu.roll(a, r - s, 0)
                b_sh = pltpu.roll(bb, r - s, 0)
                valid = row8 < SUBLANES - s
            else:
                a_sh = pltpu.roll(a, s, 0)
                b_sh = pltpu.roll(bb, s, 0)
                valid = row8 >= s
            bb = jnp.where(valid, a * b_sh + bb, bb)
            a = jnp.where(valid, a * a_sh, a)
        return a, bb

    def gates_chunk(c, carry):
        r0 = pl.multiple_of(c * r, r)
        seg_start = jnp.logical_or(r0 == 0, r0 == n)
        seg_end = jnp.logical_or(r0 + r == n, r0 + r == tt)
        main = ux_ref[0, pl.ds(r0, r), :]
        prev = ux_ref[0, pl.ds(pl.multiple_of(jnp.maximum(r0 - SUBLANES, 0), SUBLANES), SUBLANES), :]
        nxt = ux_ref[0, pl.ds(pl.multiple_of(jnp.minimum(r0 + r, tt - SUBLANES), SUBLANES), SUBLANES), :]
        prev = jnp.where(seg_start, 0.0, prev)
        nxt = jnp.where(seg_end, 0.0, nxt)
        ext = jnp.concatenate([prev, main, nxt], axis=0)
        u = cb
        for t in range(4):
            sh = (2 - t) % ext_rows
            win = ext if sh == 0 else pltpu.roll(ext, sh, 0)
            u = u + cw[t:t + 1, :] * win[SUBLANES:SUBLANES + r, :]
        zz = jnp.dot(u.astype(BF16), wg_ref[0], preferred_element_type=F32) + bg
        for dr, (a_s, b_s) in enumerate(((a_f, b_f), (a_b, b_b))):
            rg = jax.nn.sigmoid(zz[:, (2 * dr) * w:(2 * dr + 1) * w])
            ig = jax.nn.sigmoid(zz[:, (2 * dr + 1) * w:(2 * dr + 2) * w])
            a = jnp.exp(-LRU_C * rg * softplus[dr:dr + 1, :])
            bb = jnp.sqrt(1.0 - a * a) * ig * u
            a, bb = local_scan(a, bb, reverse=(dr == 1))
            a_s[pl.ds(r0, r), :] = a
            b_s[pl.ds(r0, r), :] = bb
        return carry

    lax.fori_loop(0, tt // r, gates_chunk, 0)

    def seg_scan(first_tile, ntiles, cf, cbk):
        def body(i, carry):
            cf, cbk = carry
            rf = pl.multiple_of((first_tile + i) * SUBLANES, SUBLANES)
            hf = b_f[pl.ds(rf, SUBLANES), :] + a_f[pl.ds(rf, SUBLANES), :] * cf
            b_f[pl.ds(rf, SUBLANES), :] = hf
            rb = pl.multiple_of((first_tile + ntiles - 1 - i) * SUBLANES, SUBLANES)
            hb = b_b[pl.ds(rb, SUBLANES), :] + a_b[pl.ds(rb, SUBLANES), :] * cbk
            b_b[pl.ds(rb, SUBLANES), :] = hb
            return hf[SUBLANES - 1:SUBLANES, :], hb[0:1, :]
        return lax.fori_loop(0, ntiles, body, (cf, cbk))

    zero = jnp.zeros((1, w), F32)
    cf, cbk = seg_scan(n // SUBLANES, cn // SUBLANES, zero, zero)
    seg_scan(0, n // SUBLANES, cf, cbk)

    def out_chunk(c, carry):
        r0 = pl.multiple_of(c * r, r)
        y = (b_f[pl.ds(r0, r), :] + b_b[pl.ds(r0, r), :]) * gg_ref[0, pl.ds(r0, r), :]
        y_ref[0, pl.ds(r0, r), :] = y.astype(BF16)
        return carry

    lax.fori_loop(0, tt // r, out_chunk, 0)


def _rglru(ux, gg, conv_w2, conv_b2, wgate, bgate, lam2, l, *, n, cn, tb):
    b, tt, _ = ux.shape
    kern = functools.partial(_lru_kernel, n=n, cn=cn, r=tb)
    half = lambda: pl.BlockSpec((1, tt, LANES), lambda i, hh: (i, 0, hh))
    return pl.pallas_call(
        kern,
        grid=(b, 2),
        in_specs=[
            half(), half(),
            pl.BlockSpec((1, 4, LANES), lambda i, hh: (2 * l + hh, 0, 0)),
            pl.BlockSpec((1, 1, LANES), lambda i, hh: (2 * l + hh, 0, 0)),
            pl.BlockSpec((1, LANES, 4 * LANES), lambda i, hh: (2 * l + hh, 0, 0)),
            pl.BlockSpec((1, 1, 4 * LANES), lambda i, hh: (2 * l + hh, 0, 0)),
            pl.BlockSpec((1, 2, LANES), lambda i, hh: (2 * l + hh, 0, 0)),
        ],
        out_specs=half(),
        out_shape=jax.ShapeDtypeStruct((b, tt, LRU_W), BF16),
        scratch_shapes=[pltpu.VMEM((tt, LANES), F32)] * 4,
        compiler_params=_params(("arbitrary", "arbitrary")),
        name="rglru",
    )(ux, gg, conv_w2, conv_b2, wgate, bgate, lam2)


def _dft_kernel(c_ref, s_ref, gc_ref, gs_ref, o_ref, acc_ref):
    kk = pl.program_id(2)

    @pl.when(kk == 0)
    def _():
        acc_ref[...] = jnp.zeros_like(acc_ref)

    acc_ref[...] += (jnp.dot(c_ref[...], gc_ref[...], preferred_element_type=F32)
                     + jnp.dot(s_ref[...], gs_ref[...], preferred_element_type=F32))

    @pl.when(kk == pl.num_programs(2) - 1)
    def _():
        o_ref[...] = acc_ref[...].astype(BF16)


def _position_dft(cmat, smat, gc, gs):
    tt = cmat.shape[0]
    nn = gc.shape[1]
    tm = _largest_divisor(tt, (1088, 640, 512, 256, 128))
    tk = _largest_divisor(tt, (2176, 640, 512, 256, 128))
    tn = _largest_divisor(nn, (512, 256))
    return pl.pallas_call(
        _dft_kernel,
        grid=(tt // tm, nn // tn, tt // tk),
        in_specs=[
            pl.BlockSpec((tm, tk), lambda i, j, k: (i, k)),
            pl.BlockSpec((tm, tk), lambda i, j, k: (i, k)),
            pl.BlockSpec((tk, tn), lambda i, j, k: (k, j)),
            pl.BlockSpec((tk, tn), lambda i, j, k: (k, j)),
        ],
        out_specs=pl.BlockSpec((tm, tn), lambda i, j, k: (i, j)),
        out_shape=jax.ShapeDtypeStruct((tt, nn), BF16),
        scratch_shapes=[pltpu.VMEM((tm, tn), F32)],
        compiler_params=_params(("arbitrary", "arbitrary", "arbitrary")),
        name="position_dft",
    )(cmat, smat, gc, gs)


def _out_kernel(att_ref, y_ref, f_ref, x_ref, mod_ref, w_ref, g_ref, b_ref, wr_ref,
                x1_ref, h2_ref, lg_ref, *, d, alpha):
    w = w_ref[0]
    mix = (jnp.dot(att_ref[0], w[0:ATT_W], preferred_element_type=F32)
           + jnp.dot(y_ref[0], w[ATT_W:ATT_W + LRU_W], preferred_element_type=F32)
           + jnp.dot(f_ref[...], w[ATT_W + LRU_W:], preferred_element_type=F32))
    m = mod_ref[0]
    x1 = _ln(alpha * x_ref[0] + m[:, 2 * d:3 * d] * mix) * g_ref[0] + b_ref[0]
    x1_ref[0] = x1
    h2 = (_ln(x1) * (1.0 + m[:, 4 * d:5 * d]) + m[:, 3 * d:4 * d]).astype(BF16)
    h2_ref[0] = h2
    lg_ref[0] = lax.dot_general(wr_ref[0], h2, (((1,), (1,)), ((), ())), preferred_element_type=F32)


def _out_proj(att, y, fy, xc, mod2, w_out_bf, ln_g, ln_b, wr_t, l, *, nbl, tb, alpha):
    b, tt, d = xc.shape
    nbt = tt // tb
    kern = functools.partial(_out_kernel, d=d, alpha=alpha)
    tok = lambda w: pl.BlockSpec((1, tb, w), lambda i, j: (i, j, 0))
    return pl.pallas_call(
        kern,
        grid=(b, nbt),
        in_specs=[
            tok(ATT_W), tok(LRU_W),
            pl.BlockSpec((tb, FNET_W), lambda i, j: (j, i)),
            tok(d),
            pl.BlockSpec((1, 1, 6 * d), lambda i, j: (2 * i + (j >= nbl).astype(I32), 0, 0)),
            pl.BlockSpec((1, d, d), lambda i, j: (l, 0, 0)),
            pl.BlockSpec((1, 1, d), lambda i, j: (l, 0, 0)),
            pl.BlockSpec((1, 1, d), lambda i, j: (l, 0, 0)),
            pl.BlockSpec((1, N_EXPERTS, d), lambda i, j: (l, 0, 0)),
        ],
        out_specs=[tok(d), tok(d), pl.BlockSpec((1, N_EXPERTS, tb), lambda i, j: (i, 0, j))],
        out_shape=[
            jax.ShapeDtypeStruct((b, tt, d), F32),
            jax.ShapeDtypeStruct((b, tt, d), BF16),
            jax.ShapeDtypeStruct((b, N_EXPERTS, tt), F32),
        ],
        compiler_params=_params(("arbitrary", "arbitrary")),
        name="out_proj",
    )(att, y, fy, xc, mod2, w_out_bf, ln_g, ln_b, wr_t)


def _route_kernel(lg_ref, pos_ref, gate_ref, starts_ref, *, segments, n_slots):
    lg = lg_ref[0]
    lane_id = lax.broadcasted_iota(I32, (N_EXPERTS, LANES), 1)
    starts = jnp.zeros((N_EXPERTS, LANES), I32)
    chunks_per_tile = TOKEN_TILE // LANES
    e = jnp.exp(lg - jnp.max(lg, axis=0, keepdims=True))
    s = e / jnp.sum(e, axis=0, keepdims=True)
    ri = lax.broadcasted_iota(I32, (LANES, LANES), 0)
    ci = lax.broadcasted_iota(I32, (LANES, LANES), 1)
    strict_upper = jnp.where(ri < ci, 1.0, 0.0).astype(BF16)

    for lo, t, cap, base in segments:
        ss = s[:, lo:lo + t]
        bits = pltpu.bitcast(ss, I32)
        capf = float(cap)

        def search(i, thr, bits=bits, capf=capf):
            cand = thr | jnp.left_shift(jnp.int32(1), 30 - i)
            cnt = jnp.sum(jnp.where(bits >= cand, 1.0, 0.0), axis=1, keepdims=True)
            return jnp.where(cnt >= capf, cand, thr)

        thr = lax.fori_loop(0, 31, search, jnp.zeros((N_EXPERTS, 1), I32))
        need = capf - jnp.sum(jnp.where(bits > thr, 1.0, 0.0), axis=1, keepdims=True)
        off_eq = jnp.zeros((N_EXPERTS, 1), F32)
        off_sel = jnp.zeros((N_EXPERTS, 1), F32)
        for c in range(t // LANES):
            sl = slice(c * LANES, (c + 1) * LANES)
            tile, sub = divmod(lo // LANES + c, chunks_per_tile)
            if sub == 0:
                starts = jnp.where(lane_id == tile, off_sel.astype(I32) + base, starts)
            bits_c = bits[:, sl]
            eq = bits_c == thr
            eq_c = jnp.where(eq, 1.0, 0.0)
            rank_eq = jnp.dot(eq_c.astype(BF16), strict_upper, preferred_element_type=F32) + off_eq
            off_eq = off_eq + jnp.sum(eq_c, axis=1, keepdims=True)
            sel = jnp.logical_or(bits_c > thr, jnp.logical_and(eq, rank_eq < need))
            sel_c = jnp.where(sel, 1.0, 0.0)
            slot = jnp.dot(sel_c.astype(BF16), strict_upper, preferred_element_type=F32) + off_sel
            off_sel = off_sel + jnp.sum(sel_c, axis=1, keepdims=True)
            osl = slice(sub * LANES, (sub + 1) * LANES)
            pos_ref[0, tile, :, osl] = jnp.where(sel, slot.astype(I32) + base, -1)
            gate_ref[0, tile, :, osl] = jnp.where(sel, ss[:, sl], 0.0)

    n_tiles = lg.shape[1] // TOKEN_TILE
    starts_ref[0] = jnp.where(lane_id == n_tiles, n_slots, starts)


def _route(logits_t, *, segments, n_slots):
    b, ne, tt = logits_t.shape
    nt = tt // TOKEN_TILE
    assert nt < LANES and all(lo % TOKEN_TILE == 0 and t % TOKEN_TILE == 0 for lo, t, _, _ in segments)
    kern = functools.partial(_route_kernel, segments=segments, n_slots=n_slots)
    tiled = lambda: pl.BlockSpec((1, nt, ne, TOKEN_TILE), lambda i: (i, 0, 0, 0))
    return pl.pallas_call(
        kern,
        grid=(b,),
        in_specs=[pl.BlockSpec((1, ne, tt), lambda i: (i, 0, 0))],
        out_specs=[tiled(), tiled(), pl.BlockSpec((1, ne, LANES), lambda i: (i, 0, 0))],
        out_shape=[jax.ShapeDtypeStruct((b, nt, ne, TOKEN_TILE), I32),
                   jax.ShapeDtypeStruct((b, nt, ne, TOKEN_TILE), F32),
                   jax.ShapeDtypeStruct((b, ne, LANES), I32)],
        compiler_params=_params(("arbitrary",)),
        name="route",
    )(logits_t)


def _window_start(starts_ref, base_idx, tile, w, win, limit):
    c0 = starts_ref[base_idx + tile]
    lo = ((c0 >> 4) << 4) + w * win
    return lo, pl.multiple_of(jnp.minimum(lo, limit), BF16_ROWS)


def _window_count(starts_ref, base_idx, tiles, win):
    nw = jnp.int32(1)
    for tile in tiles:
        c0 = starts_ref[base_idx + tile]
        c1 = starts_ref[base_idx + tile + 1]
        nw = jnp.maximum(nw, (c1 - ((c0 >> 4) << 4) + win - 1) // win)
    return nw


def _gather_kernel(starts_ref, h_ref, pos_ref, xs_ref, acc_ref, *, slots, win, group):
    b = pl.program_id(0)
    eg = pl.program_id(1)
    nt = pos_ref.shape[1]
    acc_ref[...] = jnp.zeros_like(acc_ref)
    rel = lax.broadcasted_iota(I32, (win, TOKEN_TILE), 0)
    bases = [(b * N_EXPERTS + eg * group + i) * LANES for i in range(group)]
    nw = jnp.int32(1)
    for i in range(group):
        nw = jnp.maximum(nw, _window_count(starts_ref, bases[i], range(nt), win))

    def window_pass(w, carry):
        for tile in range(nt):
            onehots, offs = [], []
            for i in range(group):
                lo, start = _window_start(starts_ref, bases[i], tile, w, win, slots)
                prow = pos_ref[0, tile, pl.ds(eg * group + i, 1), :] - start
                onehots.append(jnp.where(rel == prow, 1.0, 0.0).astype(BF16))
                offs.append(start)
            res = jnp.dot(jnp.concatenate(onehots, axis=0), h_ref[0, tile * TOKEN_TILE:(tile + 1) * TOKEN_TILE, :],
                          preferred_element_type=F32)
            for i in range(group):
                acc_ref[i, pl.ds(offs[i], win), :] += res[i * win:(i + 1) * win]
        return carry

    lax.fori_loop(0, nw, window_pass, 0)
    xs_ref[...] = acc_ref[:, 0:slots, :].astype(BF16)


def _gather(starts, h2, pos, *, slots, win, group=4):
    b, tt, d = h2.shape
    nt = tt // TOKEN_TILE
    kern = functools.partial(_gather_kernel, slots=slots, win=win, group=group)
    return pl.pallas_call(
        kern,
        grid_spec=pltpu.PrefetchScalarGridSpec(
            num_scalar_prefetch=1,
            grid=(b, N_EXPERTS // group),
            in_specs=[
                pl.BlockSpec((1, tt, d), lambda i, g, s: (i, 0, 0)),
                pl.BlockSpec((1, nt, N_EXPERTS, TOKEN_TILE), lambda i, g, s: (i, 0, 0, 0)),
            ],
            out_specs=pl.BlockSpec((group, slots, d), lambda i, g, s: (g, i, 0)),
            scratch_shapes=[pltpu.VMEM((group, slots + win, d), F32)],
        ),
        out_shape=jax.ShapeDtypeStruct((N_EXPERTS, b * slots, d), BF16),
        compiler_params=_params(("arbitrary", "arbitrary")),
        name="moe_gather",
    )(starts, h2, pos)


def _ffn_kernel(xs_ref, wg_ref, wu_ref, wd_ref, ys_ref, *, fchunk):
    xs = xs_ref[0]
    f = wg_ref.shape[-1]
    acc = None
    for c in range(f // fchunk):
        sl = slice(c * fchunk, (c + 1) * fchunk)
        a = jnp.dot(xs, wg_ref[0, 0, :, sl], preferred_element_type=F32)
        u = jnp.dot(xs, wu_ref[0, 0, :, sl], preferred_element_type=F32)
        hm = (a * jax.nn.sigmoid(a) * u).astype(BF16)
        y = jnp.dot(hm, wd_ref[0, 0, sl, :], preferred_element_type=F32)
        acc = y if acc is None else acc + y
    ys_ref[0] = acc.astype(BF16)


def _expert_ffn(xs, wg, wu, wd, l, *, slots):
    ne, rows, d = xs.shape
    f = wg.shape[-1]
    nb = rows // slots
    kern = functools.partial(_ffn_kernel, fchunk=_largest_divisor(f, (512,)))
    return pl.pallas_call(
        kern,
        grid=(ne, nb),
        in_specs=[
            pl.BlockSpec((1, slots, d), lambda e, i: (e, i, 0)),
            pl.BlockSpec((1, 1, d, f), lambda e, i: (l, e, 0, 0)),
            pl.BlockSpec((1, 1, d, f), lambda e, i: (l, e, 0, 0)),
            pl.BlockSpec((1, 1, f, d), lambda e, i: (l, e, 0, 0)),
        ],
        out_specs=pl.BlockSpec((1, slots, d), lambda e, i: (e, i, 0)),
        out_shape=jax.ShapeDtypeStruct((ne, rows, d), BF16),
        compiler_params=_params(("arbitrary", "arbitrary")),
        name="expert_ffn",
    )(xs, wg, wu, wd)


def _combine_kernel(starts_ref, ys_ref, pos_ref, gate_ref, x1_ref, mod_ref, g_ref, b_ref, o_ref,
                    *, slots, win, d, alpha):
    b = pl.program_id(0)
    tile = pl.program_id(1)
    rel = lax.broadcasted_iota(I32, (win, TOKEN_TILE), 0)
    bases = [(b * N_EXPERTS + e) * LANES for e in range(N_EXPERTS)]
    nw = jnp.int32(1)
    for e in range(N_EXPERTS):
        c0 = starts_ref[bases[e] + tile]
        c1 = starts_ref[bases[e] + tile + 1]
        nw = jnp.maximum(nw, (c1 - ((c0 >> 4) << 4) + win - 1) // win)

    def window_pass(w, moe):
        gated, rows = [], []
        for e in range(N_EXPERTS):
            lo, start = _window_start(starts_ref, bases[e], tile, w, win, slots - win)
            prow = pos_ref[0, 0, e:e + 1, :]
            prow = jnp.where(jnp.logical_and(prow >= lo, prow < lo + win), prow - start, -1)
            gated.append(jnp.where(rel == prow, gate_ref[0, 0, e:e + 1, :], 0.0).astype(BF16))
            rows.append(ys_ref[e, pl.ds(start, win), :])
        return moe + lax.dot_general(jnp.concatenate(gated, axis=0), jnp.concatenate(rows, axis=0),
                                     (((0,), (0,)), ((), ())), preferred_element_type=F32)

    moe = lax.fori_loop(0, nw, window_pass, jnp.zeros((TOKEN_TILE, d), F32))
    m = mod_ref[0]
    o_ref[0] = _ln(alpha * x1_ref[0] + m[:, 5 * d:6 * d] * moe) * g_ref[0] + b_ref[0]


def _combine(starts, ys, pos, gate, x1, mod2, ln_g, ln_b, l, *, slots, win, n_lat_tiles, alpha, out_tokens):
    b, _, d = x1.shape
    kern = functools.partial(_combine_kernel, slots=slots, win=win, d=d, alpha=alpha)
    tiled = lambda: pl.BlockSpec((1, 1, N_EXPERTS, TOKEN_TILE), lambda i, t, s: (i, t, 0, 0))
    tok = lambda: pl.BlockSpec((1, TOKEN_TILE, d), lambda i, t, s: (i, t, 0))
    return pl.pallas_call(
        kern,
        grid_spec=pltpu.PrefetchScalarGridSpec(
            num_scalar_prefetch=1,
            grid=(b, out_tokens // TOKEN_TILE),
            in_specs=[
                pl.BlockSpec((N_EXPERTS, slots, d), lambda i, t, s: (0, i, 0)),
                tiled(), tiled(), tok(),
                pl.BlockSpec((1, 1, 6 * d), lambda i, t, s: (2 * i + (t >= n_lat_tiles).astype(I32), 0, 0)),
                pl.BlockSpec((1, 1, d), lambda i, t, s: (l, 0, 0)),
                pl.BlockSpec((1, 1, d), lambda i, t, s: (l, 0, 0)),
            ],
            out_specs=tok(),
        ),
        out_shape=jax.ShapeDtypeStruct((b, out_tokens, d), F32),
        compiler_params=_params(("arbitrary", "arbitrary")),
        name="moe_combine",
    )(starts, ys, pos, gate, x1, mod2, ln_g, ln_b)


def _rope_tables(n, cn):
    lane = np.arange(LANES)
    within = lane % QK_DIM
    use_col = (within // 32) == 1
    first_half = (within % 32) < 16
    inv = ROPE_BASE ** (-(within % 16).astype(np.float64) / 16.0)
    pos = np.arange(n)
    coord = np.where(use_col[None, :], (pos % GRID_W)[:, None], (pos // GRID_W)[:, None]).astype(np.float32)
    ang = jnp.asarray(coord) * jnp.asarray(inv.astype(np.float32))[None, :]
    cos = jnp.cos(ang)
    sin = jnp.where(jnp.asarray(first_half)[None, :], -jnp.sin(ang), jnp.sin(ang))
    cos = jnp.concatenate([cos, jnp.ones((cn, LANES), F32)], axis=0)
    sin = jnp.concatenate([sin, jnp.zeros((cn, LANES), F32)], axis=0)
    return cos, sin


def _channel_dft():
    idx = np.arange(FNET_W)
    same = (idx[:, None] // FNET_GROUP_W) == (idx[None, :] // FNET_GROUP_W)
    ang = 2.0 * np.pi * ((idx[:, None] % FNET_GROUP_W) * (idx[None, :] % FNET_GROUP_W) % FNET_GROUP_W) / FNET_GROUP_W
    cs = np.concatenate([np.where(same, np.cos(ang), 0.0), np.where(same, np.sin(ang), 0.0)], axis=1)
    return jnp.asarray(cs.astype(np.float32)).astype(BF16)


def _position_dft_mats(n, cn):
    def segment(t):
        g = math.gcd(t, 64)
        kk = jnp.arange(t, dtype=I32)

        def table(m):
            ph = ((kk[:, None] * m[None, :]) % t).astype(F32) * (2.0 * math.pi / t)
            return jnp.cos(ph), jnp.sin(ph)

        ch, sh = table(jnp.arange(t // g, dtype=I32) * g)
        cl, sl = table(jnp.arange(g, dtype=I32))
        scale = 1.0 / math.sqrt(t * FNET_GROUP_W)
        cmat = (ch[:, :, None] * cl[:, None, :] - sh[:, :, None] * sl[:, None, :]).reshape(t, t) * scale
        smat = (sh[:, :, None] * cl[:, None, :] + ch[:, :, None] * sl[:, None, :]).reshape(t, t) * (-scale)
        return cmat.astype(BF16), smat.astype(BF16)

    def block_diag(a, c):
        top = jnp.concatenate([a, jnp.zeros((n, cn), BF16)], axis=1)
        bot = jnp.concatenate([jnp.zeros((cn, n), BF16), c], axis=1)
        return jnp.concatenate([top, bot], axis=0)

    (cl_, sl_), (cc_, sc_) = segment(n), segment(cn)
    return block_diag(cl_, cc_), block_diag(sl_, sc_)


def _block_diag_gates(wa, wx):
    depth = wa.shape[0]

    def dense(wb):
        eye = jnp.eye(LRU_BLOCKS, dtype=wb.dtype)
        return jnp.einsum('lncd,nm->lncmd', wb, eye).reshape(depth, LRU_W, LRU_W)

    halves = []
    for hh in range(2):
        sl = slice(hh * LANES, (hh + 1) * LANES)
        cols = [dense(wmat[:, dr])[:, sl, sl] for dr in range(2) for wmat in (wa, wx)]
        halves.append(jnp.concatenate(cols, axis=-1))
    return jnp.stack(halves, axis=1).reshape(depth * 2, LANES, 4 * LANES).astype(BF16)


def _gate_bias(ba, bx):
    depth = ba.shape[0]
    halves = []
    for hh in range(2):
        sl = slice(hh * LANES, (hh + 1) * LANES)
        halves.append(jnp.concatenate([bvec[:, dr, sl] for dr in range(2) for bvec in (ba, bx)], axis=-1))
    return jnp.stack(halves, axis=1).reshape(depth * 2, 1, 4 * LANES)


def _split_halves(a):
    depth, r, _ = a.shape
    return a.reshape(depth, r, 2, LANES).transpose(0, 2, 1, 3).reshape(depth * 2, r, LANES)


def kernel(x, c, ctx, c_ctx, w_mod, b_mod, w_in, lam_q1, lam_k1, lam_q2, lam_k2, attn_norm_g, conv_w, conv_b, lru_wa, lru_ba, lru_wx, lru_bx, lru_lam, w_out, ln1_g, ln1_b, w_router, w_gate, w_up, w_down, ln2_g, ln2_b):
    b, n, d = x.shape
    cn = ctx.shape[1]
    depth = w_mod.shape[0]
    tt = n + cn
    tb = _largest_divisor(math.gcd(n, cn), (256, 128))
    assert n % tb == 0 and cn % tb == 0 and n % GRID_W == 0 and tt % LANES == 0
    nbl = n // tb
    cap_l = CAPACITY_FACTOR * n // N_EXPERTS
    cap_c = CAPACITY_FACTOR * cn // N_EXPERTS
    slots = cap_l + cap_c
    win = min(SLOT_WINDOW, slots)
    assert slots % BF16_ROWS == 0 and win % BF16_ROWS == 0 and n % TOKEN_TILE == 0 and cn % TOKEN_TILE == 0
    alpha = (2 * depth) ** 0.25

    rows = -(-(b + 1) // SUBLANES) * SUBLANES
    cc = jnp.concatenate([c, c_ctx[None, :], jnp.zeros((rows - b - 1, d), F32)], axis=0)
    mod = _modulation(cc, w_mod, b_mod)

    cos_t, sin_t = _rope_tables(n, cn)
    cs = _channel_dft()
    cmat, smat = _position_dft_mats(n, cn)

    w_in_bf = w_in.astype(BF16)
    w_out_bf = w_out.astype(BF16)
    wg_bf = w_gate.astype(BF16)
    wu_bf = w_up.astype(BF16)
    wd_bf = w_down.astype(BF16)
    wr_t = jnp.swapaxes(w_router, 1, 2).astype(BF16)
    lamv = jnp.stack([lam_q1, lam_k1, lam_q2, lam_k2], axis=1).astype(F32)
    lam_init = np.array([0.8 - 0.6 * math.exp(-0.3 * l) for l in range(depth)], np.float32)
    lin = jnp.asarray(np.broadcast_to(lam_init[:, None, None], (depth, 1, LANES)).copy())
    gain = attn_norm_g.reshape(depth * N_HEADS, 1, HEAD_V)
    conv_w2 = _split_halves(conv_w)
    conv_b2 = _split_halves(conv_b[:, None, :])
    lam2 = _split_halves(lru_lam)
    wgate = _block_diag_gates(lru_wa, lru_wx)
    bgate = _gate_bias(lru_ba, lru_bx)
    ln1g, ln1b = ln1_g[:, None, :], ln1_b[:, None, :]
    ln2g, ln2b = ln2_g[:, None, :], ln2_b[:, None, :]
    segments = ((0, n, cap_l, 0), (n, cn, cap_c, cap_l))

    xc = jnp.concatenate([x, ctx], axis=1)
    for l in range(depth):
        ml = mod[l]
        mod2 = jnp.stack([ml[:b], jnp.broadcast_to(ml[b][None, :], (b, 6 * d))], axis=1).reshape(2 * b, 1, 6 * d)
        qx, k, v, ux, gg, gc, gs = _in_proj(xc, mod2, w_in_bf, l, cos_t, sin_t, cs, nbl=nbl, tb=tb)
        att = _attention(qx, k, v, lamv, lin, gain, l, n=n, nbl=nbl, tb=tb)
        y = _rglru(ux, gg, conv_w2, conv_b2, wgate, bgate, lam2, l, n=n, cn=cn, tb=tb)
        fy = _position_dft(cmat, smat, gc, gs)
        x1, h2, logits_t = _out_proj(att, y, fy, xc, mod2, w_out_bf, ln1g, ln1b, wr_t, l,
                                     nbl=nbl, tb=tb, alpha=alpha)
        pos, gate, starts = _route(logits_t, segments=segments, n_slots=slots)
        starts = starts.reshape(-1)
        xs = _gather(starts, h2, pos, slots=slots, win=win)
        ys = _expert_ffn(xs, wg_bf, wu_bf, wd_bf, l, slots=slots)
        xc = _combine(starts, ys, pos, gate, x1, mod2, ln2g, ln2b, l, slots=slots, win=win,
                      n_lat_tiles=n // TOKEN_TILE, alpha=alpha, out_tokens=n if l == depth - 1 else tt)
    return xc
```

```python
import functools
import math

import numpy as np
import jax
import jax.numpy as jnp
from jax import lax
from jax.experimental import pallas as pl
from jax.experimental.pallas import tpu as pltpu

F32 = jnp.float32
BF16 = jnp.bfloat16
I32 = jnp.int32

GRID_W = 64
QK_DIM = 64
N_HEADS = 4
HEAD_V = 128
QK_W = 512
ATT_W = 512
LRU_W = 256
LRU_BLOCKS = 4
LRU_BLOCK_W = 64
LRU_C = 8.0
FNET_W = 256
FNET_GROUP_W = 64
IN_W = 2304
ROPE_BASE = 10000.0
N_EXPERTS = 16
CAPACITY_FACTOR = 2
LN_EPS = 1e-5
RMS_EPS = 1e-6
GELU_C = math.sqrt(2.0 / math.pi)
LOG2E = math.log2(math.e)

LANES = 128
SUBLANES = 8
BF16_ROWS = 16
KEY_CHUNK = 512
TOKEN_TILE = 256
SLOT_WINDOW = 128
VMEM_LIMIT = 56 << 20


def _params(sem, vmem=VMEM_LIMIT):
    return pltpu.CompilerParams(dimension_semantics=sem, vmem_limit_bytes=vmem)


def _ln(x):
    mu = jnp.mean(x, axis=-1, keepdims=True)
    xc = x - mu
    var = jnp.mean(xc * xc, axis=-1, keepdims=True)
    return xc * lax.rsqrt(var + LN_EPS)


def _largest_divisor(n, candidates):
    for c in candidates:
        if c <= n and n % c == 0:
            return c
    return n


def _mod_kernel(c_ref, w_ref, b_ref, o_ref):
    c = c_ref[...]
    s = c * jax.nn.sigmoid(c)
    o_ref[0] = jnp.dot(s, w_ref[0], precision=lax.Precision.HIGHEST, preferred_element_type=F32) + b_ref[0]


def _modulation(cc, w_mod, b_mod):
    depth, d, d6 = w_mod.shape
    rows = cc.shape[0]
    tn = 1024
    return pl.pallas_call(
        _mod_kernel,
        grid=(depth, d6 // tn),
        in_specs=[
            pl.BlockSpec((rows, d), lambda l, n: (0, 0)),
            pl.BlockSpec((1, d, tn), lambda l, n: (l, 0, n)),
            pl.BlockSpec((1, 1, tn), lambda l, n: (l, 0, n)),
        ],
        out_specs=pl.BlockSpec((1, rows, tn), lambda l, n: (l, 0, n)),
        out_shape=jax.ShapeDtypeStruct((depth, rows, d6), F32),
        compiler_params=_params(("arbitrary", "arbitrary")),
        name="modulation",
    )(cc, w_mod, b_mod.reshape(depth, 1, d6))


def _in_kernel(x_ref, mod_ref, w_ref, cos_ref, sin_ref, cs_ref,
               qx_ref, k_ref, v_ref, ux_ref, gg_ref, gc_ref, gs_ref, *, d):
    x = x_ref[0]
    m = mod_ref[0]
    h = _ln(x) * (1.0 + m[:, d:2 * d]) + m[:, 0:d]
    z = jnp.dot(h.astype(BF16), w_ref[0], preferred_element_type=F32)

    tb = x.shape[0]
    cos = cos_ref[...]
    sin = sin_ref[...]
    lane = lax.broadcasted_iota(I32, (tb, LANES), 1)
    first_half = (lane & 31) < 16
    low_map = lane < QK_DIM

    def rope(t):
        partner = jnp.where(first_half, pltpu.roll(t, LANES - 16, 1), pltpu.roll(t, 16, 1))
        return t * cos + partner * sin

    for p in range(N_HEADS):
        qp = rope(z[:, p * LANES:(p + 1) * LANES] * (QK_DIM ** -0.5 * LOG2E))
        qx_ref[0, :, (2 * p) * LANES:(2 * p + 1) * LANES] = jnp.where(low_map, qp, 0.0).astype(BF16)
        qx_ref[0, :, (2 * p + 1) * LANES:(2 * p + 2) * LANES] = jnp.where(low_map, 0.0, qp).astype(BF16)
        kp = rope(z[:, QK_W + p * LANES:QK_W + (p + 1) * LANES])
        k_ref[0, :, p * LANES:(p + 1) * LANES] = kp.astype(BF16)

    o = 2 * QK_W
    v_ref[0] = z[:, o:o + ATT_W].astype(BF16)
    o += ATT_W
    ux_ref[0] = z[:, o:o + LRU_W]
    o += LRU_W
    g = z[:, o:o + LRU_W]
    gg_ref[0] = 0.5 * g * (1.0 + jnp.tanh(GELU_C * (g + 0.044715 * (g * g * g))))
    o += LRU_W
    uf = z[:, o:o + FNET_W].astype(BF16)
    gcs = jnp.dot(uf, cs_ref[...], preferred_element_type=F32)
    gc_ref[...] = gcs[:, :FNET_W].astype(BF16)
    gs_ref[...] = gcs[:, FNET_W:].astype(BF16)


def _in_proj(xc, mod2, w_in_bf, l, cos_t, sin_t, cs, *, nbl, tb):
    b, tt, d = xc.shape
    nbt = tt // tb
    kern = functools.partial(_in_kernel, d=d)
    tok = lambda w: pl.BlockSpec((1, tb, w), lambda i, j: (i, j, 0))
    return pl.pallas_call(
        kern,
        grid=(b, nbt),
        in_specs=[
            tok(d),
            pl.BlockSpec((1, 1, 6 * d), lambda i, j: (2 * i + (j >= nbl).astype(I32), 0, 0)),
            pl.BlockSpec((1, d, IN_W), lambda i, j: (l, 0, 0)),
            pl.BlockSpec((tb, LANES), lambda i, j: (j, 0)),
            pl.BlockSpec((tb, LANES), lambda i, j: (j, 0)),
            pl.BlockSpec((FNET_W, 2 * FNET_W), lambda i, j: (0, 0)),
        ],
        out_specs=[
            tok(2 * QK_W), tok(QK_W), tok(ATT_W), tok(LRU_W), tok(LRU_W),
            pl.BlockSpec((tb, FNET_W), lambda i, j: (j, i)),
            pl.BlockSpec((tb, FNET_W), lambda i, j: (j, i)),
        ],
        out_shape=[
            jax.ShapeDtypeStruct((b, tt, 2 * QK_W), BF16),
            jax.ShapeDtypeStruct((b, tt, QK_W), BF16),
            jax.ShapeDtypeStruct((b, tt, ATT_W), BF16),
            jax.ShapeDtypeStruct((b, tt, LRU_W), F32),
            jax.ShapeDtypeStruct((b, tt, LRU_W), F32),
            jax.ShapeDtypeStruct((tt, b * FNET_W), BF16),
            jax.ShapeDtypeStruct((tt, b * FNET_W), BF16),
        ],
        compiler_params=_params(("arbitrary", "arbitrary")),
        name="in_proj",
    )(xc, mod2, w_in_bf, cos_t, sin_t, cs)


def _attn_kernel(ql_ref, qc_ref, k_ref, v_ref, lamv_ref, lin_ref, g_ref, ol_ref, oc_ref, vt_ref, s0_ref, s1_ref,
                 *, n, lat_steps, qb):
    j = pl.program_id(2)
    lv = lamv_ref[0]
    lam_init = lin_ref[0][:, 0:1]
    lam = (jnp.exp(jnp.sum(lv[0:1] * lv[1:2], axis=1, keepdims=True))
           - jnp.exp(jnp.sum(lv[2:3] * lv[3:4], axis=1, keepdims=True)) + lam_init)
    gain = g_ref[0] * (1.0 - lam_init)
    nt = (((1,), (1,)), ((), ()))
    tt = k_ref.shape[1]

    @pl.when(j == 0)
    def _():
        vt_ref[0:HEAD_V, :] = v_ref[0].astype(F32).T.astype(BF16)
        row = lax.broadcasted_iota(I32, (BF16_ROWS, tt), 0)
        vt_ref[HEAD_V:, :] = jnp.where(row == 0, 1.0, 0.0).astype(BF16)

    def attend(q_ref, o_ref, blocks, lo):
        chunks = [(c, min(KEY_CHUNK, tt - c)) for c in range(lo, tt, KEY_CHUNK)]
        tq = s0_ref.shape[1]
        streams = [(blk, mp) for blk in range(blocks) for mp in range(2)]
        s_refs = (s0_ref, s1_ref)

        def scores(i, c, w, m):
            blk, mp = streams[i]
            q = q_ref[0, blk * tq:(blk + 1) * tq, mp * LANES:(mp + 1) * LANES]
            st = lax.dot_general(k_ref[0, c:c + w, :], q, nt, preferred_element_type=F32)
            s_refs[i % 2][c:c + w, :] = st
            mc = jnp.max(st, axis=0, keepdims=True)
            return mc if m is None else jnp.maximum(m, mc)

        def weighted_values(i, c, w, m, acc):
            pt = jnp.exp2(s_refs[i % 2][c:c + w, :] - m).astype(BF16)
            part = jnp.dot(vt_ref[:, c:c + w], pt, preferred_element_type=F32)
            return part if acc is None else acc + part

        ms = [None] * len(streams)
        accs = [None] * len(streams)
        for stage in range(len(streams) + 1):
            for c, w in chunks:
                if stage >= 1:
                    accs[stage - 1] = weighted_values(stage - 1, c, w, ms[stage - 1], accs[stage - 1])
                if stage < len(streams):
                    ms[stage] = scores(stage, c, w, ms[stage])
            if stage >= 2 and stage % 2 == 0:
                blk = stage // 2 - 1
                a0, a1 = accs[stage - 2], accs[stage - 1]
                ot = (a0[0:HEAD_V] / a0[HEAD_V:HEAD_V + 1]) - lam * (a1[0:HEAD_V] / a1[HEAD_V:HEAD_V + 1])
                rt = ot * lax.rsqrt(jnp.mean(ot * ot, axis=0, keepdims=True) + RMS_EPS)
                o_ref[0, blk * tq:(blk + 1) * tq, :] = (rt.T * gain).astype(BF16)

    @pl.when(j < lat_steps)
    def _():
        attend(ql_ref, ol_ref, qb, 0)

    @pl.when(j >= lat_steps)
    def _():
        attend(qc_ref, oc_ref, qc_ref.shape[1] // s0_ref.shape[1], n)


def _attention(qx, k, v, lamv, lin, gain, l, *, n, tb):
    b, tt, _ = k.shape
    cn = tt - n
    qb = _largest_divisor(n // tb, (4, 2, 1))
    lat_steps = n // (qb * tb)
    assert n % cn == 0 and cn % tb == 0
    kern = functools.partial(_attn_kernel, n=n, lat_steps=lat_steps, qb=qb)
    lat = lambda w: pl.BlockSpec((1, qb * tb, w), lambda i, h, j: (i, jnp.minimum(j, lat_steps - 1), h))
    ctx = lambda w: pl.BlockSpec((1, cn, w), lambda i, h, j: (i, n // cn, h))
    return pl.pallas_call(
        kern,
        grid=(b, N_HEADS, lat_steps + 1),
        in_specs=[
            lat(2 * LANES), ctx(2 * LANES),
            pl.BlockSpec((1, tt, LANES), lambda i, h, j: (i, 0, h)),
            pl.BlockSpec((1, tt, LANES), lambda i, h, j: (i, 0, h)),
            pl.BlockSpec((1, 4, QK_DIM), lambda i, h, j: (l, 0, 0)),
            pl.BlockSpec((1, 1, LANES), lambda i, h, j: (l, 0, 0)),
            pl.BlockSpec((1, 1, HEAD_V), lambda i, h, j: (l * N_HEADS + h, 0, 0)),
        ],
        out_specs=[lat(LANES), pl.BlockSpec((1, cn, LANES), lambda i, h, j: (i, 0, h))],
        out_shape=[jax.ShapeDtypeStruct((b, n, ATT_W), BF16), jax.ShapeDtypeStruct((b, cn, ATT_W), BF16)],
        scratch_shapes=[pltpu.VMEM((HEAD_V + BF16_ROWS, tt), BF16),
                        pltpu.VMEM((tt, tb), F32), pltpu.VMEM((tt, tb), F32)],
        compiler_params=_params(("arbitrary", "arbitrary", "arbitrary")),
        name="diff_attention",
    )(qx, qx, k, v, lamv, lin, gain)


def _lru_kernel(ux_ref, gg_ref, cw_ref, cb_ref, wg_ref, bg_ref, lam_ref, y_ref,
                a_f, b_f, a_b, b_b, *, n, cn, r):
    tt = n + cn
    w = LANES
    cw = cw_ref[0]
    cb = cb_ref[0]
    bg = bg_ref[0]
    neg_lam = -lam_ref[0]
    softplus = jnp.maximum(neg_lam, 0.0) + jnp.log(1.0 + jnp.exp(-jnp.abs(neg_lam)))
    row8 = lax.broadcasted_iota(I32, (r, w), 0) & (SUBLANES - 1)
    ext_rows = r + 2 * SUBLANES

    def local_scan(a, bb, reverse):
        for s in (1, 2, 4):
            if reverse:
                a_sh = pltpu.roll(a, r - s, 0)
                b_sh = pltpu.roll(bb, r - s, 0)
                valid = row8 < SUBLANES - s
            else:
                a_sh = pltpu.roll(a, s, 0)
                b_sh = pltpu.roll(bb, s, 0)
                valid = row8 >= s
            bb = jnp.where(valid, a * b_sh + bb, bb)
            a = jnp.where(valid, a * a_sh, a)
        return a, bb

    def gates_chunk(c, carry):
        r0 = pl.multiple_of(c * r, r)
        seg_start = jnp.logical_or(r0 == 0, r0 == n)
        seg_end = jnp.logical_or(r0 + r == n, r0 + r == tt)
        main = ux_ref[0, pl.ds(r0, r), :]
        prev = ux_ref[0, pl.ds(pl.multiple_of(jnp.maximum(r0 - SUBLANES, 0), SUBLANES), SUBLANES), :]
        nxt = ux_ref[0, pl.ds(pl.multiple_of(jnp.minimum(r0 + r, tt - SUBLANES), SUBLANES), SUBLANES), :]
        prev = jnp.where(seg_start, 0.0, prev)
        nxt = jnp.where(seg_end, 0.0, nxt)
        ext = jnp.concatenate([prev, main, nxt], axis=0)
        u = cb
        for t in range(4):
            sh = (2 - t) % ext_rows
            win = ext if sh == 0 else pltpu.roll(ext, sh, 0)
            u = u + cw[t:t + 1, :] * win[SUBLANES:SUBLANES + r, :]
        zz = jnp.dot(u.astype(BF16), wg_ref[0], preferred_element_type=F32) + bg
        for dr, (a_s, b_s) in enumerate(((a_f, b_f), (a_b, b_b))):
            rg = jax.nn.sigmoid(zz[:, (2 * dr) * w:(2 * dr + 1) * w])
            ig = jax.nn.sigmoid(zz[:, (2 * dr + 1) * w:(2 * dr + 2) * w])
            a = jnp.exp(-LRU_C * rg * softplus[dr:dr + 1, :])
            bb = jnp.sqrt(1.0 - a * a) * ig * u
            a, bb = local_scan(a, bb, reverse=(dr == 1))
            a_s[pl.ds(r0, r), :] = a
            b_s[pl.ds(r0, r), :] = bb
        return carry

    lax.fori_loop(0, tt // r, gates_chunk, 0)

    def seg_scan(first_tile, ntiles, cf, cbk):
        def body(i, carry):
            cf, cbk = carry
            rf = pl.multiple_of((first_tile + i) * SUBLANES, SUBLANES)
            hf = b_f[pl.ds(rf, SUBLANES), :] + a_f[pl.ds(rf, SUBLANES), :] * cf
            b_f[pl.ds(rf, SUBLANES), :] = hf
            rb = pl.multiple_of((first_tile + ntiles - 1 - i) * SUBLANES, SUBLANES)
            hb = b_b[pl.ds(rb, SUBLANES), :] + a_b[pl.ds(rb, SUBLANES), :] * cbk
            b_b[pl.ds(rb, SUBLANES), :] = hb
            return hf[SUBLANES - 1:SUBLANES, :], hb[0:1, :]
        return lax.fori_loop(0, ntiles, body, (cf, cbk))

    zero = jnp.zeros((1, w), F32)
    cf, cbk = seg_scan(n // SUBLANES, cn // SUBLANES, zero, zero)
    seg_scan(0, n // SUBLANES, cf, cbk)

    def out_chunk(c, carry):
        r0 = pl.multiple_of(c * r, r)
        y = (b_f[pl.ds(r0, r), :] + b_b[pl.ds(r0, r), :]) * gg_ref[0, pl.ds(r0, r), :]
        y_ref[0, pl.ds(r0, r), :] = y.astype(BF16)
        return carry

    lax.fori_loop(0, tt // r, out_chunk, 0)


def _rglru(ux, gg, conv_w2, conv_b2, wgate, bgate, lam2, l, *, n, cn, tb):
    b, tt, _ = ux.shape
    kern = functools.partial(_lru_kernel, n=n, cn=cn, r=tb)
    half = lambda: pl.BlockSpec((1, tt, LANES), lambda i, hh: (i, 0, hh))
    return pl.pallas_call(
        kern,
        grid=(b, 2),
        in_specs=[
            half(), half(),
            pl.BlockSpec((1, 4, LANES), lambda i, hh: (2 * l + hh, 0, 0)),
            pl.BlockSpec((1, 1, LANES), lambda i, hh: (2 * l + hh, 0, 0)),
            pl.BlockSpec((1, LANES, 4 * LANES), lambda i, hh: (2 * l + hh, 0, 0)),
            pl.BlockSpec((1, 1, 4 * LANES), lambda i, hh: (2 * l + hh, 0, 0)),
            pl.BlockSpec((1, 2, LANES), lambda i, hh: (2 * l + hh, 0, 0)),
        ],
        out_specs=half(),
        out_shape=jax.ShapeDtypeStruct((b, tt, LRU_W), BF16),
        scratch_shapes=[pltpu.VMEM((tt, LANES), F32)] * 4,
        compiler_params=_params(("arbitrary", "arbitrary")),
        name="rglru",
    )(ux, gg, conv_w2, conv_b2, wgate, bgate, lam2)


def _dft_kernel(c_ref, s_ref, gc_ref, gs_ref, o_ref, acc_ref):
    kk = pl.program_id(2)

    @pl.when(kk == 0)
    def _():
        acc_ref[...] = jnp.zeros_like(acc_ref)

    acc_ref[...] += (jnp.dot(c_ref[...], gc_ref[...], preferred_element_type=F32)
                     + jnp.dot(s_ref[...], gs_ref[...], preferred_element_type=F32))

    @pl.when(kk == pl.num_programs(2) - 1)
    def _():
        o_ref[...] = acc_ref[...].astype(BF16)


def _position_dft(cmat, smat, gc, gs):
    tt = cmat.shape[0]
    nn = gc.shape[1]
    tm = _largest_divisor(tt, (1088, 640, 512, 256, 128))
    tk = _largest_divisor(tt, (2176, 640, 512, 256, 128))
    tn = _largest_divisor(nn, (512, 256))
    return pl.pallas_call(
        _dft_kernel,
        grid=(tt // tm, nn // tn, tt // tk),
        in_specs=[
            pl.BlockSpec((tm, tk), lambda i, j, k: (i, k)),
            pl.BlockSpec((tm, tk), lambda i, j, k: (i, k)),
            pl.BlockSpec((tk, tn), lambda i, j, k: (k, j)),
            pl.BlockSpec((tk, tn), lambda i, j, k: (k, j)),
        ],
        out_specs=pl.BlockSpec((tm, tn), lambda i, j, k: (i, j)),
        out_shape=jax.ShapeDtypeStruct((tt, nn), BF16),
        scratch_shapes=[pltpu.VMEM((tm, tn), F32)],
        compiler_params=_params(("arbitrary", "arbitrary", "arbitrary")),
        name="position_dft",
    )(cmat, smat, gc, gs)


def _out_kernel(attl_ref, attc_ref, y_ref, f_ref, x_ref, mod_ref, w_ref, g_ref, b_ref, wr_ref,
                x1_ref, h2_ref, lg_ref, *, d, alpha, nbl):
    w = w_ref[0]
    att = jnp.where(pl.program_id(1) >= nbl, attc_ref[0], attl_ref[0])
    mix = (jnp.dot(att, w[0:ATT_W], preferred_element_type=F32)
           + jnp.dot(y_ref[0], w[ATT_W:ATT_W + LRU_W], preferred_element_type=F32)
           + jnp.dot(f_ref[...], w[ATT_W + LRU_W:], preferred_element_type=F32))
    m = mod_ref[0]
    x1 = _ln(alpha * x_ref[0] + m[:, 2 * d:3 * d] * mix) * g_ref[0] + b_ref[0]
    x1_ref[0] = x1
    h2 = (_ln(x1) * (1.0 + m[:, 4 * d:5 * d]) + m[:, 3 * d:4 * d]).astype(BF16)
    h2_ref[0] = h2
    lg_ref[0] = lax.dot_general(wr_ref[0], h2, (((1,), (1,)), ((), ())), preferred_element_type=F32)


def _out_proj(att_l, att_c, y, fy, xc, mod2, w_out_bf, ln_g, ln_b, wr_t, l, *, nbl, tb, alpha):
    b, tt, d = xc.shape
    nbt = tt // tb
    kern = functools.partial(_out_kernel, d=d, alpha=alpha, nbl=nbl)
    tok = lambda w: pl.BlockSpec((1, tb, w), lambda i, j: (i, j, 0))
    return pl.pallas_call(
        kern,
        grid=(b, nbt),
        in_specs=[
            pl.BlockSpec((1, tb, ATT_W), lambda i, j: (i, jnp.minimum(j, nbl - 1), 0)),
            pl.BlockSpec((1, tb, ATT_W), lambda i, j: (i, jnp.maximum(j - nbl, 0), 0)),
            tok(LRU_W),
            pl.BlockSpec((tb, FNET_W), lambda i, j: (j, i)),
            tok(d),
            pl.BlockSpec((1, 1, 6 * d), lambda i, j: (2 * i + (j >= nbl).astype(I32), 0, 0)),
            pl.BlockSpec((1, d, d), lambda i, j: (l, 0, 0)),
            pl.BlockSpec((1, 1, d), lambda i, j: (l, 0, 0)),
            pl.BlockSpec((1, 1, d), lambda i, j: (l, 0, 0)),
            pl.BlockSpec((1, N_EXPERTS, d), lambda i, j: (l, 0, 0)),
        ],
        out_specs=[tok(d), tok(d), pl.BlockSpec((1, N_EXPERTS, tb), lambda i, j: (i, 0, j))],
        out_shape=[
            jax.ShapeDtypeStruct((b, tt, d), F32),
            jax.ShapeDtypeStruct((b, tt, d), BF16),
            jax.ShapeDtypeStruct((b, N_EXPERTS, tt), F32),
        ],
        compiler_params=_params(("arbitrary", "arbitrary")),
        name="out_proj",
    )(att_l, att_c, y, fy, xc, mod2, w_out_bf, ln_g, ln_b, wr_t)


def _route_kernel(lg_ref, pos_ref, gate_ref, starts_ref, *, segments, n_slots):
    lg = lg_ref[0]
    lane_id = lax.broadcasted_iota(I32, (N_EXPERTS, LANES), 1)
    starts = jnp.zeros((N_EXPERTS, LANES), I32)
    chunks_per_tile = TOKEN_TILE // LANES
    e = jnp.exp(lg - jnp.max(lg, axis=0, keepdims=True))
    s = e / jnp.sum(e, axis=0, keepdims=True)
    ri = lax.broadcasted_iota(I32, (LANES, LANES), 0)
    ci = lax.broadcasted_iota(I32, (LANES, LANES), 1)
    strict_upper = jnp.where(ri < ci, 1.0, 0.0).astype(BF16)

    for lo, t, cap, base in segments:
        ss = s[:, lo:lo + t]
        bits = pltpu.bitcast(ss, I32)
        capf = float(cap)

        def search(i, thr, bits=bits, capf=capf):
            cand = thr | jnp.left_shift(jnp.int32(1), 30 - i)
            cnt = jnp.sum(jnp.where(bits >= cand, 1.0, 0.0), axis=1, keepdims=True)
            return jnp.where(cnt >= capf, cand, thr)

        thr = lax.fori_loop(0, 31, search, jnp.zeros((N_EXPERTS, 1), I32))
        need = capf - jnp.sum(jnp.where(bits > thr, 1.0, 0.0), axis=1, keepdims=True)
        off_eq = jnp.zeros((N_EXPERTS, 1), F32)
        off_sel = jnp.zeros((N_EXPERTS, 1), F32)
        for c in range(t // LANES):
            sl = slice(c * LANES, (c + 1) * LANES)
            tile, sub = divmod(lo // LANES + c, chunks_per_tile)
            if sub == 0:
                starts = jnp.where(lane_id == tile, off_sel.astype(I32) + base, starts)
            bits_c = bits[:, sl]
            eq = bits_c == thr
            eq_c = jnp.where(eq, 1.0, 0.0)
            rank_eq = jnp.dot(eq_c.astype(BF16), strict_upper, preferred_element_type=F32) + off_eq
            off_eq = off_eq + jnp.sum(eq_c, axis=1, keepdims=True)
            sel = jnp.logical_or(bits_c > thr, jnp.logical_and(eq, rank_eq < need))
            sel_c = jnp.where(sel, 1.0, 0.0)
            slot = jnp.dot(sel_c.astype(BF16), strict_upper, preferred_element_type=F32) + off_sel
            off_sel = off_sel + jnp.sum(sel_c, axis=1, keepdims=True)
            osl = slice(sub * LANES, (sub + 1) * LANES)
            pos_ref[0, tile, :, osl] = jnp.where(sel, slot.astype(I32) + base, -1)
            gate_ref[0, tile, :, osl] = jnp.where(sel, ss[:, sl], 0.0)

    n_tiles = lg.shape[1] // TOKEN_TILE
    starts_ref[0] = jnp.where(lane_id == n_tiles, n_slots, starts)


def _route(logits_t, *, segments, n_slots):
    b, ne, tt = logits_t.shape
    nt = tt // TOKEN_TILE
    assert nt < LANES and all(lo % TOKEN_TILE == 0 and t % TOKEN_TILE == 0 for lo, t, _, _ in segments)
    kern = functools.partial(_route_kernel, segments=segments, n_slots=n_slots)
    tiled = lambda: pl.BlockSpec((1, nt, ne, TOKEN_TILE), lambda i: (i, 0, 0, 0))
    return pl.pallas_call(
        kern,
        grid=(b,),
        in_specs=[pl.BlockSpec((1, ne, tt), lambda i: (i, 0, 0))],
        out_specs=[tiled(), tiled(), pl.BlockSpec((1, ne, LANES), lambda i: (i, 0, 0))],
        out_shape=[jax.ShapeDtypeStruct((b, nt, ne, TOKEN_TILE), I32),
                   jax.ShapeDtypeStruct((b, nt, ne, TOKEN_TILE), F32),
                   jax.ShapeDtypeStruct((b, ne, LANES), I32)],
        compiler_params=_params(("arbitrary",)),
        name="route",
    )(logits_t)


def _window_start(starts_ref, base_idx, tile, w, win, limit):
    c0 = starts_ref[base_idx + tile]
    lo = ((c0 >> 4) << 4) + w * win
    return lo, pl.multiple_of(jnp.minimum(lo, limit), BF16_ROWS)


def _window_count(starts_ref, base_idx, tiles, win):
    nw = jnp.int32(1)
    for tile in tiles:
        c0 = starts_ref[base_idx + tile]
        c1 = starts_ref[base_idx + tile + 1]
        nw = jnp.maximum(nw, (c1 - ((c0 >> 4) << 4) + win - 1) // win)
    return nw


def _gather_kernel(starts_ref, h_ref, pos_ref, xs_ref, acc_ref, *, slots, win, group):
    b = pl.program_id(0)
    eg = pl.program_id(1)
    nt = pos_ref.shape[1]
    acc_ref[...] = jnp.zeros_like(acc_ref)
    rel = lax.broadcasted_iota(I32, (win, TOKEN_TILE), 0)
    bases = [(b * N_EXPERTS + eg * group + i) * LANES for i in range(group)]
    nw = jnp.int32(1)
    for i in range(group):
        nw = jnp.maximum(nw, _window_count(starts_ref, bases[i], range(nt), win))

    def window_pass(w, carry):
        for tile in range(nt):
            onehots, offs = [], []
            for i in range(group):
                lo, start = _window_start(starts_ref, bases[i], tile, w, win, slots)
                prow = pos_ref[0, tile, pl.ds(eg * group + i, 1), :] - start
                onehots.append(jnp.where(rel == prow, 1.0, 0.0).astype(BF16))
                offs.append(start)
            res = jnp.dot(jnp.concatenate(onehots, axis=0), h_ref[0, tile * TOKEN_TILE:(tile + 1) * TOKEN_TILE, :],
                          preferred_element_type=F32)
            for i in range(group):
                acc_ref[i, pl.ds(offs[i], win), :] += res[i * win:(i + 1) * win]
        return carry

    lax.fori_loop(0, nw, window_pass, 0)
    xs_ref[...] = acc_ref[:, 0:slots, :].astype(BF16)


def _gather(starts, h2, pos, *, slots, win, group=4):
    b, tt, d = h2.shape
    nt = tt // TOKEN_TILE
    kern = functools.partial(_gather_kernel, slots=slots, win=win, group=group)
    return pl.pallas_call(
        kern,
        grid_spec=pltpu.PrefetchScalarGridSpec(
            num_scalar_prefetch=1,
            grid=(b, N_EXPERTS // group),
            in_specs=[
                pl.BlockSpec((1, tt, d), lambda i, g, s: (i, 0, 0)),
                pl.BlockSpec((1, nt, N_EXPERTS, TOKEN_TILE), lambda i, g, s: (i, 0, 0, 0)),
            ],
            out_specs=pl.BlockSpec((group, slots, d), lambda i, g, s: (g, i, 0)),
            scratch_shapes=[pltpu.VMEM((group, slots + win, d), F32)],
        ),
        out_shape=jax.ShapeDtypeStruct((N_EXPERTS, b * slots, d), BF16),
        compiler_params=_params(("arbitrary", "arbitrary")),
        name="moe_gather",
    )(starts, h2, pos)


def _ffn_kernel(xs_ref, wg_ref, wu_ref, wd_ref, ys_ref, *, fchunk):
    xs = xs_ref[0]
    f = wg_ref.shape[-1]
    acc = None
    for c in range(f // fchunk):
        sl = slice(c * fchunk, (c + 1) * fchunk)
        a = jnp.dot(xs, wg_ref[0, 0, :, sl], preferred_element_type=F32)
        u = jnp.dot(xs, wu_ref[0, 0, :, sl], preferred_element_type=F32)
        hm = (a * jax.nn.sigmoid(a) * u).astype(BF16)
        y = jnp.dot(hm, wd_ref[0, 0, sl, :], preferred_element_type=F32)
        acc = y if acc is None else acc + y
    ys_ref[0] = acc.astype(BF16)


def _expert_ffn(xs, wg, wu, wd, l, *, slots):
    ne, rows, d = xs.shape
    f = wg.shape[-1]
    nb = rows // slots
    kern = functools.partial(_ffn_kernel, fchunk=_largest_divisor(f, (512,)))
    return pl.pallas_call(
        kern,
        grid=(ne, nb),
        in_specs=[
            pl.BlockSpec((1, slots, d), lambda e, i: (e, i, 0)),
            pl.BlockSpec((1, 1, d, f), lambda e, i: (l, e, 0, 0)),
            pl.BlockSpec((1, 1, d, f), lambda e, i: (l, e, 0, 0)),
            pl.BlockSpec((1, 1, f, d), lambda e, i: (l, e, 0, 0)),
        ],
        out_specs=pl.BlockSpec((1, slots, d), lambda e, i: (e, i, 0)),
        out_shape=jax.ShapeDtypeStruct((ne, rows, d), BF16),
        compiler_params=_params(("arbitrary", "arbitrary")),
        name="expert_ffn",
    )(xs, wg, wu, wd)


def _combine_kernel(starts_ref, ys_ref, pos_ref, gate_ref, x1_ref, mod_ref, g_ref, b_ref, o_ref,
                    *, slots, win, d, alpha):
    b = pl.program_id(0)
    tile = pl.program_id(1)
    rel = lax.broadcasted_iota(I32, (win, TOKEN_TILE), 0)
    bases = [(b * N_EXPERTS + e) * LANES for e in range(N_EXPERTS)]
    nw = jnp.int32(1)
    for e in range(N_EXPERTS):
        c0 = starts_ref[bases[e] + tile]
        c1 = starts_ref[bases[e] + tile + 1]
        nw = jnp.maximum(nw, (c1 - ((c0 >> 4) << 4) + win - 1) // win)

    def window_pass(w, moe):
        gated, rows = [], []
        for e in range(N_EXPERTS):
            lo, start = _window_start(starts_ref, bases[e], tile, w, win, slots - win)
            prow = pos_ref[0, 0, e:e + 1, :]
            prow = jnp.where(jnp.logical_and(prow >= lo, prow < lo + win), prow - start, -1)
            gated.append(jnp.where(rel == prow, gate_ref[0, 0, e:e + 1, :], 0.0).astype(BF16))
            rows.append(ys_ref[e, pl.ds(start, win), :])
        return moe + lax.dot_general(jnp.concatenate(gated, axis=0), jnp.concatenate(rows, axis=0),
                                     (((0,), (0,)), ((), ())), preferred_element_type=F32)

    moe = lax.fori_loop(0, nw, window_pass, jnp.zeros((TOKEN_TILE, d), F32))
    m = mod_ref[0]
    o_ref[0] = _ln(alpha * x1_ref[0] + m[:, 5 * d:6 * d] * moe) * g_ref[0] + b_ref[0]


def _combine(starts, ys, pos, gate, x1, mod2, ln_g, ln_b, l, *, slots, win, n_lat_tiles, alpha, out_tokens):
    b, _, d = x1.shape
    kern = functools.partial(_combine_kernel, slots=slots, win=win, d=d, alpha=alpha)
    tiled = lambda: pl.BlockSpec((1, 1, N_EXPERTS, TOKEN_TILE), lambda i, t, s: (i, t, 0, 0))
    tok = lambda: pl.BlockSpec((1, TOKEN_TILE, d), lambda i, t, s: (i, t, 0))
    return pl.pallas_call(
        kern,
        grid_spec=pltpu.PrefetchScalarGridSpec(
            num_scalar_prefetch=1,
            grid=(b, out_tokens // TOKEN_TILE),
            in_specs=[
                pl.BlockSpec((N_EXPERTS, slots, d), lambda i, t, s: (0, i, 0)),
                tiled(), tiled(), tok(),
                pl.BlockSpec((1, 1, 6 * d), lambda i, t, s: (2 * i + (t >= n_lat_tiles).astype(I32), 0, 0)),
                pl.BlockSpec((1, 1, d), lambda i, t, s: (l, 0, 0)),
                pl.BlockSpec((1, 1, d), lambda i, t, s: (l, 0, 0)),
            ],
            out_specs=tok(),
        ),
        out_shape=jax.ShapeDtypeStruct((b, out_tokens, d), F32),
        compiler_params=_params(("arbitrary", "arbitrary")),
        name="moe_combine",
    )(starts, ys, pos, gate, x1, mod2, ln_g, ln_b)


def _rope_tables(n, cn):
    lane = np.arange(LANES)
    within = lane % QK_DIM
    use_col = (within // 32) == 1
    first_half = (within % 32) < 16
    inv = ROPE_BASE ** (-(within % 16).astype(np.float64) / 16.0)
    pos = np.arange(n)
    coord = np.where(use_col[None, :], (pos % GRID_W)[:, None], (pos // GRID_W)[:, None]).astype(np.float32)
    ang = jnp.asarray(coord) * jnp.asarray(inv.astype(np.float32))[None, :]
    cos = jnp.cos(ang)
    sin = jnp.where(jnp.asarray(first_half)[None, :], -jnp.sin(ang), jnp.sin(ang))
    cos = jnp.concatenate([cos, jnp.ones((cn, LANES), F32)], axis=0)
    sin = jnp.concatenate([sin, jnp.zeros((cn, LANES), F32)], axis=0)
    return cos, sin


def _channel_dft():
    idx = np.arange(FNET_W)
    same = (idx[:, None] // FNET_GROUP_W) == (idx[None, :] // FNET_GROUP_W)
    ang = 2.0 * np.pi * ((idx[:, None] % FNET_GROUP_W) * (idx[None, :] % FNET_GROUP_W) % FNET_GROUP_W) / FNET_GROUP_W
    cs = np.concatenate([np.where(same, np.cos(ang), 0.0), np.where(same, np.sin(ang), 0.0)], axis=1)
    return jnp.asarray(cs.astype(np.float32)).astype(BF16)


def _position_dft_mats(n, cn):
    def segment(t):
        g = math.gcd(t, 64)
        kk = jnp.arange(t, dtype=I32)

        def table(m):
            ph = ((kk[:, None] * m[None, :]) % t).astype(F32) * (2.0 * math.pi / t)
            return jnp.cos(ph), jnp.sin(ph)

        ch, sh = table(jnp.arange(t // g, dtype=I32) * g)
        cl, sl = table(jnp.arange(g, dtype=I32))
        scale = 1.0 / math.sqrt(t * FNET_GROUP_W)
        cmat = (ch[:, :, None] * cl[:, None, :] - sh[:, :, None] * sl[:, None, :]).reshape(t, t) * scale
        smat = (sh[:, :, None] * cl[:, None, :] + ch[:, :, None] * sl[:, None, :]).reshape(t, t) * (-scale)
        return cmat.astype(BF16), smat.astype(BF16)

    def block_diag(a, c):
        top = jnp.concatenate([a, jnp.zeros((n, cn), BF16)], axis=1)
        bot = jnp.concatenate([jnp.zeros((cn, n), BF16), c], axis=1)
        return jnp.concatenate([top, bot], axis=0)

    (cl_, sl_), (cc_, sc_) = segment(n), segment(cn)
    return block_diag(cl_, cc_), block_diag(sl_, sc_)


def _block_diag_gates(wa, wx):
    depth = wa.shape[0]

    def dense(wb):
        eye = jnp.eye(LRU_BLOCKS, dtype=wb.dtype)
        return jnp.einsum('lncd,nm->lncmd', wb, eye).reshape(depth, LRU_W, LRU_W)

    halves = []
    for hh in range(2):
        sl = slice(hh * LANES, (hh + 1) * LANES)
        cols = [dense(wmat[:, dr])[:, sl, sl] for dr in range(2) for wmat in (wa, wx)]
        halves.append(jnp.concatenate(cols, axis=-1))
    return jnp.stack(halves, axis=1).reshape(depth * 2, LANES, 4 * LANES).astype(BF16)


def _gate_bias(ba, bx):
    depth = ba.shape[0]
    halves = []
    for hh in range(2):
        sl = slice(hh * LANES, (hh + 1) * LANES)
        halves.append(jnp.concatenate([bvec[:, dr, sl] for dr in range(2) for bvec in (ba, bx)], axis=-1))
    return jnp.stack(halves, axis=1).reshape(depth * 2, 1, 4 * LANES)


def _split_halves(a):
    depth, r, _ = a.shape
    return a.reshape(depth, r, 2, LANES).transpose(0, 2, 1, 3).reshape(depth * 2, r, LANES)


def kernel(x, c, ctx, c_ctx, w_mod, b_mod, w_in, lam_q1, lam_k1, lam_q2, lam_k2, attn_norm_g, conv_w, conv_b, lru_wa, lru_ba, lru_wx, lru_bx, lru_lam, w_out, ln1_g, ln1_b, w_router, w_gate, w_up, w_down, ln2_g, ln2_b):
    b, n, d = x.shape
    cn = ctx.shape[1]
    depth = w_mod.shape[0]
    tt = n + cn
    tb = _largest_divisor(math.gcd(n, cn), (256, 128))
    assert n % tb == 0 and cn % tb == 0 and n % GRID_W == 0 and tt % LANES == 0
    nbl = n // tb
    cap_l = CAPACITY_FACTOR * n // N_EXPERTS
    cap_c = CAPACITY_FACTOR * cn // N_EXPERTS
    slots = cap_l + cap_c
    win = min(SLOT_WINDOW, slots)
    assert slots % BF16_ROWS == 0 and win % BF16_ROWS == 0 and n % TOKEN_TILE == 0 and cn % TOKEN_TILE == 0
    alpha = (2 * depth) ** 0.25

    rows = -(-(b + 1) // SUBLANES) * SUBLANES
    cc = jnp.concatenate([c, c_ctx[None, :], jnp.zeros((rows - b - 1, d), F32)], axis=0)
    mod = _modulation(cc, w_mod, b_mod)

    cos_t, sin_t = _rope_tables(n, cn)
    cs = _channel_dft()
    cmat, smat = _position_dft_mats(n, cn)

    w_in_bf = w_in.astype(BF16)
    w_out_bf = w_out.astype(BF16)
    wg_bf = w_gate.astype(BF16)
    wu_bf = w_up.astype(BF16)
    wd_bf = w_down.astype(BF16)
    wr_t = jnp.swapaxes(w_router, 1, 2).astype(BF16)
    lamv = jnp.stack([lam_q1, lam_k1, lam_q2, lam_k2], axis=1).astype(F32)
    lam_init = np.array([0.8 - 0.6 * math.exp(-0.3 * l) for l in range(depth)], np.float32)
    lin = jnp.asarray(np.broadcast_to(lam_init[:, None, None], (depth, 1, LANES)).copy())
    gain = attn_norm_g.reshape(depth * N_HEADS, 1, HEAD_V)
    conv_w2 = _split_halves(conv_w)
    conv_b2 = _split_halves(conv_b[:, None, :])
    lam2 = _split_halves(lru_lam)
    wgate = _block_diag_gates(lru_wa, lru_wx)
    bgate = _gate_bias(lru_ba, lru_bx)
    ln1g, ln1b = ln1_g[:, None, :], ln1_b[:, None, :]
    ln2g, ln2b = ln2_g[:, None, :], ln2_b[:, None, :]
    segments = ((0, n, cap_l, 0), (n, cn, cap_c, cap_l))

    xc = jnp.concatenate([x, ctx], axis=1)
    for l in range(depth):
        ml = mod[l]
        mod2 = jnp.stack([ml[:b], jnp.broadcast_to(ml[b][None, :], (b, 6 * d))], axis=1).reshape(2 * b, 1, 6 * d)
        qx, k, v, ux, gg, gc, gs = _in_proj(xc, mod2, w_in_bf, l, cos_t, sin_t, cs, nbl=nbl, tb=tb)
        att_l, att_c = _attention(qx, k, v, lamv, lin, gain, l, n=n, tb=tb)
        y = _rglru(ux, gg, conv_w2, conv_b2, wgate, bgate, lam2, l, n=n, cn=cn, tb=tb)
        fy = _position_dft(cmat, smat, gc, gs)
        x1, h2, logits_t = _out_proj(att_l, att_c, y, fy, xc, mod2, w_out_bf, ln1g, ln1b, wr_t, l,
                                     nbl=nbl, tb=tb, alpha=alpha)
        pos, gate, starts = _route(logits_t, segments=segments, n_slots=slots)
        starts = starts.reshape(-1)
        xs = _gather(starts, h2, pos, slots=slots, win=win)
        ys = _expert_ffn(xs, wg_bf, wu_bf, wd_bf, l, slots=slots)
        xc = _combine(starts, ys, pos, gate, x1, mod2, ln2g, ln2b, l, slots=slots, win=win,
                      n_lat_tiles=n // TOKEN_TILE, alpha=alpha, out_tokens=n if l == depth - 1 else tt)
    return xc
```

```python
import functools
import math

import numpy as np
import jax
import jax.numpy as jnp
from jax import lax
from jax.experimental import pallas as pl
from jax.experimental.pallas import tpu as pltpu

F32 = jnp.float32
BF16 = jnp.bfloat16
I32 = jnp.int32

GRID_W = 64
QK_DIM = 64
N_HEADS = 4
HEAD_V = 128
QK_W = 512
ATT_W = 512
LRU_W = 256
LRU_BLOCKS = 4
LRU_BLOCK_W = 64
LRU_C = 8.0
FNET_W = 256
FNET_GROUP_W = 64
IN_W = 2304
ROPE_BASE = 10000.0
N_EXPERTS = 16
CAPACITY_FACTOR = 2
LN_EPS = 1e-5
RMS_EPS = 1e-6
GELU_C = math.sqrt(2.0 / math.pi)
LOG2E = math.log2(math.e)

LANES = 128
SUBLANES = 8
BF16_ROWS = 16
KEY_CHUNK = 256
TOKEN_TILE = 256
SLOT_WINDOW = 64
VMEM_LIMIT = 56 << 20


def _params(sem, vmem=VMEM_LIMIT):
    return pltpu.CompilerParams(dimension_semantics=sem, vmem_limit_bytes=vmem)


def _ln(x):
    mu = jnp.mean(x, axis=-1, keepdims=True)
    xc = x - mu
    var = jnp.mean(xc * xc, axis=-1, keepdims=True)
    return xc * lax.rsqrt(var + LN_EPS)


def _largest_divisor(n, candidates):
    for c in candidates:
        if c <= n and n % c == 0:
            return c
    return n


def _mod_kernel(c_ref, w_ref, b_ref, o_ref):
    c = c_ref[...]
    s = c * jax.nn.sigmoid(c)
    o_ref[0] = jnp.dot(s, w_ref[0], precision=lax.Precision.HIGHEST, preferred_element_type=F32) + b_ref[0]


def _modulation(cc, w_mod, b_mod):
    depth, d, d6 = w_mod.shape
    rows = cc.shape[0]
    tn = 1024
    return pl.pallas_call(
        _mod_kernel,
        grid=(depth, d6 // tn),
        in_specs=[
            pl.BlockSpec((rows, d), lambda l, n: (0, 0)),
            pl.BlockSpec((1, d, tn), lambda l, n: (l, 0, n)),
            pl.BlockSpec((1, 1, tn), lambda l, n: (l, 0, n)),
        ],
        out_specs=pl.BlockSpec((1, rows, tn), lambda l, n: (l, 0, n)),
        out_shape=jax.ShapeDtypeStruct((depth, rows, d6), F32),
        compiler_params=_params(("arbitrary", "arbitrary")),
        name="modulation",
    )(cc, w_mod, b_mod.reshape(depth, 1, d6))


def _in_kernel(x_ref, mod_ref, w_ref, cos_ref, sin_ref, cs_ref,
               qx_ref, k_ref, v_ref, ux_ref, gg_ref, gc_ref, gs_ref, *, d):
    x = x_ref[0]
    m = mod_ref[0]
    h = _ln(x) * (1.0 + m[:, d:2 * d]) + m[:, 0:d]
    z = jnp.dot(h.astype(BF16), w_ref[0], preferred_element_type=F32)

    tb = x.shape[0]
    cos = cos_ref[...]
    sin = sin_ref[...]
    lane = lax.broadcasted_iota(I32, (tb, LANES), 1)
    first_half = (lane & 31) < 16
    low_map = lane < QK_DIM

    def rope(t):
        partner = jnp.where(first_half, pltpu.roll(t, LANES - 16, 1), pltpu.roll(t, 16, 1))
        return t * cos + partner * sin

    for p in range(N_HEADS):
        qp = rope(z[:, p * LANES:(p + 1) * LANES] * (QK_DIM ** -0.5 * LOG2E))
        qx_ref[0, :, (2 * p) * LANES:(2 * p + 1) * LANES] = jnp.where(low_map, qp, 0.0).astype(BF16)
        qx_ref[0, :, (2 * p + 1) * LANES:(2 * p + 2) * LANES] = jnp.where(low_map, 0.0, qp).astype(BF16)
        kp = rope(z[:, QK_W + p * LANES:QK_W + (p + 1) * LANES])
        k_ref[0, :, p * LANES:(p + 1) * LANES] = kp.astype(BF16)

    o = 2 * QK_W
    v_ref[0] = z[:, o:o + ATT_W].astype(BF16)
    o += ATT_W
    ux_ref[0] = z[:, o:o + LRU_W]
    o += LRU_W
    g = z[:, o:o + LRU_W]
    gg_ref[0] = 0.5 * g * (1.0 + jnp.tanh(GELU_C * (g + 0.044715 * (g * g * g))))
    o += LRU_W
    uf = z[:, o:o + FNET_W].astype(BF16)
    gcs = jnp.dot(uf, cs_ref[...], preferred_element_type=F32)
    gc_ref[...] = gcs[:, :FNET_W].astype(BF16)
    gs_ref[...] = gcs[:, FNET_W:].astype(BF16)


def _in_proj(xc, mod2, w_in_bf, l, cos_t, sin_t, cs, *, nbl, tb):
    b, tt, d = xc.shape
    nbt = tt // tb
    kern = functools.partial(_in_kernel, d=d)
    tok = lambda w: pl.BlockSpec((1, tb, w), lambda i, j: (i, j, 0))
    return pl.pallas_call(
        kern,
        grid=(b, nbt),
        in_specs=[
            tok(d),
            pl.BlockSpec((1, 1, 6 * d), lambda i, j: (2 * i + (j >= nbl).astype(I32), 0, 0)),
            pl.BlockSpec((1, d, IN_W), lambda i, j: (l, 0, 0)),
            pl.BlockSpec((tb, LANES), lambda i, j: (j, 0)),
            pl.BlockSpec((tb, LANES), lambda i, j: (j, 0)),
            pl.BlockSpec((FNET_W, 2 * FNET_W), lambda i, j: (0, 0)),
        ],
        out_specs=[
            tok(2 * QK_W), tok(QK_W), tok(ATT_W), tok(LRU_W), tok(LRU_W),
            pl.BlockSpec((tb, FNET_W), lambda i, j: (j, i)),
            pl.BlockSpec((tb, FNET_W), lambda i, j: (j, i)),
        ],
        out_shape=[
            jax.ShapeDtypeStruct((b, tt, 2 * QK_W), BF16),
            jax.ShapeDtypeStruct((b, tt, QK_W), BF16),
            jax.ShapeDtypeStruct((b, tt, ATT_W), BF16),
            jax.ShapeDtypeStruct((b, tt, LRU_W), F32),
            jax.ShapeDtypeStruct((b, tt, LRU_W), F32),
            jax.ShapeDtypeStruct((tt, b * FNET_W), BF16),
            jax.ShapeDtypeStruct((tt, b * FNET_W), BF16),
        ],
        compiler_params=_params(("arbitrary", "arbitrary")),
        name="in_proj",
    )(xc, mod2, w_in_bf, cos_t, sin_t, cs)


def _attn_kernel(ql_ref, qc_ref, k_ref, v_ref, lamv_ref, lin_ref, g_ref, ol_ref, oc_ref, vt_ref, s0_ref, s1_ref,
                 *, n, lat_steps, qb):
    j = pl.program_id(2)
    lv = lamv_ref[0]
    lam_init = lin_ref[0][:, 0:1]
    lam = (jnp.exp(jnp.sum(lv[0:1] * lv[1:2], axis=1, keepdims=True))
           - jnp.exp(jnp.sum(lv[2:3] * lv[3:4], axis=1, keepdims=True)) + lam_init)
    gain = g_ref[0] * (1.0 - lam_init)
    nt = (((1,), (1,)), ((), ()))
    tt = k_ref.shape[1]

    @pl.when(j == 0)
    def _():
        vt_ref[0:HEAD_V, :] = v_ref[0].astype(F32).T.astype(BF16)
        row = lax.broadcasted_iota(I32, (BF16_ROWS, tt), 0)
        vt_ref[HEAD_V:, :] = jnp.where(row == 0, 1.0, 0.0).astype(BF16)

    def attend(q_ref, o_ref, blocks, lo):
        chunks = [(c, min(KEY_CHUNK, tt - c)) for c in range(lo, tt, KEY_CHUNK)]
        tq = s0_ref.shape[1]
        streams = [(blk, mp) for blk in range(blocks) for mp in range(2)]
        s_refs = (s0_ref, s1_ref)

        def scores(i, c, w, m):
            blk, mp = streams[i]
            q = q_ref[0, blk * tq:(blk + 1) * tq, mp * LANES:(mp + 1) * LANES]
            st = lax.dot_general(k_ref[0, c:c + w, :], q, nt, preferred_element_type=F32)
            s_refs[i % 2][c:c + w, :] = st
            mc = jnp.max(st, axis=0, keepdims=True)
            return mc if m is None else jnp.maximum(m, mc)

        def weighted_values(i, c, w, m, acc):
            pt = jnp.exp2(s_refs[i % 2][c:c + w, :] - m).astype(BF16)
            part = jnp.dot(vt_ref[:, c:c + w], pt, preferred_element_type=F32)
            return part if acc is None else acc + part

        ms = [None] * len(streams)
        accs = [None] * len(streams)
        for stage in range(len(streams) + 1):
            for c, w in chunks:
                if stage >= 1:
                    accs[stage - 1] = weighted_values(stage - 1, c, w, ms[stage - 1], accs[stage - 1])
                if stage < len(streams):
                    ms[stage] = scores(stage, c, w, ms[stage])
            if stage >= 2 and stage % 2 == 0:
                blk = stage // 2 - 1
                a0, a1 = accs[stage - 2], accs[stage - 1]
                ot = (a0[0:HEAD_V] / a0[HEAD_V:HEAD_V + 1]) - lam * (a1[0:HEAD_V] / a1[HEAD_V:HEAD_V + 1])
                rt = ot * lax.rsqrt(jnp.mean(ot * ot, axis=0, keepdims=True) + RMS_EPS)
                o_ref[0, blk * tq:(blk + 1) * tq, :] = (rt.T * gain).astype(BF16)

    @pl.when(j < lat_steps)
    def _():
        attend(ql_ref, ol_ref, qb, 0)

    @pl.when(j >= lat_steps)
    def _():
        attend(qc_ref, oc_ref, qc_ref.shape[1] // s0_ref.shape[1], n)


def _attention(qx, k, v, lamv, lin, gain, l, *, n, tb):
    b, tt, _ = k.shape
    cn = tt - n
    qb = _largest_divisor(n // tb, (8, 4, 2, 1))
    lat_steps = n // (qb * tb)
    assert n % cn == 0 and cn % tb == 0
    kern = functools.partial(_attn_kernel, n=n, lat_steps=lat_steps, qb=qb)
    lat = lambda w: pl.BlockSpec((1, qb * tb, w), lambda i, h, j: (i, jnp.minimum(j, lat_steps - 1), h))
    ctx = lambda w: pl.BlockSpec((1, cn, w), lambda i, h, j: (i, n // cn, h))
    return pl.pallas_call(
        kern,
        grid=(b, N_HEADS, lat_steps + 1),
        in_specs=[
            lat(2 * LANES), ctx(2 * LANES),
            pl.BlockSpec((1, tt, LANES), lambda i, h, j: (i, 0, h)),
            pl.BlockSpec((1, tt, LANES), lambda i, h, j: (i, 0, h)),
            pl.BlockSpec((1, 4, QK_DIM), lambda i, h, j: (l, 0, 0)),
            pl.BlockSpec((1, 1, LANES), lambda i, h, j: (l, 0, 0)),
            pl.BlockSpec((1, 1, HEAD_V), lambda i, h, j: (l * N_HEADS + h, 0, 0)),
        ],
        out_specs=[lat(LANES), pl.BlockSpec((1, cn, LANES), lambda i, h, j: (i, 0, h))],
        out_shape=[jax.ShapeDtypeStruct((b, n, ATT_W), BF16), jax.ShapeDtypeStruct((b, cn, ATT_W), BF16)],
        scratch_shapes=[pltpu.VMEM((HEAD_V + BF16_ROWS, tt), BF16),
                        pltpu.VMEM((tt, tb), F32), pltpu.VMEM((tt, tb), F32)],
        compiler_params=_params(("arbitrary", "arbitrary", "arbitrary")),
        name="diff_attention",
    )(qx, qx, k, v, lamv, lin, gain)


def _lru_kernel(ux_ref, gg_ref, cw_ref, cb_ref, wg_ref, bg_ref, lam_ref, y_ref,
                a_f, b_f, a_b, b_b, *, n, cn, r):
    tt = n + cn
    w = LANES
    cw = cw_ref[0]
    cb = cb_ref[0]
    bg = bg_ref[0]
    neg_lam = -lam_ref[0]
    softplus = jnp.maximum(neg_lam, 0.0) + jnp.log(1.0 + jnp.exp(-jnp.abs(neg_lam)))
    row8 = lax.broadcasted_iota(I32, (r, w), 0) & (SUBLANES - 1)
    ext_rows = r + 2 * SUBLANES

    def local_scan(a, bb, reverse):
        for s in (1, 2, 4):
            if reverse:
                a_sh = pltpu.roll(a, r - s, 0)
                b_sh = pltpu.roll(bb, r - s, 0)
                valid = row8 < SUBLANES - s
            else:
                a_sh = pltpu.roll(a, s, 0)
                b_sh = pltpu.roll(bb, s, 0)
                valid = row8 >= s
            bb = jnp.where(valid, a * b_sh + bb, bb)
            a = jnp.where(valid, a * a_sh, a)
        return a, bb

    def gates_chunk(c, carry):
        r0 = pl.multiple_of(c * r, r)
        seg_start = jnp.logical_or(r0 == 0, r0 == n)
        seg_end = jnp.logical_or(r0 + r == n, r0 + r == tt)
        main = ux_ref[0, pl.ds(r0, r), :]
        prev = ux_ref[0, pl.ds(pl.multiple_of(jnp.maximum(r0 - SUBLANES, 0), SUBLANES), SUBLANES), :]
        nxt = ux_ref[0, pl.ds(pl.multiple_of(jnp.minimum(r0 + r, tt - SUBLANES), SUBLANES), SUBLANES), :]
        prev = jnp.where(seg_start, 0.0, prev)
        nxt = jnp.where(seg_end, 0.0, nxt)
        ext = jnp.concatenate([prev, main, nxt], axis=0)
        u = cb
        for t in range(4):
            sh = (2 - t) % ext_rows
            win = ext if sh == 0 else pltpu.roll(ext, sh, 0)
            u = u + cw[t:t + 1, :] * win[SUBLANES:SUBLANES + r, :]
        zz = jnp.dot(u.astype(BF16), wg_ref[0], preferred_element_type=F32) + bg
        for dr, (a_s, b_s) in enumerate(((a_f, b_f), (a_b, b_b))):
            rg = jax.nn.sigmoid(zz[:, (2 * dr) * w:(2 * dr + 1) * w])
            ig = jax.nn.sigmoid(zz[:, (2 * dr + 1) * w:(2 * dr + 2) * w])
            a = jnp.exp(-LRU_C * rg * softplus[dr:dr + 1, :])
            bb = jnp.sqrt(1.0 - a * a) * ig * u
            a, bb = local_scan(a, bb, reverse=(dr == 1))
            a_s[pl.ds(r0, r), :] = a
            b_s[pl.ds(r0, r), :] = bb
        return carry

    lax.fori_loop(0, tt // r, gates_chunk, 0)

    def seg_scan(first_tile, ntiles, cf, cbk):
        def body(i, carry):
            cf, cbk = carry
            rf = pl.multiple_of((first_tile + i) * SUBLANES, SUBLANES)
            hf = b_f[pl.ds(rf, SUBLANES), :] + a_f[pl.ds(rf, SUBLANES), :] * cf
            b_f[pl.ds(rf, SUBLANES), :] = hf
            rb = pl.multiple_of((first_tile + ntiles - 1 - i) * SUBLANES, SUBLANES)
            hb = b_b[pl.ds(rb, SUBLANES), :] + a_b[pl.ds(rb, SUBLANES), :] * cbk
            b_b[pl.ds(rb, SUBLANES), :] = hb
            return hf[SUBLANES - 1:SUBLANES, :], hb[0:1, :]
        return lax.fori_loop(0, ntiles, body, (cf, cbk))

    zero = jnp.zeros((1, w), F32)
    cf, cbk = seg_scan(n // SUBLANES, cn // SUBLANES, zero, zero)
    seg_scan(0, n // SUBLANES, cf, cbk)

    def out_chunk(c, carry):
        r0 = pl.multiple_of(c * r, r)
        y = (b_f[pl.ds(r0, r), :] + b_b[pl.ds(r0, r), :]) * gg_ref[0, pl.ds(r0, r), :]
        y_ref[0, pl.ds(r0, r), :] = y.astype(BF16)
        return carry

    lax.fori_loop(0, tt // r, out_chunk, 0)


def _rglru(ux, gg, conv_w2, conv_b2, wgate, bgate, lam2, l, *, n, cn, tb):
    b, tt, _ = ux.shape
    kern = functools.partial(_lru_kernel, n=n, cn=cn, r=tb)
    half = lambda: pl.BlockSpec((1, tt, LANES), lambda i, hh: (i, 0, hh))
    return pl.pallas_call(
        kern,
        grid=(b, 2),
        in_specs=[
            half(), half(),
            pl.BlockSpec((1, 4, LANES), lambda i, hh: (2 * l + hh, 0, 0)),
            pl.BlockSpec((1, 1, LANES), lambda i, hh: (2 * l + hh, 0, 0)),
            pl.BlockSpec((1, LANES, 4 * LANES), lambda i, hh: (2 * l + hh, 0, 0)),
            pl.BlockSpec((1, 1, 4 * LANES), lambda i, hh: (2 * l + hh, 0, 0)),
            pl.BlockSpec((1, 2, LANES), lambda i, hh: (2 * l + hh, 0, 0)),
        ],
        out_specs=half(),
        out_shape=jax.ShapeDtypeStruct((b, tt, LRU_W), BF16),
        scratch_shapes=[pltpu.VMEM((tt, LANES), F32)] * 4,
        compiler_params=_params(("arbitrary", "arbitrary")),
        name="rglru",
    )(ux, gg, conv_w2, conv_b2, wgate, bgate, lam2)


def _dft_kernel(c_ref, s_ref, gc_ref, gs_ref, o_ref, acc_ref):
    kk = pl.program_id(2)

    @pl.when(kk == 0)
    def _():
        acc_ref[...] = jnp.zeros_like(acc_ref)

    acc_ref[...] += (jnp.dot(c_ref[...], gc_ref[...], preferred_element_type=F32)
                     + jnp.dot(s_ref[...], gs_ref[...], preferred_element_type=F32))

    @pl.when(kk == pl.num_programs(2) - 1)
    def _():
        o_ref[...] = acc_ref[...].astype(BF16)


def _position_dft(cmat, smat, gc, gs):
    tt = cmat.shape[0]
    nn = gc.shape[1]
    tm = _largest_divisor(tt, (1088, 640, 512, 256, 128))
    tk = _largest_divisor(tt, (2176, 640, 512, 256, 128))
    tn = _largest_divisor(nn, (512, 256))
    return pl.pallas_call(
        _dft_kernel,
        grid=(tt // tm, nn // tn, tt // tk),
        in_specs=[
            pl.BlockSpec((tm, tk), lambda i, j, k: (i, k)),
            pl.BlockSpec((tm, tk), lambda i, j, k: (i, k)),
            pl.BlockSpec((tk, tn), lambda i, j, k: (k, j)),
            pl.BlockSpec((tk, tn), lambda i, j, k: (k, j)),
        ],
        out_specs=pl.BlockSpec((tm, tn), lambda i, j, k: (i, j)),
        out_shape=jax.ShapeDtypeStruct((tt, nn), BF16),
        scratch_shapes=[pltpu.VMEM((tm, tn), F32)],
        compiler_params=_params(("arbitrary", "arbitrary", "arbitrary")),
        name="position_dft",
    )(cmat, smat, gc, gs)


def _out_kernel(attl_ref, attc_ref, y_ref, f_ref, x_ref, mod_ref, w_ref, g_ref, b_ref, wr_ref,
                x1_ref, h2_ref, lg_ref, *, d, alpha, nbl):
    w = w_ref[0]
    att = jnp.where(pl.program_id(1) >= nbl, attc_ref[0], attl_ref[0])
    mix = (jnp.dot(att, w[0:ATT_W], preferred_element_type=F32)
           + jnp.dot(y_ref[0], w[ATT_W:ATT_W + LRU_W], preferred_element_type=F32)
           + jnp.dot(f_ref[...], w[ATT_W + LRU_W:], preferred_element_type=F32))
    m = mod_ref[0]
    x1 = _ln(alpha * x_ref[0] + m[:, 2 * d:3 * d] * mix) * g_ref[0] + b_ref[0]
    x1_ref[0] = x1
    h2 = (_ln(x1) * (1.0 + m[:, 4 * d:5 * d]) + m[:, 3 * d:4 * d]).astype(BF16)
    h2_ref[0] = h2
    lg_ref[0] = lax.dot_general(wr_ref[0], h2, (((1,), (1,)), ((), ())), preferred_element_type=F32)


def _out_proj(att_l, att_c, y, fy, xc, mod2, w_out_bf, ln_g, ln_b, wr_t, l, *, nbl, tb, alpha):
    b, tt, d = xc.shape
    nbt = tt // tb
    kern = functools.partial(_out_kernel, d=d, alpha=alpha, nbl=nbl)
    tok = lambda w: pl.BlockSpec((1, tb, w), lambda i, j: (i, j, 0))
    return pl.pallas_call(
        kern,
        grid=(b, nbt),
        in_specs=[
            pl.BlockSpec((1, tb, ATT_W), lambda i, j: (i, jnp.minimum(j, nbl - 1), 0)),
            pl.BlockSpec((1, tb, ATT_W), lambda i, j: (i, jnp.maximum(j - nbl, 0), 0)),
            tok(LRU_W),
            pl.BlockSpec((tb, FNET_W), lambda i, j: (j, i)),
            tok(d),
            pl.BlockSpec((1, 1, 6 * d), lambda i, j: (2 * i + (j >= nbl).astype(I32), 0, 0)),
            pl.BlockSpec((1, d, d), lambda i, j: (l, 0, 0)),
            pl.BlockSpec((1, 1, d), lambda i, j: (l, 0, 0)),
            pl.BlockSpec((1, 1, d), lambda i, j: (l, 0, 0)),
            pl.BlockSpec((1, N_EXPERTS, d), lambda i, j: (l, 0, 0)),
        ],
        out_specs=[tok(d), tok(d), pl.BlockSpec((1, N_EXPERTS, tb), lambda i, j: (i, 0, j))],
        out_shape=[
            jax.ShapeDtypeStruct((b, tt, d), F32),
            jax.ShapeDtypeStruct((b, tt, d), BF16),
            jax.ShapeDtypeStruct((b, N_EXPERTS, tt), F32),
        ],
        compiler_params=_params(("arbitrary", "arbitrary")),
        name="out_proj",
    )(att_l, att_c, y, fy, xc, mod2, w_out_bf, ln_g, ln_b, wr_t)


def _route_kernel(lg_ref, pos_ref, gate_ref, starts_ref, *, segments, n_slots):
    lg = lg_ref[0]
    lane_id = lax.broadcasted_iota(I32, (N_EXPERTS, LANES), 1)
    starts = jnp.zeros((N_EXPERTS, LANES), I32)
    chunks_per_tile = TOKEN_TILE // LANES
    e = jnp.exp(lg - jnp.max(lg, axis=0, keepdims=True))
    s = e / jnp.sum(e, axis=0, keepdims=True)
    ri = lax.broadcasted_iota(I32, (LANES, LANES), 0)
    ci = lax.broadcasted_iota(I32, (LANES, LANES), 1)
    strict_upper = jnp.where(ri < ci, 1.0, 0.0).astype(BF16)

    for lo, t, cap, base in segments:
        ss = s[:, lo:lo + t]
        bits = pltpu.bitcast(ss, I32)
        capf = float(cap)

        def search(i, thr, bits=bits, capf=capf):
            cand = thr | jnp.left_shift(jnp.int32(1), 30 - i)
            cnt = jnp.sum(jnp.where(bits >= cand, 1.0, 0.0), axis=1, keepdims=True)
            return jnp.where(cnt >= capf, cand, thr)

        thr = lax.fori_loop(0, 31, search, jnp.zeros((N_EXPERTS, 1), I32))
        need = capf - jnp.sum(jnp.where(bits > thr, 1.0, 0.0), axis=1, keepdims=True)
        off_eq = jnp.zeros((N_EXPERTS, 1), F32)
        off_sel = jnp.zeros((N_EXPERTS, 1), F32)
        for c in range(t // LANES):
            sl = slice(c * LANES, (c + 1) * LANES)
            tile, sub = divmod(lo // LANES + c, chunks_per_tile)
            if sub == 0:
                starts = jnp.where(lane_id == tile, off_sel.astype(I32) + base, starts)
            bits_c = bits[:, sl]
            eq = bits_c == thr
            eq_c = jnp.where(eq, 1.0, 0.0)
            rank_eq = jnp.dot(eq_c.astype(BF16), strict_upper, preferred_element_type=F32) + off_eq
            off_eq = off_eq + jnp.sum(eq_c, axis=1, keepdims=True)
            sel = jnp.logical_or(bits_c > thr, jnp.logical_and(eq, rank_eq < need))
            sel_c = jnp.where(sel, 1.0, 0.0)
            slot = jnp.dot(sel_c.astype(BF16), strict_upper, preferred_element_type=F32) + off_sel
            off_sel = off_sel + jnp.sum(sel_c, axis=1, keepdims=True)
            osl = slice(sub * LANES, (sub + 1) * LANES)
            pos_ref[0, tile, :, osl] = jnp.where(sel, slot.astype(I32) + base, -1)
            gate_ref[0, tile, :, osl] = jnp.where(sel, ss[:, sl], 0.0)

    n_tiles = lg.shape[1] // TOKEN_TILE
    starts_ref[0] = jnp.where(lane_id == n_tiles, n_slots, starts)


def _route(logits_t, *, segments, n_slots):
    b, ne, tt = logits_t.shape
    nt = tt // TOKEN_TILE
    assert nt < LANES and all(lo % TOKEN_TILE == 0 and t % TOKEN_TILE == 0 for lo, t, _, _ in segments)
    kern = functools.partial(_route_kernel, segments=segments, n_slots=n_slots)
    tiled = lambda: pl.BlockSpec((1, nt, ne, TOKEN_TILE), lambda i: (i, 0, 0, 0))
    return pl.pallas_call(
        kern,
        grid=(b,),
        in_specs=[pl.BlockSpec((1, ne, tt), lambda i: (i, 0, 0))],
        out_specs=[tiled(), tiled(), pl.BlockSpec((1, ne, LANES), lambda i: (i, 0, 0))],
        out_shape=[jax.ShapeDtypeStruct((b, nt, ne, TOKEN_TILE), I32),
                   jax.ShapeDtypeStruct((b, nt, ne, TOKEN_TILE), F32),
                   jax.ShapeDtypeStruct((b, ne, LANES), I32)],
        compiler_params=_params(("arbitrary",)),
        name="route",
    )(logits_t)


def _window_start(starts_ref, base_idx, tile, w, win, limit):
    c0 = starts_ref[base_idx + tile]
    lo = ((c0 >> 4) << 4) + w * win
    return lo, pl.multiple_of(jnp.minimum(lo, limit), BF16_ROWS)


def _window_count(starts_ref, base_idx, tiles, win):
    nw = jnp.int32(1)
    for tile in tiles:
        c0 = starts_ref[base_idx + tile]
        c1 = starts_ref[base_idx + tile + 1]
        nw = jnp.maximum(nw, (c1 - ((c0 >> 4) << 4) + win - 1) // win)
    return nw


def _gather_kernel(starts_ref, h_ref, pos_ref, xs_ref, acc_ref, *, slots, win, group):
    b = pl.program_id(0)
    eg = pl.program_id(1)
    nt = pos_ref.shape[1]
    acc_ref[...] = jnp.zeros_like(acc_ref)
    rel = lax.broadcasted_iota(I32, (win, TOKEN_TILE), 0)
    bases = [(b * N_EXPERTS + eg * group + i) * LANES for i in range(group)]
    nw = jnp.int32(1)
    for i in range(group):
        nw = jnp.maximum(nw, _window_count(starts_ref, bases[i], range(nt), win))

    def window_pass(w, carry):
        for tile in range(nt):
            onehots, offs = [], []
            for i in range(group):
                lo, start = _window_start(starts_ref, bases[i], tile, w, win, slots)
                prow = pos_ref[0, tile, pl.ds(eg * group + i, 1), :] - start
                onehots.append(jnp.where(rel == prow, 1.0, 0.0).astype(BF16))
                offs.append(start)
            res = jnp.dot(jnp.concatenate(onehots, axis=0), h_ref[0, tile * TOKEN_TILE:(tile + 1) * TOKEN_TILE, :],
                          preferred_element_type=F32)
            for i in range(group):
                acc_ref[i, pl.ds(offs[i], win), :] += res[i * win:(i + 1) * win].astype(BF16)
        return carry

    lax.fori_loop(0, nw, window_pass, 0)
    xs_ref[...] = acc_ref[:, 0:slots, :]


def _gather(starts, h2, pos, *, slots, win, group=8):
    b, tt, d = h2.shape
    nt = tt // TOKEN_TILE
    kern = functools.partial(_gather_kernel, slots=slots, win=win, group=group)
    return pl.pallas_call(
        kern,
        grid_spec=pltpu.PrefetchScalarGridSpec(
            num_scalar_prefetch=1,
            grid=(b, N_EXPERTS // group),
            in_specs=[
                pl.BlockSpec((1, tt, d), lambda i, g, s: (i, 0, 0)),
                pl.BlockSpec((1, nt, N_EXPERTS, TOKEN_TILE), lambda i, g, s: (i, 0, 0, 0)),
            ],
            out_specs=pl.BlockSpec((group, slots, d), lambda i, g, s: (g, i, 0)),
            scratch_shapes=[pltpu.VMEM((group, slots + win, d), BF16)],
        ),
        out_shape=jax.ShapeDtypeStruct((N_EXPERTS, b * slots, d), BF16),
        compiler_params=_params(("arbitrary", "arbitrary")),
        name="moe_gather",
    )(starts, h2, pos)


def _ffn_kernel(xs_ref, wg_ref, wu_ref, wd_ref, ys_ref, *, fchunk):
    xs = xs_ref[0]
    f = wg_ref.shape[-1]
    acc = None
    for c in range(f // fchunk):
        sl = slice(c * fchunk, (c + 1) * fchunk)
        a = jnp.dot(xs, wg_ref[0, 0, :, sl], preferred_element_type=F32)
        u = jnp.dot(xs, wu_ref[0, 0, :, sl], preferred_element_type=F32)
        hm = (a * jax.nn.sigmoid(a) * u).astype(BF16)
        y = jnp.dot(hm, wd_ref[0, 0, sl, :], preferred_element_type=F32)
        acc = y if acc is None else acc + y
    ys_ref[0] = acc.astype(BF16)


def _expert_ffn(xs, wg, wu, wd, l, *, slots):
    ne, rows, d = xs.shape
    f = wg.shape[-1]
    nb = rows // slots
    kern = functools.partial(_ffn_kernel, fchunk=_largest_divisor(f, (512,)))
    return pl.pallas_call(
        kern,
        grid=(ne, nb),
        in_specs=[
            pl.BlockSpec((1, slots, d), lambda e, i: (e, i, 0)),
            pl.BlockSpec((1, 1, d, f), lambda e, i: (l, e, 0, 0)),
            pl.BlockSpec((1, 1, d, f), lambda e, i: (l, e, 0, 0)),
            pl.BlockSpec((1, 1, f, d), lambda e, i: (l, e, 0, 0)),
        ],
        out_specs=pl.BlockSpec((1, slots, d), lambda e, i: (e, i, 0)),
        out_shape=jax.ShapeDtypeStruct((ne, rows, d), BF16),
        compiler_params=_params(("arbitrary", "arbitrary")),
        name="expert_ffn",
    )(xs, wg, wu, wd)


def _combine_kernel(starts_ref, ys_ref, pos_ref, gate_ref, x1_ref, mod_ref, g_ref, b_ref, o_ref,
                    *, slots, win, d, alpha):
    b = pl.program_id(0)
    tile = pl.program_id(1)
    rel = lax.broadcasted_iota(I32, (win, TOKEN_TILE), 0)
    bases = [(b * N_EXPERTS + e) * LANES for e in range(N_EXPERTS)]
    nw = jnp.int32(1)
    for e in range(N_EXPERTS):
        c0 = starts_ref[bases[e] + tile]
        c1 = starts_ref[bases[e] + tile + 1]
        nw = jnp.maximum(nw, (c1 - ((c0 >> 4) << 4) + win - 1) // win)

    def window_pass(w, moe):
        gated, rows = [], []
        for e in range(N_EXPERTS):
            lo, start = _window_start(starts_ref, bases[e], tile, w, win, slots - win)
            prow = pos_ref[0, 0, e:e + 1, :]
            prow = jnp.where(jnp.logical_and(prow >= lo, prow < lo + win), prow - start, -1)
            gated.append(jnp.where(rel == prow, gate_ref[0, 0, e:e + 1, :], 0.0).astype(BF16))
            rows.append(ys_ref[e, pl.ds(start, win), :])
        return moe + lax.dot_general(jnp.concatenate(gated, axis=0), jnp.concatenate(rows, axis=0),
                                     (((0,), (0,)), ((), ())), preferred_element_type=F32)

    moe = lax.fori_loop(0, nw, window_pass, jnp.zeros((TOKEN_TILE, d), F32))
    m = mod_ref[0]
    o_ref[0] = _ln(alpha * x1_ref[0] + m[:, 5 * d:6 * d] * moe) * g_ref[0] + b_ref[0]


def _combine(starts, ys, pos, gate, x1, mod2, ln_g, ln_b, l, *, slots, win, n_lat_tiles, alpha, out_tokens):
    b, _, d = x1.shape
    kern = functools.partial(_combine_kernel, slots=slots, win=win, d=d, alpha=alpha)
    tiled = lambda: pl.BlockSpec((1, 1, N_EXPERTS, TOKEN_TILE), lambda i, t, s: (i, t, 0, 0))
    tok = lambda: pl.BlockSpec((1, TOKEN_TILE, d), lambda i, t, s: (i, t, 0))
    return pl.pallas_call(
        kern,
        grid_spec=pltpu.PrefetchScalarGridSpec(
            num_scalar_prefetch=1,
            grid=(b, out_tokens // TOKEN_TILE),
            in_specs=[
                pl.BlockSpec((N_EXPERTS, slots, d), lambda i, t, s: (0, i, 0)),
                tiled(), tiled(), tok(),
                pl.BlockSpec((1, 1, 6 * d), lambda i, t, s: (2 * i + (t >= n_lat_tiles).astype(I32), 0, 0)),
                pl.BlockSpec((1, 1, d), lambda i, t, s: (l, 0, 0)),
                pl.BlockSpec((1, 1, d), lambda i, t, s: (l, 0, 0)),
            ],
            out_specs=tok(),
        ),
        out_shape=jax.ShapeDtypeStruct((b, out_tokens, d), F32),
        compiler_params=_params(("arbitrary", "arbitrary")),
        name="moe_combine",
    )(starts, ys, pos, gate, x1, mod2, ln_g, ln_b)


def _rope_tables(n, cn):
    lane = np.arange(LANES)
    within = lane % QK_DIM
    use_col = (within // 32) == 1
    first_half = (within % 32) < 16
    inv = ROPE_BASE ** (-(within % 16).astype(np.float64) / 16.0)
    pos = np.arange(n)
    coord = np.where(use_col[None, :], (pos % GRID_W)[:, None], (pos // GRID_W)[:, None]).astype(np.float32)
    ang = jnp.asarray(coord) * jnp.asarray(inv.astype(np.float32))[None, :]
    cos = jnp.cos(ang)
    sin = jnp.where(jnp.asarray(first_half)[None, :], -jnp.sin(ang), jnp.sin(ang))
    cos = jnp.concatenate([cos, jnp.ones((cn, LANES), F32)], axis=0)
    sin = jnp.concatenate([sin, jnp.zeros((cn, LANES), F32)], axis=0)
    return cos, sin


def _channel_dft():
    idx = np.arange(FNET_W)
    same = (idx[:, None] // FNET_GROUP_W) == (idx[None, :] // FNET_GROUP_W)
    ang = 2.0 * np.pi * ((idx[:, None] % FNET_GROUP_W) * (idx[None, :] % FNET_GROUP_W) % FNET_GROUP_W) / FNET_GROUP_W
    cs = np.concatenate([np.where(same, np.cos(ang), 0.0), np.where(same, np.sin(ang), 0.0)], axis=1)
    return jnp.asarray(cs.astype(np.float32)).astype(BF16)


def _position_dft_mats(n, cn):
    def segment(t):
        g = math.gcd(t, 64)
        kk = jnp.arange(t, dtype=I32)

        def table(m):
            ph = ((kk[:, None] * m[None, :]) % t).astype(F32) * (2.0 * math.pi / t)
            return jnp.cos(ph), jnp.sin(ph)

        ch, sh = table(jnp.arange(t // g, dtype=I32) * g)
        cl, sl = table(jnp.arange(g, dtype=I32))
        scale = 1.0 / math.sqrt(t * FNET_GROUP_W)
        cmat = (ch[:, :, None] * cl[:, None, :] - sh[:, :, None] * sl[:, None, :]).reshape(t, t) * scale
        smat = (sh[:, :, None] * cl[:, None, :] + ch[:, :, None] * sl[:, None, :]).reshape(t, t) * (-scale)
        return cmat.astype(BF16), smat.astype(BF16)

    def block_diag(a, c):
        top = jnp.concatenate([a, jnp.zeros((n, cn), BF16)], axis=1)
        bot = jnp.concatenate([jnp.zeros((cn, n), BF16), c], axis=1)
        return jnp.concatenate([top, bot], axis=0)

    (cl_, sl_), (cc_, sc_) = segment(n), segment(cn)
    return block_diag(cl_, cc_), block_diag(sl_, sc_)


def _block_diag_gates(wa, wx):
    depth = wa.shape[0]

    def dense(wb):
        eye = jnp.eye(LRU_BLOCKS, dtype=wb.dtype)
        return jnp.einsum('lncd,nm->lncmd', wb, eye).reshape(depth, LRU_W, LRU_W)

    halves = []
    for hh in range(2):
        sl = slice(hh * LANES, (hh + 1) * LANES)
        cols = [dense(wmat[:, dr])[:, sl, sl] for dr in range(2) for wmat in (wa, wx)]
        halves.append(jnp.concatenate(cols, axis=-1))
    return jnp.stack(halves, axis=1).reshape(depth * 2, LANES, 4 * LANES).astype(BF16)


def _gate_bias(ba, bx):
    depth = ba.shape[0]
    halves = []
    for hh in range(2):
        sl = slice(hh * LANES, (hh + 1) * LANES)
        halves.append(jnp.concatenate([bvec[:, dr, sl] for dr in range(2) for bvec in (ba, bx)], axis=-1))
    return jnp.stack(halves, axis=1).reshape(depth * 2, 1, 4 * LANES)


def _split_halves(a):
    depth, r, _ = a.shape
    return a.reshape(depth, r, 2, LANES).transpose(0, 2, 1, 3).reshape(depth * 2, r, LANES)


def kernel(x, c, ctx, c_ctx, w_mod, b_mod, w_in, lam_q1, lam_k1, lam_q2, lam_k2, attn_norm_g, conv_w, conv_b, lru_wa, lru_ba, lru_wx, lru_bx, lru_lam, w_out, ln1_g, ln1_b, w_router, w_gate, w_up, w_down, ln2_g, ln2_b):
    b, n, d = x.shape
    cn = ctx.shape[1]
    depth = w_mod.shape[0]
    tt = n + cn
    tb = _largest_divisor(math.gcd(n, cn), (256, 128))
    assert n % tb == 0 and cn % tb == 0 and n % GRID_W == 0 and tt % LANES == 0
    nbl = n // tb
    cap_l = CAPACITY_FACTOR * n // N_EXPERTS
    cap_c = CAPACITY_FACTOR * cn // N_EXPERTS
    slots = cap_l + cap_c
    win = min(SLOT_WINDOW, slots)
    assert slots % BF16_ROWS == 0 and win % BF16_ROWS == 0 and n % TOKEN_TILE == 0 and cn % TOKEN_TILE == 0
    alpha = (2 * depth) ** 0.25

    rows = -(-(b + 1) // SUBLANES) * SUBLANES
    cc = jnp.concatenate([c, c_ctx[None, :], jnp.zeros((rows - b - 1, d), F32)], axis=0)
    mod = _modulation(cc, w_mod, b_mod)

    cos_t, sin_t = _rope_tables(n, cn)
    cs = _channel_dft()
    cmat, smat = _position_dft_mats(n, cn)

    w_in_bf = w_in.astype(BF16)
    w_out_bf = w_out.astype(BF16)
    wg_bf = w_gate.astype(BF16)
    wu_bf = w_up.astype(BF16)
    wd_bf = w_down.astype(BF16)
    wr_t = jnp.swapaxes(w_router, 1, 2).astype(BF16)
    lamv = jnp.stack([lam_q1, lam_k1, lam_q2, lam_k2], axis=1).astype(F32)
    lam_init = np.array([0.8 - 0.6 * math.exp(-0.3 * l) for l in range(depth)], np.float32)
    lin = jnp.asarray(np.broadcast_to(lam_init[:, None, None], (depth, 1, LANES)).copy())
    gain = attn_norm_g.reshape(depth * N_HEADS, 1, HEAD_V)
    conv_w2 = _split_halves(conv_w)
    conv_b2 = _split_halves(conv_b[:, None, :])
    lam2 = _split_halves(lru_lam)
    wgate = _block_diag_gates(lru_wa, lru_wx)
    bgate = _gate_bias(lru_ba, lru_bx)
    ln1g, ln1b = ln1_g[:, None, :], ln1_b[:, None, :]
    ln2g, ln2b = ln2_g[:, None, :], ln2_b[:, None, :]
    segments = ((0, n, cap_l, 0), (n, cn, cap_c, cap_l))

    xc = jnp.concatenate([x, ctx], axis=1)
    for l in range(depth):
        ml = mod[l]
        mod2 = jnp.stack([ml[:b], jnp.broadcast_to(ml[b][None, :], (b, 6 * d))], axis=1).reshape(2 * b, 1, 6 * d)
        qx, k, v, ux, gg, gc, gs = _in_proj(xc, mod2, w_in_bf, l, cos_t, sin_t, cs, nbl=nbl, tb=tb)
        att_l, att_c = _attention(qx, k, v, lamv, lin, gain, l, n=n, tb=tb)
        y = _rglru(ux, gg, conv_w2, conv_b2, wgate, bgate, lam2, l, n=n, cn=cn, tb=tb)
        fy = _position_dft(cmat, smat, gc, gs)
        x1, h2, logits_t = _out_proj(att_l, att_c, y, fy, xc, mod2, w_out_bf, ln1g, ln1b, wr_t, l,
                                     nbl=nbl, tb=tb, alpha=alpha)
        pos, gate, starts = _route(logits_t, segments=segments, n_slots=slots)
        starts = starts.reshape(-1)
        xs = _gather(starts, h2, pos, slots=slots, win=win)
        ys = _expert_ffn(xs, wg_bf, wu_bf, wd_bf, l, slots=slots)
        xc = _combine(starts, ys, pos, gate, x1, mod2, ln2g, ln2b, l, slots=slots, win=win,
                      n_lat_tiles=n // TOKEN_TILE, alpha=alpha, out_tokens=n if l == depth - 1 else tt)
    return xc
```

```python
import functools
import math

import numpy as np
import jax
import jax.numpy as jnp
from jax import lax
from jax.experimental import pallas as pl
from jax.experimental.pallas import tpu as pltpu

F32 = jnp.float32
BF16 = jnp.bfloat16
I32 = jnp.int32

GRID_W = 64
QK_DIM = 64
N_HEADS = 4
HEAD_V = 128
QK_W = 512
ATT_W = 512
LRU_W = 256
LRU_BLOCKS = 4
LRU_BLOCK_W = 64
LRU_C = 8.0
FNET_W = 256
FNET_GROUP_W = 64
IN_W = 2304
ROPE_BASE = 10000.0
N_EXPERTS = 16
CAPACITY_FACTOR = 2
LN_EPS = 1e-5
RMS_EPS = 1e-6
GELU_C = math.sqrt(2.0 / math.pi)
LOG2E = math.log2(math.e)

LANES = 128
SUBLANES = 8
BF16_ROWS = 16
KEY_CHUNK = 256
OUT_PIECES = 4
TOKEN_TILE = 256
SLOT_WINDOW = 64
VMEM_LIMIT = 56 << 20


def _params(sem, vmem=VMEM_LIMIT):
    return pltpu.CompilerParams(dimension_semantics=sem, vmem_limit_bytes=vmem)


def _ln(x):
    mu = jnp.mean(x, axis=-1, keepdims=True)
    xc = x - mu
    var = jnp.mean(xc * xc, axis=-1, keepdims=True)
    return xc * lax.rsqrt(var + LN_EPS)


def _largest_divisor(n, candidates):
    for c in candidates:
        if c <= n and n % c == 0:
            return c
    return n


def _mod_kernel(c_ref, w_ref, b_ref, o_ref):
    c = c_ref[...]
    s = c * jax.nn.sigmoid(c)
    o_ref[0] = jnp.dot(s, w_ref[0], precision=lax.Precision.HIGHEST, preferred_element_type=F32) + b_ref[0]


def _modulation(cc, w_mod, b_mod):
    depth, d, d6 = w_mod.shape
    rows = cc.shape[0]
    tn = 1024
    return pl.pallas_call(
        _mod_kernel,
        grid=(depth, d6 // tn),
        in_specs=[
            pl.BlockSpec((rows, d), lambda l, n: (0, 0)),
            pl.BlockSpec((1, d, tn), lambda l, n: (l, 0, n)),
            pl.BlockSpec((1, 1, tn), lambda l, n: (l, 0, n)),
        ],
        out_specs=pl.BlockSpec((1, rows, tn), lambda l, n: (l, 0, n)),
        out_shape=jax.ShapeDtypeStruct((depth, rows, d6), F32),
        compiler_params=_params(("arbitrary", "arbitrary")),
        name="modulation",
    )(cc, w_mod, b_mod.reshape(depth, 1, d6))


def _in_kernel(x_ref, mod_ref, w_ref, cos_ref, sin_ref, cs_ref,
               qx_ref, k_ref, v_ref, ux_ref, gg_ref, gc_ref, gs_ref, *, d):
    x = x_ref[0]
    m = mod_ref[0]
    h = _ln(x) * (1.0 + m[:, d:2 * d]) + m[:, 0:d]
    z = jnp.dot(h.astype(BF16), w_ref[0], preferred_element_type=F32)

    tb = x.shape[0]
    cos = cos_ref[...]
    sin = sin_ref[...]
    lane = lax.broadcasted_iota(I32, (tb, LANES), 1)
    first_half = (lane & 31) < 16
    low_map = lane < QK_DIM

    def rope(t):
        partner = jnp.where(first_half, pltpu.roll(t, LANES - 16, 1), pltpu.roll(t, 16, 1))
        return t * cos + partner * sin

    for p in range(N_HEADS):
        qp = rope(z[:, p * LANES:(p + 1) * LANES] * (QK_DIM ** -0.5 * LOG2E))
        qx_ref[0, :, (2 * p) * LANES:(2 * p + 1) * LANES] = jnp.where(low_map, qp, 0.0).astype(BF16)
        qx_ref[0, :, (2 * p + 1) * LANES:(2 * p + 2) * LANES] = jnp.where(low_map, 0.0, qp).astype(BF16)
        kp = rope(z[:, QK_W + p * LANES:QK_W + (p + 1) * LANES])
        k_ref[0, :, p * LANES:(p + 1) * LANES] = kp.astype(BF16)

    o = 2 * QK_W
    v_ref[0] = z[:, o:o + ATT_W].astype(BF16)
    o += ATT_W
    ux_ref[0] = z[:, o:o + LRU_W]
    o += LRU_W
    g = z[:, o:o + LRU_W]
    gg_ref[0] = 0.5 * g * (1.0 + jnp.tanh(GELU_C * (g + 0.044715 * (g * g * g))))
    o += LRU_W
    uf = z[:, o:o + FNET_W].astype(BF16)
    gcs = jnp.dot(uf, cs_ref[...], preferred_element_type=F32)
    gc_ref[...] = gcs[:, :FNET_W].astype(BF16)
    gs_ref[...] = gcs[:, FNET_W:].astype(BF16)


def _in_proj(xc, mod2, w_in_bf, l, cos_t, sin_t, cs, *, nbl, tb):
    b, tt, d = xc.shape
    nbt = tt // tb
    kern = functools.partial(_in_kernel, d=d)
    tok = lambda w: pl.BlockSpec((1, tb, w), lambda i, j: (i, j, 0))
    return pl.pallas_call(
        kern,
        grid=(b, nbt),
        in_specs=[
            tok(d),
            pl.BlockSpec((1, 1, 6 * d), lambda i, j: (2 * i + (j >= nbl).astype(I32), 0, 0)),
            pl.BlockSpec((1, d, IN_W), lambda i, j: (l, 0, 0)),
            pl.BlockSpec((tb, LANES), lambda i, j: (j, 0)),
            pl.BlockSpec((tb, LANES), lambda i, j: (j, 0)),
            pl.BlockSpec((FNET_W, 2 * FNET_W), lambda i, j: (0, 0)),
        ],
        out_specs=[
            tok(2 * QK_W), tok(QK_W), tok(ATT_W), tok(LRU_W), tok(LRU_W),
            pl.BlockSpec((tb, FNET_W), lambda i, j: (j, i)),
            pl.BlockSpec((tb, FNET_W), lambda i, j: (j, i)),
        ],
        out_shape=[
            jax.ShapeDtypeStruct((b, tt, 2 * QK_W), BF16),
            jax.ShapeDtypeStruct((b, tt, QK_W), BF16),
            jax.ShapeDtypeStruct((b, tt, ATT_W), BF16),
            jax.ShapeDtypeStruct((b, tt, LRU_W), F32),
            jax.ShapeDtypeStruct((b, tt, LRU_W), F32),
            jax.ShapeDtypeStruct((tt, b * FNET_W), BF16),
            jax.ShapeDtypeStruct((tt, b * FNET_W), BF16),
        ],
        compiler_params=_params(("arbitrary", "arbitrary")),
        name="in_proj",
    )(xc, mod2, w_in_bf, cos_t, sin_t, cs)


def _attn_kernel(*refs, n, lat_steps, qb, with_ctx):
    if with_ctx:
        ql_ref, qc_ref, k_ref, v_ref, lamv_ref, lin_ref, g_ref, ol_ref, oc_ref, vt_ref, s0_ref, s1_ref = refs
    else:
        ql_ref, k_ref, v_ref, lamv_ref, lin_ref, g_ref, ol_ref, vt_ref, s0_ref, s1_ref = refs
    j = pl.program_id(2)
    lv = lamv_ref[0]
    lam_init = lin_ref[0][:, 0:1]
    lam = (jnp.exp(jnp.sum(lv[0:1] * lv[1:2], axis=1, keepdims=True))
           - jnp.exp(jnp.sum(lv[2:3] * lv[3:4], axis=1, keepdims=True)) + lam_init)
    gain = g_ref[0] * (1.0 - lam_init)
    nt = (((1,), (1,)), ((), ()))
    tt = k_ref.shape[1]

    @pl.when(j == 0)
    def _():
        vt_ref[0:HEAD_V, :] = v_ref[0].astype(F32).T.astype(BF16)
        row = lax.broadcasted_iota(I32, (BF16_ROWS, tt), 0)
        vt_ref[HEAD_V:, :] = jnp.where(row == 0, 1.0, 0.0).astype(BF16)

    def attend(q_ref, o_ref, blocks, lo):
        chunks = [(c, min(KEY_CHUNK, tt - c)) for c in range(lo, tt, KEY_CHUNK)]
        tq = s0_ref.shape[1]
        streams = [(blk, mp) for blk in range(blocks) for mp in range(2)]
        s_refs = (s0_ref, s1_ref)

        def scores(i, c, w, m):
            blk, mp = streams[i]
            q = q_ref[0, blk * tq:(blk + 1) * tq, mp * LANES:(mp + 1) * LANES]
            st = lax.dot_general(k_ref[0, c:c + w, :], q, nt, preferred_element_type=F32)
            s_refs[i % 2][c:c + w, :] = st
            mc = jnp.max(st, axis=0, keepdims=True)
            return mc if m is None else jnp.maximum(m, mc)

        def weighted_values(i, c, w, m, acc):
            pt = jnp.exp2(s_refs[i % 2][c:c + w, :] - m).astype(BF16)
            part = jnp.dot(vt_ref[:, c:c + w], pt, preferred_element_type=F32)
            return part if acc is None else acc + part

        ms = [None] * len(streams)
        accs = [None] * len(streams)
        for stage in range(len(streams) + 1):
            for c, w in chunks:
                if stage >= 1:
                    accs[stage - 1] = weighted_values(stage - 1, c, w, ms[stage - 1], accs[stage - 1])
                if stage < len(streams):
                    ms[stage] = scores(stage, c, w, ms[stage])
            if stage >= 2 and stage % 2 == 0:
                blk = stage // 2 - 1
                a0, a1 = accs[stage - 2], accs[stage - 1]
                ot = (a0[0:HEAD_V] / a0[HEAD_V:HEAD_V + 1]) - lam * (a1[0:HEAD_V] / a1[HEAD_V:HEAD_V + 1])
                rt = ot * lax.rsqrt(jnp.mean(ot * ot, axis=0, keepdims=True) + RMS_EPS)
                o_ref[0, blk * tq:(blk + 1) * tq, :] = (rt.T * gain).astype(BF16)

    if not with_ctx:
        attend(ql_ref, ol_ref, qb, 0)
        return

    @pl.when(j < lat_steps)
    def _():
        attend(ql_ref, ol_ref, qb, 0)

    @pl.when(j >= lat_steps)
    def _():
        attend(qc_ref, oc_ref, qc_ref.shape[1] // s0_ref.shape[1], n)


def _attention(qx, k, v, lamv, lin, gain, l, *, n, tb, with_ctx):
    b, tt, _ = k.shape
    cn = tt - n
    qb = _largest_divisor(n // tb, (8, 4, 2, 1))
    lat_steps = n // (qb * tb)
    assert n % cn == 0 and cn % tb == 0
    kern = functools.partial(_attn_kernel, n=n, lat_steps=lat_steps, qb=qb, with_ctx=with_ctx)
    lat = lambda w: pl.BlockSpec((1, qb * tb, w), lambda i, h, j: (i, jnp.minimum(j, lat_steps - 1), h))
    ctx = lambda w: pl.BlockSpec((1, cn, w), lambda i, h, j: (i, n // cn, h))
    outs = pl.pallas_call(
        kern,
        grid=(b, N_HEADS, lat_steps + (1 if with_ctx else 0)),
        in_specs=[lat(2 * LANES)] + ([ctx(2 * LANES)] if with_ctx else []) + [
            pl.BlockSpec((1, tt, LANES), lambda i, h, j: (i, 0, h)),
            pl.BlockSpec((1, tt, LANES), lambda i, h, j: (i, 0, h)),
            pl.BlockSpec((1, 4, QK_DIM), lambda i, h, j: (l, 0, 0)),
            pl.BlockSpec((1, 1, LANES), lambda i, h, j: (l, 0, 0)),
            pl.BlockSpec((1, 1, HEAD_V), lambda i, h, j: (l * N_HEADS + h, 0, 0)),
        ],
        out_specs=[lat(LANES)] + ([pl.BlockSpec((1, cn, LANES), lambda i, h, j: (i, 0, h))] if with_ctx else []),
        out_shape=[jax.ShapeDtypeStruct((b, n, ATT_W), BF16)]
        + ([jax.ShapeDtypeStruct((b, cn, ATT_W), BF16)] if with_ctx else []),
        scratch_shapes=[pltpu.VMEM((HEAD_V + BF16_ROWS, tt), BF16),
                        pltpu.VMEM((tt, tb), F32), pltpu.VMEM((tt, tb), F32)],
        compiler_params=_params(("arbitrary", "arbitrary", "arbitrary")),
        name="diff_attention",
    )(*([qx, qx] if with_ctx else [qx]), k, v, lamv, lin, gain)
    return (outs[0], outs[1]) if with_ctx else (outs[0], None)


def _lru_kernel(ux_ref, gg_ref, cw_ref, cb_ref, wg_ref, bg_ref, lam_ref, y_ref,
                a_f, b_f, a_b, b_b, *, n, cn, r):
    tt = n + cn
    w = LANES
    cw = cw_ref[0]
    cb = cb_ref[0]
    bg = bg_ref[0]
    neg_lam = -lam_ref[0]
    softplus = jnp.maximum(neg_lam, 0.0) + jnp.log(1.0 + jnp.exp(-jnp.abs(neg_lam)))
    row8 = lax.broadcasted_iota(I32, (r, w), 0) & (SUBLANES - 1)
    ext_rows = r + 2 * SUBLANES

    def local_scan(a, bb, reverse):
        for s in (1, 2, 4):
            if reverse:
                a_sh = pltpu.roll(a, r - s, 0)
                b_sh = pltpu.roll(bb, r - s, 0)
                valid = row8 < SUBLANES - s
            else:
                a_sh = pltpu.roll(a, s, 0)
                b_sh = pltpu.roll(bb, s, 0)
                valid = row8 >= s
            bb = jnp.where(valid, a * b_sh + bb, bb)
            a = jnp.where(valid, a * a_sh, a)
        return a, bb

    def gates_chunk(c, carry):
        r0 = pl.multiple_of(c * r, r)
        seg_start = jnp.logical_or(r0 == 0, r0 == n)
        seg_end = jnp.logical_or(r0 + r == n, r0 + r == tt)
        main = ux_ref[0, pl.ds(r0, r), :]
        prev = ux_ref[0, pl.ds(pl.multiple_of(jnp.maximum(r0 - SUBLANES, 0), SUBLANES), SUBLANES), :]
        nxt = ux_ref[0, pl.ds(pl.multiple_of(jnp.minimum(r0 + r, tt - SUBLANES), SUBLANES), SUBLANES), :]
        prev = jnp.where(seg_start, 0.0, prev)
        nxt = jnp.where(seg_end, 0.0, nxt)
        ext = jnp.concatenate([prev, main, nxt], axis=0)
        u = cb
        for t in range(4):
            sh = (2 - t) % ext_rows
            win = ext if sh == 0 else pltpu.roll(ext, sh, 0)
            u = u + cw[t:t + 1, :] * win[SUBLANES:SUBLANES + r, :]
        zz = jnp.dot(u.astype(BF16), wg_ref[0], preferred_element_type=F32) + bg
        for dr, (a_s, b_s) in enumerate(((a_f, b_f), (a_b, b_b))):
            rg = jax.nn.sigmoid(zz[:, (2 * dr) * w:(2 * dr + 1) * w])
            ig = jax.nn.sigmoid(zz[:, (2 * dr + 1) * w:(2 * dr + 2) * w])
            a = jnp.exp(-LRU_C * rg * softplus[dr:dr + 1, :])
            bb = jnp.sqrt(1.0 - a * a) * ig * u
            a, bb = local_scan(a, bb, reverse=(dr == 1))
            a_s[pl.ds(r0, r), :] = a
            b_s[pl.ds(r0, r), :] = bb
        return carry

    lax.fori_loop(0, tt // r, gates_chunk, 0)

    def seg_scan(first_tile, ntiles, cf, cbk):
        def body(i, carry):
            cf, cbk = carry
            rf = pl.multiple_of((first_tile + i) * SUBLANES, SUBLANES)
            hf = b_f[pl.ds(rf, SUBLANES), :] + a_f[pl.ds(rf, SUBLANES), :] * cf
            b_f[pl.ds(rf, SUBLANES), :] = hf
            rb = pl.multiple_of((first_tile + ntiles - 1 - i) * SUBLANES, SUBLANES)
            hb = b_b[pl.ds(rb, SUBLANES), :] + a_b[pl.ds(rb, SUBLANES), :] * cbk
            b_b[pl.ds(rb, SUBLANES), :] = hb
            return hf[SUBLANES - 1:SUBLANES, :], hb[0:1, :]
        return lax.fori_loop(0, ntiles, body, (cf, cbk))

    zero = jnp.zeros((1, w), F32)
    cf, cbk = seg_scan(n // SUBLANES, cn // SUBLANES, zero, zero)
    seg_scan(0, n // SUBLANES, cf, cbk)

    def out_chunk(c, carry):
        r0 = pl.multiple_of(c * r, r)
        y = (b_f[pl.ds(r0, r), :] + b_b[pl.ds(r0, r), :]) * gg_ref[0, pl.ds(r0, r), :]
        y_ref[0, pl.ds(r0, r), :] = y.astype(BF16)
        return carry

    lax.fori_loop(0, tt // r, out_chunk, 0)


def _rglru(ux, gg, conv_w2, conv_b2, wgate, bgate, lam2, l, *, n, cn, tb):
    b, tt, _ = ux.shape
    kern = functools.partial(_lru_kernel, n=n, cn=cn, r=tb)
    half = lambda: pl.BlockSpec((1, tt, LANES), lambda i, hh: (i, 0, hh))
    return pl.pallas_call(
        kern,
        grid=(b, 2),
        in_specs=[
            half(), half(),
            pl.BlockSpec((1, 4, LANES), lambda i, hh: (2 * l + hh, 0, 0)),
            pl.BlockSpec((1, 1, LANES), lambda i, hh: (2 * l + hh, 0, 0)),
            pl.BlockSpec((1, LANES, 4 * LANES), lambda i, hh: (2 * l + hh, 0, 0)),
            pl.BlockSpec((1, 1, 4 * LANES), lambda i, hh: (2 * l + hh, 0, 0)),
            pl.BlockSpec((1, 2, LANES), lambda i, hh: (2 * l + hh, 0, 0)),
        ],
        out_specs=half(),
        out_shape=jax.ShapeDtypeStruct((b, tt, LRU_W), BF16),
        scratch_shapes=[pltpu.VMEM((tt, LANES), F32)] * 4,
        compiler_params=_params(("arbitrary", "arbitrary")),
        name="rglru",
    )(ux, gg, conv_w2, conv_b2, wgate, bgate, lam2)


def _dft_kernel(c_ref, s_ref, gc_ref, gs_ref, o_ref, acc_ref):
    kk = pl.program_id(2)

    @pl.when(kk == 0)
    def _():
        acc_ref[...] = jnp.zeros_like(acc_ref)

    acc_ref[...] += (jnp.dot(c_ref[...], gc_ref[...], preferred_element_type=F32)
                     + jnp.dot(s_ref[...], gs_ref[...], preferred_element_type=F32))

    @pl.when(kk == pl.num_programs(2) - 1)
    def _():
        o_ref[...] = acc_ref[...].astype(BF16)


def _position_dft(cmat, smat, gc, gs):
    tt = cmat.shape[0]
    nn = gc.shape[1]
    tm = _largest_divisor(tt, (1088, 640, 512, 256, 128))
    tk = _largest_divisor(tt, (2176, 640, 512, 256, 128))
    tn = _largest_divisor(nn, (512, 256))
    return pl.pallas_call(
        _dft_kernel,
        grid=(tt // tm, nn // tn, tt // tk),
        in_specs=[
            pl.BlockSpec((tm, tk), lambda i, j, k: (i, k)),
            pl.BlockSpec((tm, tk), lambda i, j, k: (i, k)),
            pl.BlockSpec((tk, tn), lambda i, j, k: (k, j)),
            pl.BlockSpec((tk, tn), lambda i, j, k: (k, j)),
        ],
        out_specs=pl.BlockSpec((tm, tn), lambda i, j, k: (i, j)),
        out_shape=jax.ShapeDtypeStruct((tt, nn), BF16),
        scratch_shapes=[pltpu.VMEM((tm, tn), F32)],
        compiler_params=_params(("arbitrary", "arbitrary", "arbitrary")),
        name="position_dft",
    )(cmat, smat, gc, gs)


def _out_kernel(attl_ref, attc_ref, y_ref, f_ref, x_ref, mod_ref, w_ref, g_ref, b_ref, wr_ref,
                x1_ref, h2_ref, lg_ref, mix0_ref, mix1_ref, *, d, alpha, nbl, nblocks):
    j = pl.program_id(1)

    @pl.when(j == 0)
    def _():
        mix1_ref[...] = jnp.zeros_like(mix1_ref)

    def step(new_ref, old_ref):
        m = mod_ref[0]
        tb = x_ref.shape[1]
        att = jnp.where(jnp.minimum(j, nblocks - 1) >= nbl, attc_ref[0], attl_ref[0])
        for c in range(OUT_PIECES):
            rows = slice(c * tb // OUT_PIECES, (c + 1) * tb // OUT_PIECES)
            cols = slice(c * d // OUT_PIECES, (c + 1) * d // OUT_PIECES)
            x1 = _ln(alpha * x_ref[0, rows, :] + m[:, 2 * d:3 * d] * old_ref[rows, :]) * g_ref[0] + b_ref[0]
            x1_ref[0, rows, :] = x1
            h2_ref[0, rows, :] = (_ln(x1) * (1.0 + m[:, 4 * d:5 * d]) + m[:, 3 * d:4 * d]).astype(BF16)
            new_ref[:, cols] = (
                jnp.dot(att, w_ref[0, 0:ATT_W, cols], preferred_element_type=F32)
                + jnp.dot(y_ref[0], w_ref[0, ATT_W:ATT_W + LRU_W, cols], preferred_element_type=F32)
                + jnp.dot(f_ref[...], w_ref[0, ATT_W + LRU_W:, cols], preferred_element_type=F32))
        lg_ref[0] = lax.dot_general(wr_ref[0], h2_ref[0], (((1,), (1,)), ((), ())), preferred_element_type=F32)

    @pl.when(j % 2 == 0)
    def _():
        step(mix0_ref, mix1_ref)

    @pl.when(j % 2 == 1)
    def _():
        step(mix1_ref, mix0_ref)


def _out_proj(att_l, att_c, y, fy, xc, mod2, w_out_bf, ln_g, ln_b, wr_t, l, *, nbl, tb, alpha, out_tokens):
    b, _, d = xc.shape
    nblocks = out_tokens // tb
    if att_c is None:
        assert nblocks <= nbl
        att_c = att_l
    kern = functools.partial(_out_kernel, d=d, alpha=alpha, nbl=nbl, nblocks=nblocks)
    cur = lambda j: jnp.minimum(j, nblocks - 1)
    prv = lambda j: jnp.maximum(j - 1, 0)
    tok = lambda w: pl.BlockSpec((1, tb, w), lambda i, j: (i, prv(j), 0))
    return pl.pallas_call(
        kern,
        grid=(b, nblocks + 1),
        in_specs=[
            pl.BlockSpec((1, tb, ATT_W), lambda i, j: (i, jnp.minimum(cur(j), nbl - 1), 0)),
            pl.BlockSpec((1, tb, ATT_W), lambda i, j: (i, jnp.maximum(cur(j) - nbl, 0), 0)),
            pl.BlockSpec((1, tb, LRU_W), lambda i, j: (i, cur(j), 0)),
            pl.BlockSpec((tb, FNET_W), lambda i, j: (cur(j), i)),
            tok(d),
            pl.BlockSpec((1, 1, 6 * d), lambda i, j: (2 * i + (prv(j) >= nbl).astype(I32), 0, 0)),
            pl.BlockSpec((1, d, d), lambda i, j: (l, 0, 0)),
            pl.BlockSpec((1, 1, d), lambda i, j: (l, 0, 0)),
            pl.BlockSpec((1, 1, d), lambda i, j: (l, 0, 0)),
            pl.BlockSpec((1, N_EXPERTS, d), lambda i, j: (l, 0, 0)),
        ],
        out_specs=[tok(d), tok(d), pl.BlockSpec((1, N_EXPERTS, tb), lambda i, j: (i, 0, prv(j)))],
        out_shape=[
            jax.ShapeDtypeStruct((b, out_tokens, d), F32),
            jax.ShapeDtypeStruct((b, out_tokens, d), BF16),
            jax.ShapeDtypeStruct((b, N_EXPERTS, out_tokens), F32),
        ],
        scratch_shapes=[pltpu.VMEM((tb, d), F32), pltpu.VMEM((tb, d), F32)],
        compiler_params=_params(("arbitrary", "arbitrary")),
        name="out_proj",
    )(att_l, att_c, y, fy, xc, mod2, w_out_bf, ln_g, ln_b, wr_t)


def _route_kernel(lg_ref, pos_ref, gate_ref, starts_ref, *, segments, n_slots):
    lg = lg_ref[0]
    lane_id = lax.broadcasted_iota(I32, (N_EXPERTS, LANES), 1)
    starts = jnp.zeros((N_EXPERTS, LANES), I32)
    chunks_per_tile = TOKEN_TILE // LANES
    e = jnp.exp(lg - jnp.max(lg, axis=0, keepdims=True))
    s = e / jnp.sum(e, axis=0, keepdims=True)
    ri = lax.broadcasted_iota(I32, (LANES, LANES), 0)
    ci = lax.broadcasted_iota(I32, (LANES, LANES), 1)
    strict_upper = jnp.where(ri < ci, 1.0, 0.0).astype(BF16)

    for lo, t, cap, base in segments:
        ss = s[:, lo:lo + t]
        bits = pltpu.bitcast(ss, I32)
        capf = float(cap)

        def search(i, thr, bits=bits, capf=capf):
            cand = thr | jnp.left_shift(jnp.int32(1), 30 - i)
            cnt = jnp.sum(jnp.where(bits >= cand, 1.0, 0.0), axis=1, keepdims=True)
            return jnp.where(cnt >= capf, cand, thr)

        thr = lax.fori_loop(0, 31, search, jnp.zeros((N_EXPERTS, 1), I32))
        need = capf - jnp.sum(jnp.where(bits > thr, 1.0, 0.0), axis=1, keepdims=True)
        off_eq = jnp.zeros((N_EXPERTS, 1), F32)
        off_sel = jnp.zeros((N_EXPERTS, 1), F32)
        for c in range(t // LANES):
            sl = slice(c * LANES, (c + 1) * LANES)
            tile, sub = divmod(lo // LANES + c, chunks_per_tile)
            if sub == 0:
                starts = jnp.where(lane_id == tile, off_sel.astype(I32) + base, starts)
            bits_c = bits[:, sl]
            eq = bits_c == thr
            eq_c = jnp.where(eq, 1.0, 0.0)
            rank_eq = jnp.dot(eq_c.astype(BF16), strict_upper, preferred_element_type=F32) + off_eq
            off_eq = off_eq + jnp.sum(eq_c, axis=1, keepdims=True)
            sel = jnp.logical_or(bits_c > thr, jnp.logical_and(eq, rank_eq < need))
            sel_c = jnp.where(sel, 1.0, 0.0)
            slot = jnp.dot(sel_c.astype(BF16), strict_upper, preferred_element_type=F32) + off_sel
            off_sel = off_sel + jnp.sum(sel_c, axis=1, keepdims=True)
            osl = slice(sub * LANES, (sub + 1) * LANES)
            pos_ref[0, tile, :, osl] = jnp.where(sel, slot.astype(I32) + base, -1)
            gate_ref[0, tile, :, osl] = jnp.where(sel, ss[:, sl], 0.0)

    n_tiles = lg.shape[1] // TOKEN_TILE
    starts_ref[0] = jnp.where(lane_id == n_tiles, n_slots, starts)


def _route(logits_t, *, segments, n_slots):
    b, ne, tt = logits_t.shape
    nt = tt // TOKEN_TILE
    assert nt < LANES and all(lo % TOKEN_TILE == 0 and t % TOKEN_TILE == 0 for lo, t, _, _ in segments)
    kern = functools.partial(_route_kernel, segments=segments, n_slots=n_slots)
    tiled = lambda: pl.BlockSpec((1, nt, ne, TOKEN_TILE), lambda i: (i, 0, 0, 0))
    return pl.pallas_call(
        kern,
        grid=(b,),
        in_specs=[pl.BlockSpec((1, ne, tt), lambda i: (i, 0, 0))],
        out_specs=[tiled(), tiled(), pl.BlockSpec((1, ne, LANES), lambda i: (i, 0, 0))],
        out_shape=[jax.ShapeDtypeStruct((b, nt, ne, TOKEN_TILE), I32),
                   jax.ShapeDtypeStruct((b, nt, ne, TOKEN_TILE), F32),
                   jax.ShapeDtypeStruct((b, ne, LANES), I32)],
        compiler_params=_params(("arbitrary",)),
        name="route",
    )(logits_t)


def _window_start(starts_ref, base_idx, tile, w, win, limit):
    c0 = starts_ref[base_idx + tile]
    lo = ((c0 >> 4) << 4) + w * win
    return lo, pl.multiple_of(jnp.minimum(lo, limit), BF16_ROWS)


def _window_count(starts_ref, base_idx, tiles, win):
    nw = jnp.int32(1)
    for tile in tiles:
        c0 = starts_ref[base_idx + tile]
        c1 = starts_ref[base_idx + tile + 1]
        nw = jnp.maximum(nw, (c1 - ((c0 >> 4) << 4) + win - 1) // win)
    return nw


def _gather_kernel(starts_ref, h_ref, pos_ref, xs_ref, acc_ref, *, slots, win, group):
    b = pl.program_id(0)
    eg = pl.program_id(1)
    nt = pos_ref.shape[1]
    acc_ref[...] = jnp.zeros_like(acc_ref)
    rel = lax.broadcasted_iota(I32, (win, TOKEN_TILE), 0)
    bases = [(b * N_EXPERTS + eg * group + i) * LANES for i in range(group)]
    nw = jnp.int32(1)
    for i in range(group):
        nw = jnp.maximum(nw, _window_count(starts_ref, bases[i], range(nt), win))

    def window_pass(w, carry):
        for tile in range(nt):
            onehots, offs = [], []
            for i in range(group):
                lo, start = _window_start(starts_ref, bases[i], tile, w, win, slots)
                prow = pos_ref[0, tile, pl.ds(eg * group + i, 1), :] - start
                onehots.append(jnp.where(rel == prow, 1.0, 0.0).astype(BF16))
                offs.append(start)
            res = jnp.dot(jnp.concatenate(onehots, axis=0), h_ref[0, tile * TOKEN_TILE:(tile + 1) * TOKEN_TILE, :],
                          preferred_element_type=F32)
            for i in range(group):
                acc_ref[i, pl.ds(offs[i], win), :] += res[i * win:(i + 1) * win].astype(BF16)
        return carry

    lax.fori_loop(0, nw, window_pass, 0)
    xs_ref[...] = acc_ref[:, 0:slots, :]


def _gather(starts, h2, pos, *, slots, win, group=8):
    b, tt, d = h2.shape
    nt = tt // TOKEN_TILE
    kern = functools.partial(_gather_kernel, slots=slots, win=win, group=group)
    return pl.pallas_call(
        kern,
        grid_spec=pltpu.PrefetchScalarGridSpec(
            num_scalar_prefetch=1,
            grid=(b, N_EXPERTS // group),
            in_specs=[
                pl.BlockSpec((1, tt, d), lambda i, g, s: (i, 0, 0)),
                pl.BlockSpec((1, nt, N_EXPERTS, TOKEN_TILE), lambda i, g, s: (i, 0, 0, 0)),
            ],
            out_specs=pl.BlockSpec((group, slots, d), lambda i, g, s: (g, i, 0)),
            scratch_shapes=[pltpu.VMEM((group, slots + win, d), BF16)],
        ),
        out_shape=jax.ShapeDtypeStruct((N_EXPERTS, b * slots, d), BF16),
        compiler_params=_params(("arbitrary", "arbitrary")),
        name="moe_gather",
    )(starts, h2, pos)


def _ffn_kernel(xs_ref, wg_ref, wu_ref, wd_ref, ys_ref, *, fchunk):
    xs = xs_ref[0]
    f = wg_ref.shape[-1]
    acc = None
    for c in range(f // fchunk):
        sl = slice(c * fchunk, (c + 1) * fchunk)
        a = jnp.dot(xs, wg_ref[0, 0, :, sl], preferred_element_type=F32)
        u = jnp.dot(xs, wu_ref[0, 0, :, sl], preferred_element_type=F32)
        hm = (a * jax.nn.sigmoid(a) * u).astype(BF16)
        y = jnp.dot(hm, wd_ref[0, 0, sl, :], preferred_element_type=F32)
        acc = y if acc is None else acc + y
    ys_ref[0] = acc.astype(BF16)


def _expert_ffn(xs, wg, wu, wd, l, *, slots):
    ne, rows, d = xs.shape
    f = wg.shape[-1]
    nb = rows // slots
    kern = functools.partial(_ffn_kernel, fchunk=_largest_divisor(f, (512,)))
    return pl.pallas_call(
        kern,
        grid=(ne, nb),
        in_specs=[
            pl.BlockSpec((1, slots, d), lambda e, i: (e, i, 0)),
            pl.BlockSpec((1, 1, d, f), lambda e, i: (l, e, 0, 0)),
            pl.BlockSpec((1, 1, d, f), lambda e, i: (l, e, 0, 0)),
            pl.BlockSpec((1, 1, f, d), lambda e, i: (l, e, 0, 0)),
        ],
        out_specs=pl.BlockSpec((1, slots, d), lambda e, i: (e, i, 0)),
        out_shape=jax.ShapeDtypeStruct((ne, rows, d), BF16),
        compiler_params=_params(("arbitrary", "arbitrary")),
        name="expert_ffn",
    )(xs, wg, wu, wd)


def _combine_kernel(starts_ref, ys_ref, pos_ref, gate_ref, x1_ref, mod_ref, g_ref, b_ref, o_ref,
                    *, slots, win, d, alpha):
    b = pl.program_id(0)
    tile = pl.program_id(1)
    rel = lax.broadcasted_iota(I32, (win, TOKEN_TILE), 0)
    bases = [(b * N_EXPERTS + e) * LANES for e in range(N_EXPERTS)]
    nw = jnp.int32(1)
    for e in range(N_EXPERTS):
        c0 = starts_ref[bases[e] + tile]
        c1 = starts_ref[bases[e] + tile + 1]
        nw = jnp.maximum(nw, (c1 - ((c0 >> 4) << 4) + win - 1) // win)

    def window_pass(w, moe):
        gated, rows = [], []
        for e in range(N_EXPERTS):
            lo, start = _window_start(starts_ref, bases[e], tile, w, win, slots - win)
            prow = pos_ref[0, 0, e:e + 1, :]
            prow = jnp.where(jnp.logical_and(prow >= lo, prow < lo + win), prow - start, -1)
            gated.append(jnp.where(rel == prow, gate_ref[0, 0, e:e + 1, :], 0.0).astype(BF16))
            rows.append(ys_ref[e, pl.ds(start, win), :])
        return moe + lax.dot_general(jnp.concatenate(gated, axis=0), jnp.concatenate(rows, axis=0),
                                     (((0,), (0,)), ((), ())), preferred_element_type=F32)

    moe = lax.fori_loop(0, nw, window_pass, jnp.zeros((TOKEN_TILE, d), F32))
    m = mod_ref[0]
    o_ref[0] = _ln(alpha * x1_ref[0] + m[:, 5 * d:6 * d] * moe) * g_ref[0] + b_ref[0]


def _combine(starts, ys, pos, gate, x1, mod2, ln_g, ln_b, l, *, slots, win, n_lat_tiles, alpha, out_tokens):
    b, _, d = x1.shape
    kern = functools.partial(_combine_kernel, slots=slots, win=win, d=d, alpha=alpha)
    tiled = lambda: pl.BlockSpec((1, 1, N_EXPERTS, TOKEN_TILE), lambda i, t, s: (i, t, 0, 0))
    tok = lambda: pl.BlockSpec((1, TOKEN_TILE, d), lambda i, t, s: (i, t, 0))
    return pl.pallas_call(
        kern,
        grid_spec=pltpu.PrefetchScalarGridSpec(
            num_scalar_prefetch=1,
            grid=(b, out_tokens // TOKEN_TILE),
            in_specs=[
                pl.BlockSpec((N_EXPERTS, slots, d), lambda i, t, s: (0, i, 0)),
                tiled(), tiled(), tok(),
                pl.BlockSpec((1, 1, 6 * d), lambda i, t, s: (2 * i + (t >= n_lat_tiles).astype(I32), 0, 0)),
                pl.BlockSpec((1, 1, d), lambda i, t, s: (l, 0, 0)),
                pl.BlockSpec((1, 1, d), lambda i, t, s: (l, 0, 0)),
            ],
            out_specs=tok(),
        ),
        out_shape=jax.ShapeDtypeStruct((b, out_tokens, d), F32),
        compiler_params=_params(("arbitrary", "arbitrary")),
        name="moe_combine",
    )(starts, ys, pos, gate, x1, mod2, ln_g, ln_b)


def _rope_tables(n, cn):
    lane = np.arange(LANES)
    within = lane % QK_DIM
    use_col = (within // 32) == 1
    first_half = (within % 32) < 16
    inv = ROPE_BASE ** (-(within % 16).astype(np.float64) / 16.0)
    pos = np.arange(n)
    coord = np.where(use_col[None, :], (pos % GRID_W)[:, None], (pos // GRID_W)[:, None]).astype(np.float32)
    ang = jnp.asarray(coord) * jnp.asarray(inv.astype(np.float32))[None, :]
    cos = jnp.cos(ang)
    sin = jnp.where(jnp.asarray(first_half)[None, :], -jnp.sin(ang), jnp.sin(ang))
    cos = jnp.concatenate([cos, jnp.ones((cn, LANES), F32)], axis=0)
    sin = jnp.concatenate([sin, jnp.zeros((cn, LANES), F32)], axis=0)
    return cos, sin


def _channel_dft():
    idx = np.arange(FNET_W)
    same = (idx[:, None] // FNET_GROUP_W) == (idx[None, :] // FNET_GROUP_W)
    ang = 2.0 * np.pi * ((idx[:, None] % FNET_GROUP_W) * (idx[None, :] % FNET_GROUP_W) % FNET_GROUP_W) / FNET_GROUP_W
    cs = np.concatenate([np.where(same, np.cos(ang), 0.0), np.where(same, np.sin(ang), 0.0)], axis=1)
    return jnp.asarray(cs.astype(np.float32)).astype(BF16)


def _position_dft_mats(n, cn):
    def segment(t):
        g = math.gcd(t, 64)
        kk = jnp.arange(t, dtype=I32)

        def table(m):
            ph = ((kk[:, None] * m[None, :]) % t).astype(F32) * (2.0 * math.pi / t)
            return jnp.cos(ph), jnp.sin(ph)

        ch, sh = table(jnp.arange(t // g, dtype=I32) * g)
        cl, sl = table(jnp.arange(g, dtype=I32))
        scale = 1.0 / math.sqrt(t * FNET_GROUP_W)
        cmat = (ch[:, :, None] * cl[:, None, :] - sh[:, :, None] * sl[:, None, :]).reshape(t, t) * scale
        smat = (sh[:, :, None] * cl[:, None, :] + ch[:, :, None] * sl[:, None, :]).reshape(t, t) * (-scale)
        return cmat.astype(BF16), smat.astype(BF16)

    def block_diag(a, c):
        top = jnp.concatenate([a, jnp.zeros((n, cn), BF16)], axis=1)
        bot = jnp.concatenate([jnp.zeros((cn, n), BF16), c], axis=1)
        return jnp.concatenate([top, bot], axis=0)

    (cl_, sl_), (cc_, sc_) = segment(n), segment(cn)
    return block_diag(cl_, cc_), block_diag(sl_, sc_)


def _block_diag_gates(wa, wx):
    depth = wa.shape[0]

    def dense(wb):
        eye = jnp.eye(LRU_BLOCKS, dtype=wb.dtype)
        return jnp.einsum('lncd,nm->lncmd', wb, eye).reshape(depth, LRU_W, LRU_W)

    halves = []
    for hh in range(2):
        sl = slice(hh * LANES, (hh + 1) * LANES)
        cols = [dense(wmat[:, dr])[:, sl, sl] for dr in range(2) for wmat in (wa, wx)]
        halves.append(jnp.concatenate(cols, axis=-1))
    return jnp.stack(halves, axis=1).reshape(depth * 2, LANES, 4 * LANES).astype(BF16)


def _gate_bias(ba, bx):
    depth = ba.shape[0]
    halves = []
    for hh in range(2):
        sl = slice(hh * LANES, (hh + 1) * LANES)
        halves.append(jnp.concatenate([bvec[:, dr, sl] for dr in range(2) for bvec in (ba, bx)], axis=-1))
    return jnp.stack(halves, axis=1).reshape(depth * 2, 1, 4 * LANES)


def _split_halves(a):
    depth, r, _ = a.shape
    return a.reshape(depth, r, 2, LANES).transpose(0, 2, 1, 3).reshape(depth * 2, r, LANES)


def kernel(x, c, ctx, c_ctx, w_mod, b_mod, w_in, lam_q1, lam_k1, lam_q2, lam_k2, attn_norm_g, conv_w, conv_b, lru_wa, lru_ba, lru_wx, lru_bx, lru_lam, w_out, ln1_g, ln1_b, w_router, w_gate, w_up, w_down, ln2_g, ln2_b):
    b, n, d = x.shape
    cn = ctx.shape[1]
    depth = w_mod.shape[0]
    tt = n + cn
    tb = _largest_divisor(math.gcd(n, cn), (256, 128))
    assert n % tb == 0 and cn % tb == 0 and n % GRID_W == 0 and tt % LANES == 0
    nbl = n // tb
    cap_l = CAPACITY_FACTOR * n // N_EXPERTS
    cap_c = CAPACITY_FACTOR * cn // N_EXPERTS
    assert cap_l % BF16_ROWS == 0 and cap_c % BF16_ROWS == 0 and n % TOKEN_TILE == 0 and cn % TOKEN_TILE == 0
    alpha = (2 * depth) ** 0.25

    rows = -(-(b + 1) // SUBLANES) * SUBLANES
    cc = jnp.concatenate([c, c_ctx[None, :], jnp.zeros((rows - b - 1, d), F32)], axis=0)
    mod = _modulation(cc, w_mod, b_mod)

    cos_t, sin_t = _rope_tables(n, cn)
    cs = _channel_dft()
    cmat, smat = _position_dft_mats(n, cn)

    w_in_bf = w_in.astype(BF16)
    w_out_bf = w_out.astype(BF16)
    wg_bf = w_gate.astype(BF16)
    wu_bf = w_up.astype(BF16)
    wd_bf = w_down.astype(BF16)
    wr_t = jnp.swapaxes(w_router, 1, 2).astype(BF16)
    lamv = jnp.stack([lam_q1, lam_k1, lam_q2, lam_k2], axis=1).astype(F32)
    lam_init = np.array([0.8 - 0.6 * math.exp(-0.3 * l) for l in range(depth)], np.float32)
    lin = jnp.asarray(np.broadcast_to(lam_init[:, None, None], (depth, 1, LANES)).copy())
    gain = attn_norm_g.reshape(depth * N_HEADS, 1, HEAD_V)
    conv_w2 = _split_halves(conv_w)
    conv_b2 = _split_halves(conv_b[:, None, :])
    lam2 = _split_halves(lru_lam)
    wgate = _block_diag_gates(lru_wa, lru_wx)
    bgate = _gate_bias(lru_ba, lru_bx)
    ln1g, ln1b = ln1_g[:, None, :], ln1_b[:, None, :]
    ln2g, ln2b = ln2_g[:, None, :], ln2_b[:, None, :]

    xc = jnp.concatenate([x, ctx], axis=1)
    for l in range(depth):
        ml = mod[l]
        mod2 = jnp.stack([ml[:b], jnp.broadcast_to(ml[b][None, :], (b, 6 * d))], axis=1).reshape(2 * b, 1, 6 * d)
        qx, k, v, ux, gg, gc, gs = _in_proj(xc, mod2, w_in_bf, l, cos_t, sin_t, cs, nbl=nbl, tb=tb)
        keep_ctx = l < depth - 1
        tokens = tt if keep_ctx else n
        att_l, att_c = _attention(qx, k, v, lamv, lin, gain, l, n=n, tb=tb, with_ctx=keep_ctx)
        y = _rglru(ux, gg, conv_w2, conv_b2, wgate, bgate, lam2, l, n=n, cn=cn, tb=tb)
        fy = _position_dft(cmat, smat, gc, gs)
        x1, h2, logits_t = _out_proj(att_l, att_c, y, fy, xc, mod2, w_out_bf, ln1g, ln1b, wr_t, l,
                                     nbl=nbl, tb=tb, alpha=alpha, out_tokens=tokens)
        segments = ((0, n, cap_l, 0), (n, cn, cap_c, cap_l)) if keep_ctx else ((0, n, cap_l, 0),)
        slots = cap_l + cap_c if keep_ctx else cap_l
        win = min(SLOT_WINDOW, slots)
        pos, gate, starts = _route(logits_t, segments=segments, n_slots=slots)
        starts = starts.reshape(-1)
        xs = _gather(starts, h2, pos, slots=slots, win=win)
        ys = _expert_ffn(xs, wg_bf, wu_bf, wd_bf, l, slots=slots)
        xc = _combine(starts, ys, pos, gate, x1, mod2, ln2g, ln2b, l, slots=slots, win=win,
                      n_lat_tiles=n // TOKEN_TILE, alpha=alpha, out_tokens=tokens)
    return xc
```

```python
import functools
import math

import numpy as np
import jax
import jax.numpy as jnp
from jax import lax
from jax.experimental import pallas as pl
from jax.experimental.pallas import tpu as pltpu

F32 = jnp.float32
BF16 = jnp.bfloat16
I32 = jnp.int32

GRID_W = 64
QK_DIM = 64
N_HEADS = 4
HEAD_V = 128
QK_W = 512
ATT_W = 512
LRU_W = 256
LRU_BLOCKS = 4
LRU_BLOCK_W = 64
LRU_C = 8.0
FNET_W = 256
FNET_GROUP_W = 64
IN_W = 2304
ROPE_BASE = 10000.0
N_EXPERTS = 16
CAPACITY_FACTOR = 2
LN_EPS = 1e-5
RMS_EPS = 1e-6
GELU_C = math.sqrt(2.0 / math.pi)
LOG2E = math.log2(math.e)

LANES = 128
SUBLANES = 8
BF16_ROWS = 16
KEY_CHUNK = 256
OUT_PIECES = 4
TOKEN_TILE = 256
SLOT_WINDOW = 64
VMEM_LIMIT = 56 << 20


def _params(sem, vmem=VMEM_LIMIT):
    return pltpu.CompilerParams(dimension_semantics=sem, vmem_limit_bytes=vmem)


def _ln(x):
    mu = jnp.mean(x, axis=-1, keepdims=True)
    xc = x - mu
    var = jnp.mean(xc * xc, axis=-1, keepdims=True)
    return xc * lax.rsqrt(var + LN_EPS)


def _largest_divisor(n, candidates):
    for c in candidates:
        if c <= n and n % c == 0:
            return c
    return n


def _mod_kernel(c_ref, w_ref, b_ref, o_ref):
    c = c_ref[...]
    s = c * jax.nn.sigmoid(c)
    o_ref[0] = jnp.dot(s, w_ref[0], precision=lax.Precision.HIGHEST, preferred_element_type=F32) + b_ref[0]


def _modulation(cc, w_mod, b_mod):
    depth, d, d6 = w_mod.shape
    rows = cc.shape[0]
    tn = 1024
    return pl.pallas_call(
        _mod_kernel,
        grid=(depth, d6 // tn),
        in_specs=[
            pl.BlockSpec((rows, d), lambda l, n: (0, 0)),
            pl.BlockSpec((1, d, tn), lambda l, n: (l, 0, n)),
            pl.BlockSpec((1, 1, tn), lambda l, n: (l, 0, n)),
        ],
        out_specs=pl.BlockSpec((1, rows, tn), lambda l, n: (l, 0, n)),
        out_shape=jax.ShapeDtypeStruct((depth, rows, d6), F32),
        compiler_params=_params(("arbitrary", "arbitrary")),
        name="modulation",
    )(cc, w_mod, b_mod.reshape(depth, 1, d6))


def _in_kernel(x_ref, mod_ref, w_ref, cos_ref, sin_ref, cs_ref,
               qx_ref, k_ref, v_ref, ux_ref, gg_ref, gc_ref, gs_ref, *, d):
    x = x_ref[0]
    m = mod_ref[0]
    h = _ln(x) * (1.0 + m[:, d:2 * d]) + m[:, 0:d]
    z = jnp.dot(h.astype(BF16), w_ref[0], preferred_element_type=F32)

    tb = x.shape[0]
    cos = cos_ref[...]
    sin = sin_ref[...]
    lane = lax.broadcasted_iota(I32, (tb, LANES), 1)
    first_half = (lane & 31) < 16
    low_map = lane < QK_DIM

    def rope(t):
        partner = jnp.where(first_half, pltpu.roll(t, LANES - 16, 1), pltpu.roll(t, 16, 1))
        return t * cos + partner * sin

    for p in range(N_HEADS):
        qp = rope(z[:, p * LANES:(p + 1) * LANES] * (QK_DIM ** -0.5 * LOG2E))
        qx_ref[0, :, (2 * p) * LANES:(2 * p + 1) * LANES] = jnp.where(low_map, qp, 0.0).astype(BF16)
        qx_ref[0, :, (2 * p + 1) * LANES:(2 * p + 2) * LANES] = jnp.where(low_map, 0.0, qp).astype(BF16)
        kp = rope(z[:, QK_W + p * LANES:QK_W + (p + 1) * LANES])
        k_ref[0, :, p * LANES:(p + 1) * LANES] = kp.astype(BF16)

    o = 2 * QK_W
    v_ref[0] = z[:, o:o + ATT_W].astype(BF16)
    o += ATT_W
    ux_ref[0] = z[:, o:o + LRU_W]
    o += LRU_W
    g = z[:, o:o + LRU_W]
    gg_ref[0] = 0.5 * g * (1.0 + jnp.tanh(GELU_C * (g + 0.044715 * (g * g * g))))
    o += LRU_W
    uf = z[:, o:o + FNET_W].astype(BF16)
    gcs = jnp.dot(uf, cs_ref[...], preferred_element_type=F32)
    gc_ref[...] = gcs[:, :FNET_W].astype(BF16)
    gs_ref[...] = gcs[:, FNET_W:].astype(BF16)


def _in_proj(xc, mod2, w_in_bf, l, cos_t, sin_t, cs, *, nbl, tb):
    b, tt, d = xc.shape
    nbt = tt // tb
    kern = functools.partial(_in_kernel, d=d)
    tok = lambda w: pl.BlockSpec((1, tb, w), lambda i, j: (i, j, 0))
    return pl.pallas_call(
        kern,
        grid=(b, nbt),
        in_specs=[
            tok(d),
            pl.BlockSpec((1, 1, 6 * d), lambda i, j: (2 * i + (j >= nbl).astype(I32), 0, 0)),
            pl.BlockSpec((1, d, IN_W), lambda i, j: (l, 0, 0)),
            pl.BlockSpec((tb, LANES), lambda i, j: (j, 0)),
            pl.BlockSpec((tb, LANES), lambda i, j: (j, 0)),
            pl.BlockSpec((FNET_W, 2 * FNET_W), lambda i, j: (0, 0)),
        ],
        out_specs=[
            tok(2 * QK_W), tok(QK_W), tok(ATT_W), tok(LRU_W), tok(LRU_W),
            pl.BlockSpec((tb, FNET_W), lambda i, j: (j, i)),
            pl.BlockSpec((tb, FNET_W), lambda i, j: (j, i)),
        ],
        out_shape=[
            jax.ShapeDtypeStruct((b, tt, 2 * QK_W), BF16),
            jax.ShapeDtypeStruct((b, tt, QK_W), BF16),
            jax.ShapeDtypeStruct((b, tt, ATT_W), BF16),
            jax.ShapeDtypeStruct((b, tt, LRU_W), F32),
            jax.ShapeDtypeStruct((b, tt, LRU_W), F32),
            jax.ShapeDtypeStruct((tt, b * FNET_W), BF16),
            jax.ShapeDtypeStruct((tt, b * FNET_W), BF16),
        ],
        compiler_params=_params(("arbitrary", "arbitrary")),
        name="in_proj",
    )(xc, mod2, w_in_bf, cos_t, sin_t, cs)


def _attn_kernel(ql_ref, qc_ref, k_ref, v_ref, lamv_ref, lin_ref, g_ref, ol_ref, oc_ref, vt_ref, s0_ref, s1_ref,
                 *, n, lat_steps, qb):
    j = pl.program_id(2)
    lv = lamv_ref[0]
    lam_init = lin_ref[0][:, 0:1]
    lam = (jnp.exp(jnp.sum(lv[0:1] * lv[1:2], axis=1, keepdims=True))
           - jnp.exp(jnp.sum(lv[2:3] * lv[3:4], axis=1, keepdims=True)) + lam_init)
    gain = g_ref[0] * (1.0 - lam_init)
    nt = (((1,), (1,)), ((), ()))
    tt = k_ref.shape[1]

    @pl.when(j == 0)
    def _():
        vt_ref[0:HEAD_V, :] = v_ref[0].astype(F32).T.astype(BF16)
        row = lax.broadcasted_iota(I32, (BF16_ROWS, tt), 0)
        vt_ref[HEAD_V:, :] = jnp.where(row == 0, 1.0, 0.0).astype(BF16)

    def attend(q_ref, o_ref, blocks, lo):
        chunks = [(c, min(KEY_CHUNK, tt - c)) for c in range(lo, tt, KEY_CHUNK)]
        tq = s0_ref.shape[1]
        streams = [(blk, mp) for blk in range(blocks) for mp in range(2)]
        s_refs = (s0_ref, s1_ref)

        def scores(i, c, w, m):
            blk, mp = streams[i]
            q = q_ref[0, blk * tq:(blk + 1) * tq, mp * LANES:(mp + 1) * LANES]
            st = lax.dot_general(k_ref[0, c:c + w, :], q, nt, preferred_element_type=F32)
            s_refs[i % 2][c:c + w, :] = st
            mc = jnp.max(st, axis=0, keepdims=True)
            return mc if m is None else jnp.maximum(m, mc)

        def weighted_values(i, c, w, m, acc):
            pt = jnp.exp2(s_refs[i % 2][c:c + w, :] - m).astype(BF16)
            part = jnp.dot(vt_ref[:, c:c + w], pt, preferred_element_type=F32)
            return part if acc is None else acc + part

        ms = [None] * len(streams)
        accs = [None] * len(streams)
        for stage in range(len(streams) + 1):
            for c, w in chunks:
                if stage >= 1:
                    accs[stage - 1] = weighted_values(stage - 1, c, w, ms[stage - 1], accs[stage - 1])
                if stage < len(streams):
                    ms[stage] = scores(stage, c, w, ms[stage])
            if stage >= 2 and stage % 2 == 0:
                blk = stage // 2 - 1
                a0, a1 = accs[stage - 2], accs[stage - 1]
                ot = (a0[0:HEAD_V] / a0[HEAD_V:HEAD_V + 1]) - lam * (a1[0:HEAD_V] / a1[HEAD_V:HEAD_V + 1])
                rt = ot * lax.rsqrt(jnp.mean(ot * ot, axis=0, keepdims=True) + RMS_EPS)
                o_ref[0, blk * tq:(blk + 1) * tq, :] = (rt.T * gain).astype(BF16)

    @pl.when(j < lat_steps)
    def _():
        attend(ql_ref, ol_ref, qb, 0)

    @pl.when(j >= lat_steps)
    def _():
        attend(qc_ref, oc_ref, qc_ref.shape[1] // s0_ref.shape[1], n)


def _attention(qx, k, v, lamv, lin, gain, l, *, n, tb):
    b, tt, _ = k.shape
    cn = tt - n
    qb = _largest_divisor(n // tb, (8, 4, 2, 1))
    lat_steps = n // (qb * tb)
    assert n % cn == 0 and cn % tb == 0
    kern = functools.partial(_attn_kernel, n=n, lat_steps=lat_steps, qb=qb)
    lat = lambda w: pl.BlockSpec((1, qb * tb, w), lambda i, h, j: (i, jnp.minimum(j, lat_steps - 1), h))
    ctx = lambda w: pl.BlockSpec((1, cn, w), lambda i, h, j: (i, n // cn, h))
    return pl.pallas_call(
        kern,
        grid=(b, N_HEADS, lat_steps + 1),
        in_specs=[
            lat(2 * LANES), ctx(2 * LANES),
            pl.BlockSpec((1, tt, LANES), lambda i, h, j: (i, 0, h)),
            pl.BlockSpec((1, tt, LANES), lambda i, h, j: (i, 0, h)),
            pl.BlockSpec((1, 4, QK_DIM), lambda i, h, j: (l, 0, 0)),
            pl.BlockSpec((1, 1, LANES), lambda i, h, j: (l, 0, 0)),
            pl.BlockSpec((1, 1, HEAD_V), lambda i, h, j: (l * N_HEADS + h, 0, 0)),
        ],
        out_specs=[lat(LANES), pl.BlockSpec((1, cn, LANES), lambda i, h, j: (i, 0, h))],
        out_shape=[jax.ShapeDtypeStruct((b, n, ATT_W), BF16), jax.ShapeDtypeStruct((b, cn, ATT_W), BF16)],
        scratch_shapes=[pltpu.VMEM((HEAD_V + BF16_ROWS, tt), BF16),
                        pltpu.VMEM((tt, tb), F32), pltpu.VMEM((tt, tb), F32)],
        compiler_params=_params(("arbitrary", "arbitrary", "arbitrary")),
        name="diff_attention",
    )(qx, qx, k, v, lamv, lin, gain)


def _lru_kernel(ux_ref, gg_ref, cw_ref, cb_ref, wg_ref, bg_ref, lam_ref, y_ref,
                a_f, b_f, a_b, b_b, *, n, cn, r):
    tt = n + cn
    w = LANES
    cw = cw_ref[0]
    cb = cb_ref[0]
    bg = bg_ref[0]
    neg_lam = -lam_ref[0]
    softplus = jnp.maximum(neg_lam, 0.0) + jnp.log(1.0 + jnp.exp(-jnp.abs(neg_lam)))
    row8 = lax.broadcasted_iota(I32, (r, w), 0) & (SUBLANES - 1)
    ext_rows = r + 2 * SUBLANES

    def local_scan(a, bb, reverse):
        for s in (1, 2, 4):
            if reverse:
                a_sh = pltpu.roll(a, r - s, 0)
                b_sh = pltpu.roll(bb, r - s, 0)
                valid = row8 < SUBLANES - s
            else:
                a_sh = pltpu.roll(a, s, 0)
                b_sh = pltpu.roll(bb, s, 0)
                valid = row8 >= s
            bb = jnp.where(valid, a * b_sh + bb, bb)
            a = jnp.where(valid, a * a_sh, a)
        return a, bb

    def gates_chunk(c, carry):
        r0 = pl.multiple_of(c * r, r)
        seg_start = jnp.logical_or(r0 == 0, r0 == n)
        seg_end = jnp.logical_or(r0 + r == n, r0 + r == tt)
        main = ux_ref[0, pl.ds(r0, r), :]
        prev = ux_ref[0, pl.ds(pl.multiple_of(jnp.maximum(r0 - SUBLANES, 0), SUBLANES), SUBLANES), :]
        nxt = ux_ref[0, pl.ds(pl.multiple_of(jnp.minimum(r0 + r, tt - SUBLANES), SUBLANES), SUBLANES), :]
        prev = jnp.where(seg_start, 0.0, prev)
        nxt = jnp.where(seg_end, 0.0, nxt)
        ext = jnp.concatenate([prev, main, nxt], axis=0)
        u = cb
        for t in range(4):
            sh = (2 - t) % ext_rows
            win = ext if sh == 0 else pltpu.roll(ext, sh, 0)
            u = u + cw[t:t + 1, :] * win[SUBLANES:SUBLANES + r, :]
        zz = jnp.dot(u.astype(BF16), wg_ref[0], preferred_element_type=F32) + bg
        for dr, (a_s, b_s) in enumerate(((a_f, b_f), (a_b, b_b))):
            rg = jax.nn.sigmoid(zz[:, (2 * dr) * w:(2 * dr + 1) * w])
            ig = jax.nn.sigmoid(zz[:, (2 * dr + 1) * w:(2 * dr + 2) * w])
            a = jnp.exp(-LRU_C * rg * softplus[dr:dr + 1, :])
            bb = jnp.sqrt(1.0 - a * a) * ig * u
            a, bb = local_scan(a, bb, reverse=(dr == 1))
            a_s[pl.ds(r0, r), :] = a
            b_s[pl.ds(r0, r), :] = bb
        return carry

    lax.fori_loop(0, tt // r, gates_chunk, 0)

    def seg_scan(first_tile, ntiles, cf, cbk):
        def body(i, carry):
            cf, cbk = carry
            rf = pl.multiple_of((first_tile + i) * SUBLANES, SUBLANES)
            hf = b_f[pl.ds(rf, SUBLANES), :] + a_f[pl.ds(rf, SUBLANES), :] * cf
            b_f[pl.ds(rf, SUBLANES), :] = hf
            rb = pl.multiple_of((first_tile + ntiles - 1 - i) * SUBLANES, SUBLANES)
            hb = b_b[pl.ds(rb, SUBLANES), :] + a_b[pl.ds(rb, SUBLANES), :] * cbk
            b_b[pl.ds(rb, SUBLANES), :] = hb
            return hf[SUBLANES - 1:SUBLANES, :], hb[0:1, :]
        return lax.fori_loop(0, ntiles, body, (cf, cbk))

    zero = jnp.zeros((1, w), F32)
    cf, cbk = seg_scan(n // SUBLANES, cn // SUBLANES, zero, zero)
    seg_scan(0, n // SUBLANES, cf, cbk)

    def out_chunk(c, carry):
        r0 = pl.multiple_of(c * r, r)
        y = (b_f[pl.ds(r0, r), :] + b_b[pl.ds(r0, r), :]) * gg_ref[0, pl.ds(r0, r), :]
        y_ref[0, pl.ds(r0, r), :] = y.astype(BF16)
        return carry

    lax.fori_loop(0, tt // r, out_chunk, 0)


def _rglru(ux, gg, conv_w2, conv_b2, wgate, bgate, lam2, l, *, n, cn, tb):
    b, tt, _ = ux.shape
    kern = functools.partial(_lru_kernel, n=n, cn=cn, r=tb)
    half = lambda: pl.BlockSpec((1, tt, LANES), lambda i, hh: (i, 0, hh))
    return pl.pallas_call(
        kern,
        grid=(b, 2),
        in_specs=[
            half(), half(),
            pl.BlockSpec((1, 4, LANES), lambda i, hh: (2 * l + hh, 0, 0)),
            pl.BlockSpec((1, 1, LANES), lambda i, hh: (2 * l + hh, 0, 0)),
            pl.BlockSpec((1, LANES, 4 * LANES), lambda i, hh: (2 * l + hh, 0, 0)),
            pl.BlockSpec((1, 1, 4 * LANES), lambda i, hh: (2 * l + hh, 0, 0)),
            pl.BlockSpec((1, 2, LANES), lambda i, hh: (2 * l + hh, 0, 0)),
        ],
        out_specs=half(),
        out_shape=jax.ShapeDtypeStruct((b, tt, LRU_W), BF16),
        scratch_shapes=[pltpu.VMEM((tt, LANES), F32)] * 4,
        compiler_params=_params(("arbitrary", "arbitrary")),
        name="rglru",
    )(ux, gg, conv_w2, conv_b2, wgate, bgate, lam2)


def _dft_kernel(c_ref, s_ref, gc_ref, gs_ref, o_ref, acc_ref):
    kk = pl.program_id(2)

    def part():
        return (jnp.dot(c_ref[...], gc_ref[...], preferred_element_type=F32)
                + jnp.dot(s_ref[...], gs_ref[...], preferred_element_type=F32))

    @pl.when(kk == 0)
    def _():
        acc_ref[...] = part()

    @pl.when(kk > 0)
    def _():
        acc_ref[...] += part()

    @pl.when(kk == pl.num_programs(2) - 1)
    def _():
        o_ref[...] = acc_ref[...].astype(BF16)


def _position_dft(cmat, smat, gc, gs):
    tt = cmat.shape[0]
    nn = gc.shape[1]
    tm = _largest_divisor(tt, (1088, 640, 512, 256, 128))
    tk = _largest_divisor(tt, (2176, 640, 512, 256, 128))
    tn = _largest_divisor(nn, (512, 256))
    return pl.pallas_call(
        _dft_kernel,
        grid=(tt // tm, nn // tn, tt // tk),
        in_specs=[
            pl.BlockSpec((tm, tk), lambda i, j, k: (i, k)),
            pl.BlockSpec((tm, tk), lambda i, j, k: (i, k)),
            pl.BlockSpec((tk, tn), lambda i, j, k: (k, j)),
            pl.BlockSpec((tk, tn), lambda i, j, k: (k, j)),
        ],
        out_specs=pl.BlockSpec((tm, tn), lambda i, j, k: (i, j)),
        out_shape=jax.ShapeDtypeStruct((tt, nn), BF16),
        scratch_shapes=[pltpu.VMEM((tm, tn), F32)],
        compiler_params=_params(("arbitrary", "arbitrary", "arbitrary")),
        name="position_dft",
    )(cmat, smat, gc, gs)


def _out_kernel(attl_ref, attc_ref, y_ref, f_ref, x_ref, mod_ref, w_ref, g_ref, b_ref, wr_ref,
                x1_ref, h2_ref, lg_ref, mix0_ref, mix1_ref, *, d, alpha, nbl, nblocks):
    j = pl.program_id(1)

    @pl.when(j == 0)
    def _():
        mix1_ref[...] = jnp.zeros_like(mix1_ref)

    def step(new_ref, old_ref):
        m = mod_ref[0]
        tb = x_ref.shape[1]
        att = jnp.where(jnp.minimum(j, nblocks - 1) >= nbl, attc_ref[0], attl_ref[0])
        for c in range(OUT_PIECES):
            rows = slice(c * tb // OUT_PIECES, (c + 1) * tb // OUT_PIECES)
            cols = slice(c * d // OUT_PIECES, (c + 1) * d // OUT_PIECES)
            x1 = _ln(alpha * x_ref[0, rows, :] + m[:, 2 * d:3 * d] * old_ref[rows, :]) * g_ref[0] + b_ref[0]
            x1_ref[0, rows, :] = x1
            h2_ref[0, rows, :] = (_ln(x1) * (1.0 + m[:, 4 * d:5 * d]) + m[:, 3 * d:4 * d]).astype(BF16)
            new_ref[:, cols] = (
                jnp.dot(att, w_ref[0, 0:ATT_W, cols], preferred_element_type=F32)
                + jnp.dot(y_ref[0], w_ref[0, ATT_W:ATT_W + LRU_W, cols], preferred_element_type=F32)
                + jnp.dot(f_ref[...], w_ref[0, ATT_W + LRU_W:, cols], preferred_element_type=F32))
        lg_ref[0] = lax.dot_general(wr_ref[0], h2_ref[0], (((1,), (1,)), ((), ())), preferred_element_type=F32)

    @pl.when(j % 2 == 0)
    def _():
        step(mix0_ref, mix1_ref)

    @pl.when(j % 2 == 1)
    def _():
        step(mix1_ref, mix0_ref)


def _out_proj(att_l, att_c, y, fy, xc, mod2, w_out_bf, ln_g, ln_b, wr_t, l, *, nbl, tb, alpha, out_tokens):
    b, _, d = xc.shape
    nblocks = out_tokens // tb
    kern = functools.partial(_out_kernel, d=d, alpha=alpha, nbl=nbl, nblocks=nblocks)
    cur = lambda j: jnp.minimum(j, nblocks - 1)
    prv = lambda j: jnp.maximum(j - 1, 0)
    tok = lambda w: pl.BlockSpec((1, tb, w), lambda i, j: (i, prv(j), 0))
    return pl.pallas_call(
        kern,
        grid=(b, nblocks + 1),
        in_specs=[
            pl.BlockSpec((1, tb, ATT_W), lambda i, j: (i, jnp.minimum(cur(j), nbl - 1), 0)),
            pl.BlockSpec((1, tb, ATT_W), lambda i, j: (i, jnp.maximum(cur(j) - nbl, 0), 0)),
            pl.BlockSpec((1, tb, LRU_W), lambda i, j: (i, cur(j), 0)),
            pl.BlockSpec((tb, FNET_W), lambda i, j: (cur(j), i)),
            tok(d),
            pl.BlockSpec((1, 1, 6 * d), lambda i, j: (2 * i + (prv(j) >= nbl).astype(I32), 0, 0)),
            pl.BlockSpec((1, d, d), lambda i, j: (l, 0, 0)),
            pl.BlockSpec((1, 1, d), lambda i, j: (l, 0, 0)),
            pl.BlockSpec((1, 1, d), lambda i, j: (l, 0, 0)),
            pl.BlockSpec((1, N_EXPERTS, d), lambda i, j: (l, 0, 0)),
        ],
        out_specs=[tok(d), tok(d), pl.BlockSpec((1, N_EXPERTS, tb), lambda i, j: (i, 0, prv(j)))],
        out_shape=[
            jax.ShapeDtypeStruct((b, out_tokens, d), F32),
            jax.ShapeDtypeStruct((b, out_tokens, d), BF16),
            jax.ShapeDtypeStruct((b, N_EXPERTS, out_tokens), F32),
        ],
        scratch_shapes=[pltpu.VMEM((tb, d), F32), pltpu.VMEM((tb, d), F32)],
        compiler_params=_params(("arbitrary", "arbitrary")),
        name="out_proj",
    )(att_l, att_c, y, fy, xc, mod2, w_out_bf, ln_g, ln_b, wr_t)


def _route_kernel(lg_ref, pos_ref, gate_ref, starts_ref, *, segments, n_slots):
    lg = lg_ref[0]
    lane_id = lax.broadcasted_iota(I32, (N_EXPERTS, LANES), 1)
    starts = jnp.zeros((N_EXPERTS, LANES), I32)
    chunks_per_tile = TOKEN_TILE // LANES
    e = jnp.exp(lg - jnp.max(lg, axis=0, keepdims=True))
    s = e / jnp.sum(e, axis=0, keepdims=True)
    ri = lax.broadcasted_iota(I32, (LANES, LANES), 0)
    ci = lax.broadcasted_iota(I32, (LANES, LANES), 1)
    strict_upper = jnp.where(ri < ci, 1.0, 0.0).astype(BF16)

    for lo, t, cap, base in segments:
        ss = s[:, lo:lo + t]
        bits = pltpu.bitcast(ss, I32)
        capf = float(cap)

        def search(i, thr, bits=bits, capf=capf):
            cand = thr | jnp.left_shift(jnp.int32(1), 30 - i)
            cnt = jnp.sum(jnp.where(bits >= cand, 1.0, 0.0), axis=1, keepdims=True)
            return jnp.where(cnt >= capf, cand, thr)

        thr = lax.fori_loop(0, 31, search, jnp.zeros((N_EXPERTS, 1), I32))
        need = capf - jnp.sum(jnp.where(bits > thr, 1.0, 0.0), axis=1, keepdims=True)
        off_eq = jnp.zeros((N_EXPERTS, 1), F32)
        off_sel = jnp.zeros((N_EXPERTS, 1), F32)
        for c in range(t // LANES):
            sl = slice(c * LANES, (c + 1) * LANES)
            tile, sub = divmod(lo // LANES + c, chunks_per_tile)
            if sub == 0:
                starts = jnp.where(lane_id == tile, off_sel.astype(I32) + base, starts)
            bits_c = bits[:, sl]
            eq = bits_c == thr
            eq_c = jnp.where(eq, 1.0, 0.0)
            rank_eq = jnp.dot(eq_c.astype(BF16), strict_upper, preferred_element_type=F32) + off_eq
            off_eq = off_eq + jnp.sum(eq_c, axis=1, keepdims=True)
            sel = jnp.logical_or(bits_c > thr, jnp.logical_and(eq, rank_eq < need))
            sel_c = jnp.where(sel, 1.0, 0.0)
            slot = jnp.dot(sel_c.astype(BF16), strict_upper, preferred_element_type=F32) + off_sel
            off_sel = off_sel + jnp.sum(sel_c, axis=1, keepdims=True)
            osl = slice(sub * LANES, (sub + 1) * LANES)
            pos_ref[0, tile, :, osl] = jnp.where(sel, slot.astype(I32) + base, -1)
            gate_ref[0, tile, :, osl] = jnp.where(sel, ss[:, sl], 0.0)

    n_tiles = lg.shape[1] // TOKEN_TILE
    starts_ref[0] = jnp.where(lane_id == n_tiles, n_slots, starts)


def _route(logits_t, *, segments, n_slots):
    b, ne, tt = logits_t.shape
    nt = tt // TOKEN_TILE
    assert nt < LANES and all(lo % TOKEN_TILE == 0 and t % TOKEN_TILE == 0 for lo, t, _, _ in segments)
    kern = functools.partial(_route_kernel, segments=segments, n_slots=n_slots)
    tiled = lambda: pl.BlockSpec((1, nt, ne, TOKEN_TILE), lambda i: (i, 0, 0, 0))
    return pl.pallas_call(
        kern,
        grid=(b,),
        in_specs=[pl.BlockSpec((1, ne, tt), lambda i: (i, 0, 0))],
        out_specs=[tiled(), tiled(), pl.BlockSpec((1, ne, LANES), lambda i: (i, 0, 0))],
        out_shape=[jax.ShapeDtypeStruct((b, nt, ne, TOKEN_TILE), I32),
                   jax.ShapeDtypeStruct((b, nt, ne, TOKEN_TILE), F32),
                   jax.ShapeDtypeStruct((b, ne, LANES), I32)],
        compiler_params=_params(("arbitrary",)),
        name="route",
    )(logits_t)


def _window_start(starts_ref, base_idx, tile, w, win, limit):
    c0 = starts_ref[base_idx + tile]
    lo = ((c0 >> 4) << 4) + w * win
    return lo, pl.multiple_of(jnp.minimum(lo, limit), BF16_ROWS)


def _window_count(starts_ref, base_idx, tiles, win):
    nw = jnp.int32(1)
    for tile in tiles:
        c0 = starts_ref[base_idx + tile]
        c1 = starts_ref[base_idx + tile + 1]
        nw = jnp.maximum(nw, (c1 - ((c0 >> 4) << 4) + win - 1) // win)
    return nw


def _gather_kernel(starts_ref, h_ref, pos_ref, xs_ref, acc_ref, *, slots, win, group):
    b = pl.program_id(0)
    eg = pl.program_id(1)
    nt = pos_ref.shape[1]
    acc_ref[...] = jnp.zeros_like(acc_ref)
    rel = lax.broadcasted_iota(I32, (win, TOKEN_TILE), 0)
    bases = [(b * N_EXPERTS + eg * group + i) * LANES for i in range(group)]
    nw = jnp.int32(1)
    for i in range(group):
        nw = jnp.maximum(nw, _window_count(starts_ref, bases[i], range(nt), win))

    def window_pass(w, carry):
        for tile in range(nt):
            onehots, offs = [], []
            for i in range(group):
                lo, start = _window_start(starts_ref, bases[i], tile, w, win, slots)
                prow = pos_ref[0, tile, pl.ds(eg * group + i, 1), :] - start
                onehots.append(jnp.where(rel == prow, 1.0, 0.0).astype(BF16))
                offs.append(start)
            res = jnp.dot(jnp.concatenate(onehots, axis=0), h_ref[0, tile * TOKEN_TILE:(tile + 1) * TOKEN_TILE, :],
                          preferred_element_type=F32)
            for i in range(group):
                acc_ref[i, pl.ds(offs[i], win), :] += res[i * win:(i + 1) * win].astype(BF16)
        return carry

    lax.fori_loop(0, nw, window_pass, 0)
    xs_ref[...] = acc_ref[:, 0:slots, :]


def _gather(starts, h2, pos, *, slots, win, group=8):
    b, tt, d = h2.shape
    nt = tt // TOKEN_TILE
    kern = functools.partial(_gather_kernel, slots=slots, win=win, group=group)
    return pl.pallas_call(
        kern,
        grid_spec=pltpu.PrefetchScalarGridSpec(
            num_scalar_prefetch=1,
            grid=(b, N_EXPERTS // group),
            in_specs=[
                pl.BlockSpec((1, tt, d), lambda i, g, s: (i, 0, 0)),
                pl.BlockSpec((1, nt, N_EXPERTS, TOKEN_TILE), lambda i, g, s: (i, 0, 0, 0)),
            ],
            out_specs=pl.BlockSpec((group, slots, d), lambda i, g, s: (g, i, 0)),
            scratch_shapes=[pltpu.VMEM((group, slots + win, d), BF16)],
        ),
        out_shape=jax.ShapeDtypeStruct((N_EXPERTS, b * slots, d), BF16),
        compiler_params=_params(("arbitrary", "arbitrary")),
        name="moe_gather",
    )(starts, h2, pos)


def _ffn_kernel(xs_ref, wg_ref, wu_ref, wd_ref, ys_ref, *, fchunk):
    xs = xs_ref[0]
    f = wg_ref.shape[-1]
    acc = None
    for c in range(f // fchunk):
        sl = slice(c * fchunk, (c + 1) * fchunk)
        a = jnp.dot(xs, wg_ref[0, 0, :, sl], preferred_element_type=F32)
        u = jnp.dot(xs, wu_ref[0, 0, :, sl], preferred_element_type=F32)
        hm = (a * jax.nn.sigmoid(a) * u).astype(BF16)
        y = jnp.dot(hm, wd_ref[0, 0, sl, :], preferred_element_type=F32)
        acc = y if acc is None else acc + y
    ys_ref[0] = acc.astype(BF16)


def _expert_ffn(xs, wg, wu, wd, l, *, slots):
    ne, rows, d = xs.shape
    f = wg.shape[-1]
    nb = rows // slots
    kern = functools.partial(_ffn_kernel, fchunk=_largest_divisor(f, (512,)))
    return pl.pallas_call(
        kern,
        grid=(ne, nb),
        in_specs=[
            pl.BlockSpec((1, slots, d), lambda e, i: (e, i, 0)),
            pl.BlockSpec((1, 1, d, f), lambda e, i: (l, e, 0, 0)),
            pl.BlockSpec((1, 1, d, f), lambda e, i: (l, e, 0, 0)),
            pl.BlockSpec((1, 1, f, d), lambda e, i: (l, e, 0, 0)),
        ],
        out_specs=pl.BlockSpec((1, slots, d), lambda e, i: (e, i, 0)),
        out_shape=jax.ShapeDtypeStruct((ne, rows, d), BF16),
        compiler_params=_params(("arbitrary", "arbitrary")),
        name="expert_ffn",
    )(xs, wg, wu, wd)


def _combine_kernel(starts_ref, ys_ref, pos_ref, gate_ref, x1_ref, mod_ref, g_ref, b_ref, o_ref,
                    *, slots, win, d, alpha):
    b = pl.program_id(0)
    tile = pl.program_id(1)
    rel = lax.broadcasted_iota(I32, (win, TOKEN_TILE), 0)
    bases = [(b * N_EXPERTS + e) * LANES for e in range(N_EXPERTS)]
    nw = jnp.int32(1)
    for e in range(N_EXPERTS):
        c0 = starts_ref[bases[e] + tile]
        c1 = starts_ref[bases[e] + tile + 1]
        nw = jnp.maximum(nw, (c1 - ((c0 >> 4) << 4) + win - 1) // win)

    def window_pass(w, moe):
        gated, rows = [], []
        for e in range(N_EXPERTS):
            lo, start = _window_start(starts_ref, bases[e], tile, w, win, slots - win)
            prow = pos_ref[0, 0, e:e + 1, :]
            prow = jnp.where(jnp.logical_and(prow >= lo, prow < lo + win), prow - start, -1)
            gated.append(jnp.where(rel == prow, gate_ref[0, 0, e:e + 1, :], 0.0).astype(BF16))
            rows.append(ys_ref[e, pl.ds(start, win), :])
        return moe + lax.dot_general(jnp.concatenate(gated, axis=0), jnp.concatenate(rows, axis=0),
                                     (((0,), (0,)), ((), ())), preferred_element_type=F32)

    moe = lax.fori_loop(0, nw, window_pass, jnp.zeros((TOKEN_TILE, d), F32))
    m = mod_ref[0]
    o_ref[0] = _ln(alpha * x1_ref[0] + m[:, 5 * d:6 * d] * moe) * g_ref[0] + b_ref[0]


def _combine(starts, ys, pos, gate, x1, mod2, ln_g, ln_b, l, *, slots, win, n_lat_tiles, alpha, out_tokens):
    b, _, d = x1.shape
    kern = functools.partial(_combine_kernel, slots=slots, win=win, d=d, alpha=alpha)
    tiled = lambda: pl.BlockSpec((1, 1, N_EXPERTS, TOKEN_TILE), lambda i, t, s: (i, t, 0, 0))
    tok = lambda: pl.BlockSpec((1, TOKEN_TILE, d), lambda i, t, s: (i, t, 0))
    return pl.pallas_call(
        kern,
        grid_spec=pltpu.PrefetchScalarGridSpec(
            num_scalar_prefetch=1,
            grid=(b, out_tokens // TOKEN_TILE),
            in_specs=[
                pl.BlockSpec((N_EXPERTS, slots, d), lambda i, t, s: (0, i, 0)),
                tiled(), tiled(), tok(),
                pl.BlockSpec((1, 1, 6 * d), lambda i, t, s: (2 * i + (t >= n_lat_tiles).astype(I32), 0, 0)),
                pl.BlockSpec((1, 1, d), lambda i, t, s: (l, 0, 0)),
                pl.BlockSpec((1, 1, d), lambda i, t, s: (l, 0, 0)),
            ],
            out_specs=tok(),
        ),
        out_shape=jax.ShapeDtypeStruct((b, out_tokens, d), F32),
        compiler_params=_params(("arbitrary", "arbitrary")),
        name="moe_combine",
    )(starts, ys, pos, gate, x1, mod2, ln_g, ln_b)


def _rope_tables(n, cn):
    lane = np.arange(LANES)
    within = lane % QK_DIM
    use_col = (within // 32) == 1
    first_half = (within % 32) < 16
    inv = ROPE_BASE ** (-(within % 16).astype(np.float64) / 16.0)
    pos = np.arange(n)
    coord = np.where(use_col[None, :], (pos % GRID_W)[:, None], (pos // GRID_W)[:, None]).astype(np.float32)
    ang = jnp.asarray(coord) * jnp.asarray(inv.astype(np.float32))[None, :]
    cos = jnp.cos(ang)
    sin = jnp.where(jnp.asarray(first_half)[None, :], -jnp.sin(ang), jnp.sin(ang))
    cos = jnp.concatenate([cos, jnp.ones((cn, LANES), F32)], axis=0)
    sin = jnp.concatenate([sin, jnp.zeros((cn, LANES), F32)], axis=0)
    return cos, sin


def _channel_dft():
    idx = np.arange(FNET_W)
    same = (idx[:, None] // FNET_GROUP_W) == (idx[None, :] // FNET_GROUP_W)
    ang = 2.0 * np.pi * ((idx[:, None] % FNET_GROUP_W) * (idx[None, :] % FNET_GROUP_W) % FNET_GROUP_W) / FNET_GROUP_W
    cs = np.concatenate([np.where(same, np.cos(ang), 0.0), np.where(same, np.sin(ang), 0.0)], axis=1)
    return jnp.asarray(cs.astype(np.float32)).astype(BF16)


def _position_dft_mats(n, cn):
    def segment(t):
        g = math.gcd(t, 64)
        kk = jnp.arange(t, dtype=I32)

        def table(m):
            ph = ((kk[:, None] * m[None, :]) % t).astype(F32) * (2.0 * math.pi / t)
            return jnp.cos(ph), jnp.sin(ph)

        ch, sh = table(jnp.arange(t // g, dtype=I32) * g)
        cl, sl = table(jnp.arange(g, dtype=I32))
        scale = 1.0 / math.sqrt(t * FNET_GROUP_W)
        cmat = (ch[:, :, None] * cl[:, None, :] - sh[:, :, None] * sl[:, None, :]).reshape(t, t) * scale
        smat = (sh[:, :, None] * cl[:, None, :] + ch[:, :, None] * sl[:, None, :]).reshape(t, t) * (-scale)
        return cmat.astype(BF16), smat.astype(BF16)

    def block_diag(a, c):
        top = jnp.concatenate([a, jnp.zeros((n, cn), BF16)], axis=1)
        bot = jnp.concatenate([jnp.zeros((cn, n), BF16), c], axis=1)
        return jnp.concatenate([top, bot], axis=0)

    (cl_, sl_), (cc_, sc_) = segment(n), segment(cn)
    return block_diag(cl_, cc_), block_diag(sl_, sc_)


def _block_diag_gates(wa, wx):
    depth = wa.shape[0]

    def dense(wb):
        eye = jnp.eye(LRU_BLOCKS, dtype=wb.dtype)
        return jnp.einsum('lncd,nm->lncmd', wb, eye).reshape(depth, LRU_W, LRU_W)

    halves = []
    for hh in range(2):
        sl = slice(hh * LANES, (hh + 1) * LANES)
        cols = [dense(wmat[:, dr])[:, sl, sl] for dr in range(2) for wmat in (wa, wx)]
        halves.append(jnp.concatenate(cols, axis=-1))
    return jnp.stack(halves, axis=1).reshape(depth * 2, LANES, 4 * LANES).astype(BF16)


def _gate_bias(ba, bx):
    depth = ba.shape[0]
    halves = []
    for hh in range(2):
        sl = slice(hh * LANES, (hh + 1) * LANES)
        halves.append(jnp.concatenate([bvec[:, dr, sl] for dr in range(2) for bvec in (ba, bx)], axis=-1))
    return jnp.stack(halves, axis=1).reshape(depth * 2, 1, 4 * LANES)


def _split_halves(a):
    depth, r, _ = a.shape
    return a.reshape(depth, r, 2, LANES).transpose(0, 2, 1, 3).reshape(depth * 2, r, LANES)


def kernel(x, c, ctx, c_ctx, w_mod, b_mod, w_in, lam_q1, lam_k1, lam_q2, lam_k2, attn_norm_g, conv_w, conv_b, lru_wa, lru_ba, lru_wx, lru_bx, lru_lam, w_out, ln1_g, ln1_b, w_router, w_gate, w_up, w_down, ln2_g, ln2_b):
    b, n, d = x.shape
    cn = ctx.shape[1]
    depth = w_mod.shape[0]
    tt = n + cn
    tb = _largest_divisor(math.gcd(n, cn), (256, 128))
    assert n % tb == 0 and cn % tb == 0 and n % GRID_W == 0 and tt % LANES == 0
    nbl = n // tb
    cap_l = CAPACITY_FACTOR * n // N_EXPERTS
    cap_c = CAPACITY_FACTOR * cn // N_EXPERTS
    assert cap_l % BF16_ROWS == 0 and cap_c % BF16_ROWS == 0 and n % TOKEN_TILE == 0 and cn % TOKEN_TILE == 0
    alpha = (2 * depth) ** 0.25

    rows = -(-(b + 1) // SUBLANES) * SUBLANES
    cc = jnp.concatenate([c, c_ctx[None, :], jnp.zeros((rows - b - 1, d), F32)], axis=0)
    mod = _modulation(cc, w_mod, b_mod)

    cos_t, sin_t = _rope_tables(n, cn)
    cs = _channel_dft()
    cmat, smat = _position_dft_mats(n, cn)

    w_in_bf = w_in.astype(BF16)
    w_out_bf = w_out.astype(BF16)
    wg_bf = w_gate.astype(BF16)
    wu_bf = w_up.astype(BF16)
    wd_bf = w_down.astype(BF16)
    wr_t = jnp.swapaxes(w_router, 1, 2).astype(BF16)
    lamv = jnp.stack([lam_q1, lam_k1, lam_q2, lam_k2], axis=1).astype(F32)
    lam_init = np.array([0.8 - 0.6 * math.exp(-0.3 * l) for l in range(depth)], np.float32)
    lin = jnp.asarray(np.broadcast_to(lam_init[:, None, None], (depth, 1, LANES)).copy())
    gain = attn_norm_g.reshape(depth * N_HEADS, 1, HEAD_V)
    conv_w2 = _split_halves(conv_w)
    conv_b2 = _split_halves(conv_b[:, None, :])
    lam2 = _split_halves(lru_lam)
    wgate = _block_diag_gates(lru_wa, lru_wx)
    bgate = _gate_bias(lru_ba, lru_bx)
    ln1g, ln1b = ln1_g[:, None, :], ln1_b[:, None, :]
    ln2g, ln2b = ln2_g[:, None, :], ln2_b[:, None, :]

    xc = jnp.concatenate([x, ctx], axis=1)
    for l in range(depth):
        ml = mod[l]
        mod2 = jnp.stack([ml[:b], jnp.broadcast_to(ml[b][None, :], (b, 6 * d))], axis=1).reshape(2 * b, 1, 6 * d)
        qx, k, v, ux, gg, gc, gs = _in_proj(xc, mod2, w_in_bf, l, cos_t, sin_t, cs, nbl=nbl, tb=tb)
        keep_ctx = l < depth - 1
        tokens = tt if keep_ctx else n
        att_l, att_c = _attention(qx, k, v, lamv, lin, gain, l, n=n, tb=tb)
        y = _rglru(ux, gg, conv_w2, conv_b2, wgate, bgate, lam2, l, n=n, cn=cn, tb=tb)
        fy = _position_dft(cmat, smat, gc, gs)
        x1, h2, logits_t = _out_proj(att_l, att_c, y, fy, xc, mod2, w_out_bf, ln1g, ln1b, wr_t, l,
                                     nbl=nbl, tb=tb, alpha=alpha, out_tokens=tokens)
        segments = ((0, n, cap_l, 0), (n, cn, cap_c, cap_l)) if keep_ctx else ((0, n, cap_l, 0),)
        slots = cap_l + cap_c if keep_ctx else cap_l
        win = min(SLOT_WINDOW, slots)
        pos, gate, starts = _route(logits_t, segments=segments, n_slots=slots)
        starts = starts.reshape(-1)
        xs = _gather(starts, h2, pos, slots=slots, win=win)
        ys = _expert_ffn(xs, wg_bf, wu_bf, wd_bf, l, slots=slots)
        xc = _combine(starts, ys, pos, gate, x1, mod2, ln2g, ln2b, l, slots=slots, win=win,
                      n_lat_tiles=n // TOKEN_TILE, alpha=alpha, out_tokens=tokens)
    return xc
```

```python
import functools
import math

import numpy as np
import jax
import jax.numpy as jnp
from jax import lax
from jax.experimental import pallas as pl
from jax.experimental.pallas import tpu as pltpu

F32 = jnp.float32
BF16 = jnp.bfloat16
I32 = jnp.int32

GRID_W = 64
QK_DIM = 64
N_HEADS = 4
HEAD_V = 128
QK_W = 512
ATT_W = 512
LRU_W = 256
LRU_BLOCKS = 4
LRU_BLOCK_W = 64
LRU_C = 8.0
FNET_W = 256
FNET_GROUP_W = 64
IN_W = 2304
ROPE_BASE = 10000.0
N_EXPERTS = 16
CAPACITY_FACTOR = 2
LN_EPS = 1e-5
RMS_EPS = 1e-6
GELU_C = math.sqrt(2.0 / math.pi)
LOG2E = math.log2(math.e)

LANES = 128
SUBLANES = 8
BF16_ROWS = 16
KEY_CHUNK = 256
OUT_PIECES = 4
TOKEN_TILE = 256
SLOT_WINDOW = 64
VMEM_LIMIT = 56 << 20


def _params(sem, vmem=VMEM_LIMIT):
    return pltpu.CompilerParams(dimension_semantics=sem, vmem_limit_bytes=vmem)


def _ln(x):
    mu = jnp.mean(x, axis=-1, keepdims=True)
    xc = x - mu
    var = jnp.mean(xc * xc, axis=-1, keepdims=True)
    return xc * lax.rsqrt(var + LN_EPS)


def _largest_divisor(n, candidates):
    for c in candidates:
        if c <= n and n % c == 0:
            return c
    return n


def _mod_kernel(c_ref, w_ref, b_ref, o_ref):
    c = c_ref[...]
    s = c * jax.nn.sigmoid(c)
    o_ref[0] = jnp.dot(s, w_ref[0], precision=lax.Precision.HIGHEST, preferred_element_type=F32) + b_ref[0]


def _modulation(cc, w_mod, b_mod):
    depth, d, d6 = w_mod.shape
    rows = cc.shape[0]
    tn = 1024
    return pl.pallas_call(
        _mod_kernel,
        grid=(depth, d6 // tn),
        in_specs=[
            pl.BlockSpec((rows, d), lambda l, n: (0, 0)),
            pl.BlockSpec((1, d, tn), lambda l, n: (l, 0, n)),
            pl.BlockSpec((1, 1, tn), lambda l, n: (l, 0, n)),
        ],
        out_specs=pl.BlockSpec((1, rows, tn), lambda l, n: (l, 0, n)),
        out_shape=jax.ShapeDtypeStruct((depth, rows, d6), F32),
        compiler_params=_params(("arbitrary", "arbitrary")),
        name="modulation",
    )(cc, w_mod, b_mod.reshape(depth, 1, d6))


def _in_kernel(x_ref, mod_ref, w_ref, cos_ref, sin_ref, cs_ref,
               qx_ref, k_ref, v_ref, ux_ref, gg_ref, gc_ref, gs_ref, *, d):
    x = x_ref[0]
    m = mod_ref[0]
    h = _ln(x) * (1.0 + m[:, d:2 * d]) + m[:, 0:d]
    z = jnp.dot(h.astype(BF16), w_ref[0], preferred_element_type=F32)

    tb = x.shape[0]
    cos = cos_ref[...]
    sin = sin_ref[...]
    lane = lax.broadcasted_iota(I32, (tb, LANES), 1)
    first_half = (lane & 31) < 16
    low_map = lane < QK_DIM

    def rope(t):
        partner = jnp.where(first_half, pltpu.roll(t, LANES - 16, 1), pltpu.roll(t, 16, 1))
        return t * cos + partner * sin

    for p in range(N_HEADS):
        qp = rope(z[:, p * LANES:(p + 1) * LANES] * (QK_DIM ** -0.5 * LOG2E))
        qx_ref[0, :, (2 * p) * LANES:(2 * p + 1) * LANES] = jnp.where(low_map, qp, 0.0).astype(BF16)
        qx_ref[0, :, (2 * p + 1) * LANES:(2 * p + 2) * LANES] = jnp.where(low_map, 0.0, qp).astype(BF16)
        kp = rope(z[:, QK_W + p * LANES:QK_W + (p + 1) * LANES])
        k_ref[0, :, p * LANES:(p + 1) * LANES] = kp.astype(BF16)

    o = 2 * QK_W
    v_ref[0] = z[:, o:o + ATT_W].astype(BF16)
    o += ATT_W
    ux_ref[0] = z[:, o:o + LRU_W]
    o += LRU_W
    g = z[:, o:o + LRU_W]
    gg_ref[0] = 0.5 * g * (1.0 + jnp.tanh(GELU_C * (g + 0.044715 * (g * g * g))))
    o += LRU_W
    uf = z[:, o:o + FNET_W].astype(BF16)
    gcs = jnp.dot(uf, cs_ref[...], preferred_element_type=F32)
    gc_ref[...] = gcs[:, :FNET_W].astype(BF16)
    gs_ref[...] = gcs[:, FNET_W:].astype(BF16)


def _in_proj(xc, mod2, w_in_bf, l, cos_t, sin_t, cs, *, nbl, tb):
    b, tt, d = xc.shape
    nbt = tt // tb
    kern = functools.partial(_in_kernel, d=d)
    tok = lambda w: pl.BlockSpec((1, tb, w), lambda i, j: (i, j, 0))
    return pl.pallas_call(
        kern,
        grid=(b, nbt),
        in_specs=[
            tok(d),
            pl.BlockSpec((1, 1, 6 * d), lambda i, j: (2 * i + (j >= nbl).astype(I32), 0, 0)),
            pl.BlockSpec((1, d, IN_W), lambda i, j: (l, 0, 0)),
            pl.BlockSpec((tb, LANES), lambda i, j: (j, 0)),
            pl.BlockSpec((tb, LANES), lambda i, j: (j, 0)),
            pl.BlockSpec((FNET_W, 2 * FNET_W), lambda i, j: (0, 0)),
        ],
        out_specs=[
            tok(2 * QK_W), tok(QK_W), tok(ATT_W), tok(LRU_W), tok(LRU_W),
            pl.BlockSpec((tb, FNET_W), lambda i, j: (j, i)),
            pl.BlockSpec((tb, FNET_W), lambda i, j: (j, i)),
        ],
        out_shape=[
            jax.ShapeDtypeStruct((b, tt, 2 * QK_W), BF16),
            jax.ShapeDtypeStruct((b, tt, QK_W), BF16),
            jax.ShapeDtypeStruct((b, tt, ATT_W), BF16),
            jax.ShapeDtypeStruct((b, tt, LRU_W), F32),
            jax.ShapeDtypeStruct((b, tt, LRU_W), F32),
            jax.ShapeDtypeStruct((tt, b * FNET_W), BF16),
            jax.ShapeDtypeStruct((tt, b * FNET_W), BF16),
        ],
        compiler_params=_params(("arbitrary", "arbitrary")),
        name="in_proj",
    )(xc, mod2, w_in_bf, cos_t, sin_t, cs)


def _attn_kernel(ql_ref, qc_ref, k_ref, v_ref, lamv_ref, lin_ref, g_ref, ol_ref, oc_ref, vt_ref, s0_ref, s1_ref,
                 *, n, lat_steps, qb):
    j = pl.program_id(2)
    lv = lamv_ref[0]
    lam_init = lin_ref[0][:, 0:1]
    lam = (jnp.exp(jnp.sum(lv[0:1] * lv[1:2], axis=1, keepdims=True))
           - jnp.exp(jnp.sum(lv[2:3] * lv[3:4], axis=1, keepdims=True)) + lam_init)
    gain = g_ref[0] * (1.0 - lam_init)
    nt = (((1,), (1,)), ((), ()))
    tt = k_ref.shape[1]

    @pl.when(j == 0)
    def _():
        vt_ref[0:HEAD_V, :] = v_ref[0].astype(F32).T.astype(BF16)
        row = lax.broadcasted_iota(I32, (BF16_ROWS, tt), 0)
        vt_ref[HEAD_V:, :] = jnp.where(row == 0, 1.0, 0.0).astype(BF16)

    def attend(q_ref, o_ref, blocks, lo):
        chunks = [(c, min(KEY_CHUNK, tt - c)) for c in range(lo, tt, KEY_CHUNK)]
        tq = s0_ref.shape[1]
        streams = [(blk, mp) for blk in range(blocks) for mp in range(2)]
        s_refs = (s0_ref, s1_ref)

        def scores(i, c, w, m):
            blk, mp = streams[i]
            q = q_ref[0, blk * tq:(blk + 1) * tq, mp * LANES:(mp + 1) * LANES]
            st = lax.dot_general(k_ref[0, c:c + w, :], q, nt, preferred_element_type=F32)
            s_refs[i % 2][c:c + w, :] = st
            mc = jnp.max(st, axis=0, keepdims=True)
            return mc if m is None else jnp.maximum(m, mc)

        def weighted_values(i, c, w, m, acc):
            pt = jnp.exp2(s_refs[i % 2][c:c + w, :] - m).astype(BF16)
            part = jnp.dot(vt_ref[:, c:c + w], pt, preferred_element_type=F32)
            return part if acc is None else acc + part

        ms = [None] * len(streams)
        accs = [None] * len(streams)
        for stage in range(len(streams) + 1):
            for c, w in chunks:
                if stage >= 1:
                    accs[stage - 1] = weighted_values(stage - 1, c, w, ms[stage - 1], accs[stage - 1])
                if stage < len(streams):
                    ms[stage] = scores(stage, c, w, ms[stage])
            if stage >= 2 and stage % 2 == 0:
                blk = stage // 2 - 1
                a0, a1 = accs[stage - 2], accs[stage - 1]
                ot = (a0[0:HEAD_V] / a0[HEAD_V:HEAD_V + 1]) - lam * (a1[0:HEAD_V] / a1[HEAD_V:HEAD_V + 1])
                rt = ot * lax.rsqrt(jnp.mean(ot * ot, axis=0, keepdims=True) + RMS_EPS)
                o_ref[0, blk * tq:(blk + 1) * tq, :] = (rt.T * gain).astype(BF16)

    @pl.when(j < lat_steps)
    def _():
        attend(ql_ref, ol_ref, qb, 0)

    @pl.when(j >= lat_steps)
    def _():
        attend(qc_ref, oc_ref, qc_ref.shape[1] // s0_ref.shape[1], n)


def _attention(qx, k, v, lamv, lin, gain, l, *, n, tb):
    b, tt, _ = k.shape
    cn = tt - n
    qb = _largest_divisor(n // tb, (8, 4, 2, 1))
    lat_steps = n // (qb * tb)
    assert n % cn == 0 and cn % tb == 0
    kern = functools.partial(_attn_kernel, n=n, lat_steps=lat_steps, qb=qb)
    lat = lambda w: pl.BlockSpec((1, qb * tb, w), lambda i, h, j: (i, jnp.minimum(j, lat_steps - 1), h))
    ctx = lambda w: pl.BlockSpec((1, cn, w), lambda i, h, j: (i, n // cn, h))
    return pl.pallas_call(
        kern,
        grid=(b, N_HEADS, lat_steps + 1),
        in_specs=[
            lat(2 * LANES), ctx(2 * LANES),
            pl.BlockSpec((1, tt, LANES), lambda i, h, j: (i, 0, h)),
            pl.BlockSpec((1, tt, LANES), lambda i, h, j: (i, 0, h)),
            pl.BlockSpec((1, 4, QK_DIM), lambda i, h, j: (l, 0, 0)),
            pl.BlockSpec((1, 1, LANES), lambda i, h, j: (l, 0, 0)),
            pl.BlockSpec((1, 1, HEAD_V), lambda i, h, j: (l * N_HEADS + h, 0, 0)),
        ],
        out_specs=[lat(LANES), pl.BlockSpec((1, cn, LANES), lambda i, h, j: (i, 0, h))],
        out_shape=[jax.ShapeDtypeStruct((b, n, ATT_W), BF16), jax.ShapeDtypeStruct((b, cn, ATT_W), BF16)],
        scratch_shapes=[pltpu.VMEM((HEAD_V + BF16_ROWS, tt), BF16),
                        pltpu.VMEM((tt, tb), F32), pltpu.VMEM((tt, tb), F32)],
        compiler_params=_params(("arbitrary", "arbitrary", "arbitrary")),
        name="diff_attention",
    )(qx, qx, k, v, lamv, lin, gain)


def _lru_kernel(ux_ref, gg_ref, cw_ref, cb_ref, wg_ref, bg_ref, lam_ref, y_ref,
                a_f, b_f, a_b, b_b, *, n, cn, r):
    tt = n + cn
    w = LANES
    cw = cw_ref[0]
    cb = cb_ref[0]
    bg = bg_ref[0]
    neg_lam = -lam_ref[0]
    softplus = jnp.maximum(neg_lam, 0.0) + jnp.log(1.0 + jnp.exp(-jnp.abs(neg_lam)))
    row8 = lax.broadcasted_iota(I32, (r // SUBLANES, SUBLANES, w), 1)
    ext_rows = r + 2 * SUBLANES

    def local_scan(a, bb, reverse):
        a = a.reshape(r // SUBLANES, SUBLANES, w)
        bb = bb.reshape(r // SUBLANES, SUBLANES, w)
        for s in (1, 2, 4):
            if reverse:
                a_sh = pltpu.roll(a, SUBLANES - s, 1)
                b_sh = pltpu.roll(bb, SUBLANES - s, 1)
                valid = row8 < SUBLANES - s
            else:
                a_sh = pltpu.roll(a, s, 1)
                b_sh = pltpu.roll(bb, s, 1)
                valid = row8 >= s
            bb = jnp.where(valid, a * b_sh + bb, bb)
            a = jnp.where(valid, a * a_sh, a)
        return a.reshape(r, w), bb.reshape(r, w)

    def gates_chunk(c, carry):
        r0 = pl.multiple_of(c * r, r)
        seg_start = jnp.logical_or(r0 == 0, r0 == n)
        seg_end = jnp.logical_or(r0 + r == n, r0 + r == tt)
        main = ux_ref[0, pl.ds(r0, r), :]
        prev = ux_ref[0, pl.ds(pl.multiple_of(jnp.maximum(r0 - SUBLANES, 0), SUBLANES), SUBLANES), :]
        nxt = ux_ref[0, pl.ds(pl.multiple_of(jnp.minimum(r0 + r, tt - SUBLANES), SUBLANES), SUBLANES), :]
        prev = jnp.where(seg_start, 0.0, prev)
        nxt = jnp.where(seg_end, 0.0, nxt)
        ext = jnp.concatenate([prev, main, nxt], axis=0)
        u = cb
        for t in range(4):
            sh = (2 - t) % ext_rows
            win = ext if sh == 0 else pltpu.roll(ext, sh, 0)
            u = u + cw[t:t + 1, :] * win[SUBLANES:SUBLANES + r, :]
        zz = jnp.dot(u.astype(BF16), wg_ref[0], preferred_element_type=F32) + bg
        for dr, (a_s, b_s) in enumerate(((a_f, b_f), (a_b, b_b))):
            rg = 0.5 + 0.5 * jnp.tanh(0.5 * zz[:, (2 * dr) * w:(2 * dr + 1) * w])
            ig = 0.5 + 0.5 * jnp.tanh(0.5 * zz[:, (2 * dr + 1) * w:(2 * dr + 2) * w])
            a = jnp.exp(-LRU_C * rg * softplus[dr:dr + 1, :])
            bb = jnp.sqrt(1.0 - a * a) * ig * u
            a, bb = local_scan(a, bb, reverse=(dr == 1))
            a_s[pl.ds(r0, r), :] = a
            b_s[pl.ds(r0, r), :] = bb
        return carry

    lax.fori_loop(0, tt // r, gates_chunk, 0)

    def seg_scan(first_tile, ntiles, cf, cbk):
        def body(i, carry):
            cf, cbk = carry
            rf = pl.multiple_of((first_tile + i) * SUBLANES, SUBLANES)
            hf = b_f[pl.ds(rf, SUBLANES), :] + a_f[pl.ds(rf, SUBLANES), :] * cf
            b_f[pl.ds(rf, SUBLANES), :] = hf
            rb = pl.multiple_of((first_tile + ntiles - 1 - i) * SUBLANES, SUBLANES)
            hb = b_b[pl.ds(rb, SUBLANES), :] + a_b[pl.ds(rb, SUBLANES), :] * cbk
            b_b[pl.ds(rb, SUBLANES), :] = hb
            return hf[SUBLANES - 1:SUBLANES, :], hb[0:1, :]
        return lax.fori_loop(0, ntiles, body, (cf, cbk))

    zero = jnp.zeros((1, w), F32)
    cf, cbk = seg_scan(n // SUBLANES, cn // SUBLANES, zero, zero)
    seg_scan(0, n // SUBLANES, cf, cbk)

    def out_chunk(c, carry):
        r0 = pl.multiple_of(c * r, r)
        y = (b_f[pl.ds(r0, r), :] + b_b[pl.ds(r0, r), :]) * gg_ref[0, pl.ds(r0, r), :]
        y_ref[0, pl.ds(r0, r), :] = y.astype(BF16)
        return carry

    lax.fori_loop(0, tt // r, out_chunk, 0)


def _rglru(ux, gg, conv_w2, conv_b2, wgate, bgate, lam2, l, *, n, cn, tb):
    b, tt, _ = ux.shape
    kern = functools.partial(_lru_kernel, n=n, cn=cn, r=tb)
    half = lambda: pl.BlockSpec((1, tt, LANES), lambda i, hh: (i, 0, hh))
    return pl.pallas_call(
        kern,
        grid=(b, 2),
        in_specs=[
            half(), half(),
            pl.BlockSpec((1, 4, LANES), lambda i, hh: (2 * l + hh, 0, 0)),
            pl.BlockSpec((1, 1, LANES), lambda i, hh: (2 * l + hh, 0, 0)),
            pl.BlockSpec((1, LANES, 4 * LANES), lambda i, hh: (2 * l + hh, 0, 0)),
            pl.BlockSpec((1, 1, 4 * LANES), lambda i, hh: (2 * l + hh, 0, 0)),
            pl.BlockSpec((1, 2, LANES), lambda i, hh: (2 * l + hh, 0, 0)),
        ],
        out_specs=half(),
        out_shape=jax.ShapeDtypeStruct((b, tt, LRU_W), BF16),
        scratch_shapes=[pltpu.VMEM((tt, LANES), F32)] * 4,
        compiler_params=_params(("arbitrary", "arbitrary")),
        name="rglru",
    )(ux, gg, conv_w2, conv_b2, wgate, bgate, lam2)


def _dft_kernel(c_ref, s_ref, gc_ref, gs_ref, o_ref, acc_ref):
    kk = pl.program_id(2)

    def part():
        return (jnp.dot(c_ref[...], gc_ref[...], preferred_element_type=F32)
                + jnp.dot(s_ref[...], gs_ref[...], preferred_element_type=F32))

    @pl.when(kk == 0)
    def _():
        acc_ref[...] = part()

    @pl.when(kk > 0)
    def _():
        acc_ref[...] += part()

    @pl.when(kk == pl.num_programs(2) - 1)
    def _():
        o_ref[...] = acc_ref[...].astype(BF16)


def _position_dft(cmat, smat, gc, gs, *, row0):
    t = cmat.shape[0]
    nn = gc.shape[1]
    tm = _largest_divisor(t, (1024, 512, 256, 128))
    tk = _largest_divisor(t, (2048, 1024, 512, 256, 128))
    tn = _largest_divisor(nn, (512, 256))
    assert row0 % tk == 0
    return pl.pallas_call(
        _dft_kernel,
        grid=(t // tm, nn // tn, t // tk),
        in_specs=[
            pl.BlockSpec((tm, tk), lambda i, j, k: (i, k)),
            pl.BlockSpec((tm, tk), lambda i, j, k: (i, k)),
            pl.BlockSpec((tk, tn), lambda i, j, k: (row0 // tk + k, j)),
            pl.BlockSpec((tk, tn), lambda i, j, k: (row0 // tk + k, j)),
        ],
        out_specs=pl.BlockSpec((tm, tn), lambda i, j, k: (i, j)),
        out_shape=jax.ShapeDtypeStruct((t, nn), BF16),
        scratch_shapes=[pltpu.VMEM((tm, tn), F32)],
        compiler_params=_params(("arbitrary", "arbitrary", "arbitrary")),
        name="position_dft",
    )(cmat, smat, gc, gs)


def _out_kernel(attl_ref, attc_ref, y_ref, fl_ref, fc_ref, x_ref, mod_ref, w_ref, g_ref, b_ref, wr_ref,
                x1_ref, h2_ref, lg_ref, mix0_ref, mix1_ref, *, d, alpha, nbl, nblocks):
    j = pl.program_id(1)

    @pl.when(j == 0)
    def _():
        mix1_ref[...] = jnp.zeros_like(mix1_ref)

    def step(new_ref, old_ref):
        m = mod_ref[0]
        tb = x_ref.shape[1]
        is_ctx = jnp.minimum(j, nblocks - 1) >= nbl
        att = jnp.where(is_ctx, attc_ref[0], attl_ref[0])
        fy = jnp.where(is_ctx, fc_ref[...], fl_ref[...])
        for c in range(OUT_PIECES):
            rows = slice(c * tb // OUT_PIECES, (c + 1) * tb // OUT_PIECES)
            cols = slice(c * d // OUT_PIECES, (c + 1) * d // OUT_PIECES)
            x1 = _ln(alpha * x_ref[0, rows, :] + m[:, 2 * d:3 * d] * old_ref[rows, :]) * g_ref[0] + b_ref[0]
            x1_ref[0, rows, :] = x1
            h2_ref[0, rows, :] = (_ln(x1) * (1.0 + m[:, 4 * d:5 * d]) + m[:, 3 * d:4 * d]).astype(BF16)
            new_ref[:, cols] = (
                jnp.dot(att, w_ref[0, 0:ATT_W, cols], preferred_element_type=F32)
                + jnp.dot(y_ref[0], w_ref[0, ATT_W:ATT_W + LRU_W, cols], preferred_element_type=F32)
                + jnp.dot(fy, w_ref[0, ATT_W + LRU_W:, cols], preferred_element_type=F32))
        lg_ref[0] = lax.dot_general(wr_ref[0], h2_ref[0], (((1,), (1,)), ((), ())), preferred_element_type=F32)

    @pl.when(j % 2 == 0)
    def _():
        step(mix0_ref, mix1_ref)

    @pl.when(j % 2 == 1)
    def _():
        step(mix1_ref, mix0_ref)


def _out_proj(att_l, att_c, y, fy_l, fy_c, xc, mod2, w_out_bf, ln_g, ln_b, wr_t, l, *, nbl, tb, alpha, out_tokens):
    b, _, d = xc.shape
    nblocks = out_tokens // tb
    kern = functools.partial(_out_kernel, d=d, alpha=alpha, nbl=nbl, nblocks=nblocks)
    cur = lambda j: jnp.minimum(j, nblocks - 1)
    prv = lambda j: jnp.maximum(j - 1, 0)
    tok = lambda w: pl.BlockSpec((1, tb, w), lambda i, j: (i, prv(j), 0))
    return pl.pallas_call(
        kern,
        grid=(b, nblocks + 1),
        in_specs=[
            pl.BlockSpec((1, tb, ATT_W), lambda i, j: (i, jnp.minimum(cur(j), nbl - 1), 0)),
            pl.BlockSpec((1, tb, ATT_W), lambda i, j: (i, jnp.maximum(cur(j) - nbl, 0), 0)),
            pl.BlockSpec((1, tb, LRU_W), lambda i, j: (i, cur(j), 0)),
            pl.BlockSpec((tb, FNET_W), lambda i, j: (jnp.minimum(cur(j), nbl - 1), i)),
            pl.BlockSpec((tb, FNET_W), lambda i, j: (jnp.maximum(cur(j) - nbl, 0), i)),
            tok(d),
            pl.BlockSpec((1, 1, 6 * d), lambda i, j: (2 * i + (prv(j) >= nbl).astype(I32), 0, 0)),
            pl.BlockSpec((1, d, d), lambda i, j: (l, 0, 0)),
            pl.BlockSpec((1, 1, d), lambda i, j: (l, 0, 0)),
            pl.BlockSpec((1, 1, d), lambda i, j: (l, 0, 0)),
            pl.BlockSpec((1, N_EXPERTS, d), lambda i, j: (l, 0, 0)),
        ],
        out_specs=[tok(d), tok(d), pl.BlockSpec((1, N_EXPERTS, tb), lambda i, j: (i, 0, prv(j)))],
        out_shape=[
            jax.ShapeDtypeStruct((b, out_tokens, d), F32),
            jax.ShapeDtypeStruct((b, out_tokens, d), BF16),
            jax.ShapeDtypeStruct((b, N_EXPERTS, out_tokens), F32),
        ],
        scratch_shapes=[pltpu.VMEM((tb, d), F32), pltpu.VMEM((tb, d), F32)],
        compiler_params=_params(("arbitrary", "arbitrary")),
        name="out_proj",
    )(att_l, att_c, y, fy_l, fy_c, xc, mod2, w_out_bf, ln_g, ln_b, wr_t)


def _route_kernel(lg_ref, pos_ref, gate_ref, starts_ref, *, segments, n_slots):
    lg = lg_ref[0]
    lane_id = lax.broadcasted_iota(I32, (N_EXPERTS, LANES), 1)
    starts = jnp.zeros((N_EXPERTS, LANES), I32)
    chunks_per_tile = TOKEN_TILE // LANES
    e = jnp.exp(lg - jnp.max(lg, axis=0, keepdims=True))
    s = e / jnp.sum(e, axis=0, keepdims=True)
    ri = lax.broadcasted_iota(I32, (LANES, LANES), 0)
    ci = lax.broadcasted_iota(I32, (LANES, LANES), 1)
    strict_upper = jnp.where(ri < ci, 1.0, 0.0).astype(BF16)

    for lo, t, cap, base in segments:
        ss = s[:, lo:lo + t]
        bits = pltpu.bitcast(ss, I32)
        capf = float(cap)

        def search(i, thr, bits=bits, capf=capf):
            cand = thr | jnp.left_shift(jnp.int32(1), 30 - i)
            cnt = jnp.sum(jnp.where(bits >= cand, 1.0, 0.0), axis=1, keepdims=True)
            return jnp.where(cnt >= capf, cand, thr)

        thr = lax.fori_loop(0, 31, search, jnp.zeros((N_EXPERTS, 1), I32))
        need = capf - jnp.sum(jnp.where(bits > thr, 1.0, 0.0), axis=1, keepdims=True)
        off_eq = jnp.zeros((N_EXPERTS, 1), F32)
        off_sel = jnp.zeros((N_EXPERTS, 1), F32)
        for c in range(t // LANES):
            sl = slice(c * LANES, (c + 1) * LANES)
            tile, sub = divmod(lo // LANES + c, chunks_per_tile)
            if sub == 0:
                starts = jnp.where(lane_id == tile, off_sel.astype(I32) + base, starts)
            bits_c = bits[:, sl]
            eq = bits_c == thr
            eq_c = jnp.where(eq, 1.0, 0.0)
            rank_eq = jnp.dot(eq_c.astype(BF16), strict_upper, preferred_element_type=F32) + off_eq
            off_eq = off_eq + jnp.sum(eq_c, axis=1, keepdims=True)
            sel = jnp.logical_or(bits_c > thr, jnp.logical_and(eq, rank_eq < need))
            sel_c = jnp.where(sel, 1.0, 0.0)
            slot = jnp.dot(sel_c.astype(BF16), strict_upper, preferred_element_type=F32) + off_sel
            off_sel = off_sel + jnp.sum(sel_c, axis=1, keepdims=True)
            osl = slice(sub * LANES, (sub + 1) * LANES)
            pos_ref[0, tile, :, osl] = jnp.where(sel, slot.astype(I32) + base, -1)
            gate_ref[0, tile, :, osl] = jnp.where(sel, ss[:, sl], 0.0)

    n_tiles = lg.shape[1] // TOKEN_TILE
    starts_ref[0] = jnp.where(lane_id == n_tiles, n_slots, starts)


def _route(logits_t, *, segments, n_slots):
    b, ne, tt = logits_t.shape
    nt = tt // TOKEN_TILE
    assert nt < LANES and all(lo % TOKEN_TILE == 0 and t % TOKEN_TILE == 0 for lo, t, _, _ in segments)
    kern = functools.partial(_route_kernel, segments=segments, n_slots=n_slots)
    tiled = lambda: pl.BlockSpec((1, nt, ne, TOKEN_TILE), lambda i: (i, 0, 0, 0))
    return pl.pallas_call(
        kern,
        grid=(b,),
        in_specs=[pl.BlockSpec((1, ne, tt), lambda i: (i, 0, 0))],
        out_specs=[tiled(), tiled(), pl.BlockSpec((1, ne, LANES), lambda i: (i, 0, 0))],
        out_shape=[jax.ShapeDtypeStruct((b, nt, ne, TOKEN_TILE), I32),
                   jax.ShapeDtypeStruct((b, nt, ne, TOKEN_TILE), F32),
                   jax.ShapeDtypeStruct((b, ne, LANES), I32)],
        compiler_params=_params(("arbitrary",)),
        name="route",
    )(logits_t)


def _window_start(starts_ref, base_idx, tile, w, win, limit):
    c0 = starts_ref[base_idx + tile]
    lo = ((c0 >> 4) << 4) + w * win
    return lo, pl.multiple_of(jnp.minimum(lo, limit), BF16_ROWS)


def _window_count(starts_ref, base_idx, tiles, win):
    nw = jnp.int32(1)
    for tile in tiles:
        c0 = starts_ref[base_idx + tile]
        c1 = starts_ref[base_idx + tile + 1]
        nw = jnp.maximum(nw, (c1 - ((c0 >> 4) << 4) + win - 1) // win)
    return nw


def _gather_kernel(starts_ref, h_ref, pos_ref, xs_ref, acc_ref, *, slots, win, group):
    b = pl.program_id(0)
    eg = pl.program_id(1)
    nt = pos_ref.shape[1]
    acc_ref[...] = jnp.zeros_like(acc_ref)
    rel = lax.broadcasted_iota(I32, (win, TOKEN_TILE), 0)
    bases = [(b * N_EXPERTS + eg * group + i) * LANES for i in range(group)]
    nw = jnp.int32(1)
    for i in range(group):
        nw = jnp.maximum(nw, _window_count(starts_ref, bases[i], range(nt), win))

    def window_pass(w, carry):
        for tile in range(nt):
            onehots, offs = [], []
            for i in range(group):
                lo, start = _window_start(starts_ref, bases[i], tile, w, win, slots)
                prow = pos_ref[0, tile, pl.ds(eg * group + i, 1), :] - start
                onehots.append(jnp.where(rel == prow, 1.0, 0.0).astype(BF16))
                offs.append(start)
            res = jnp.dot(jnp.concatenate(onehots, axis=0), h_ref[0, tile * TOKEN_TILE:(tile + 1) * TOKEN_TILE, :],
                          preferred_element_type=F32)
            for i in range(group):
                acc_ref[i, pl.ds(offs[i], win), :] += res[i * win:(i + 1) * win].astype(BF16)
        return carry

    lax.fori_loop(0, nw, window_pass, 0)
    xs_ref[...] = acc_ref[:, 0:slots, :]


def _gather(starts, h2, pos, *, slots, win, group=8):
    b, tt, d = h2.shape
    nt = tt // TOKEN_TILE
    kern = functools.partial(_gather_kernel, slots=slots, win=win, group=group)
    return pl.pallas_call(
        kern,
        grid_spec=pltpu.PrefetchScalarGridSpec(
            num_scalar_prefetch=1,
            grid=(b, N_EXPERTS // group),
            in_specs=[
                pl.BlockSpec((1, tt, d), lambda i, g, s: (i, 0, 0)),
                pl.BlockSpec((1, nt, N_EXPERTS, TOKEN_TILE), lambda i, g, s: (i, 0, 0, 0)),
            ],
            out_specs=pl.BlockSpec((group, slots, d), lambda i, g, s: (g, i, 0)),
            scratch_shapes=[pltpu.VMEM((group, slots + win, d), BF16)],
        ),
        out_shape=jax.ShapeDtypeStruct((N_EXPERTS, b * slots, d), BF16),
        compiler_params=_params(("arbitrary", "arbitrary")),
        name="moe_gather",
    )(starts, h2, pos)


def _ffn_kernel(xs_ref, wg_ref, wu_ref, wd_ref, ys_ref, *, fchunk):
    xs = xs_ref[0]
    f = wg_ref.shape[-1]
    acc = None
    for c in range(f // fchunk):
        sl = slice(c * fchunk, (c + 1) * fchunk)
        a = jnp.dot(xs, wg_ref[0, 0, :, sl], preferred_element_type=F32)
        u = jnp.dot(xs, wu_ref[0, 0, :, sl], preferred_element_type=F32)
        hm = (a * jax.nn.sigmoid(a) * u).astype(BF16)
        y = jnp.dot(hm, wd_ref[0, 0, sl, :], preferred_element_type=F32)
        acc = y if acc is None else acc + y
    ys_ref[0] = acc.astype(BF16)


def _expert_ffn(xs, wg, wu, wd, l, *, slots):
    ne, rows, d = xs.shape
    f = wg.shape[-1]
    nb = rows // slots
    kern = functools.partial(_ffn_kernel, fchunk=_largest_divisor(f, (512,)))
    return pl.pallas_call(
        kern,
        grid=(ne, nb),
        in_specs=[
            pl.BlockSpec((1, slots, d), lambda e, i: (e, i, 0)),
            pl.BlockSpec((1, 1, d, f), lambda e, i: (l, e, 0, 0)),
            pl.BlockSpec((1, 1, d, f), lambda e, i: (l, e, 0, 0)),
            pl.BlockSpec((1, 1, f, d), lambda e, i: (l, e, 0, 0)),
        ],
        out_specs=pl.BlockSpec((1, slots, d), lambda e, i: (e, i, 0)),
        out_shape=jax.ShapeDtypeStruct((ne, rows, d), BF16),
        compiler_params=_params(("arbitrary", "arbitrary")),
        name="expert_ffn",
    )(xs, wg, wu, wd)


def _combine_kernel(starts_ref, ys_ref, pos_ref, gate_ref, x1_ref, mod_ref, g_ref, b_ref, o_ref,
                    *, slots, win, d, alpha):
    b = pl.program_id(0)
    tile = pl.program_id(1)
    rel = lax.broadcasted_iota(I32, (win, TOKEN_TILE), 0)
    bases = [(b * N_EXPERTS + e) * LANES for e in range(N_EXPERTS)]
    nw = jnp.int32(1)
    for e in range(N_EXPERTS):
        c0 = starts_ref[bases[e] + tile]
        c1 = starts_ref[bases[e] + tile + 1]
        nw = jnp.maximum(nw, (c1 - ((c0 >> 4) << 4) + win - 1) // win)

    def window_pass(w, moe):
        gated, rows = [], []
        for e in range(N_EXPERTS):
            lo, start = _window_start(starts_ref, bases[e], tile, w, win, slots - win)
            prow = pos_ref[0, 0, e:e + 1, :]
            prow = jnp.where(jnp.logical_and(prow >= lo, prow < lo + win), prow - start, -1)
            gated.append(jnp.where(rel == prow, gate_ref[0, 0, e:e + 1, :], 0.0).astype(BF16))
            rows.append(ys_ref[e, pl.ds(start, win), :])
        return moe + lax.dot_general(jnp.concatenate(gated, axis=0), jnp.concatenate(rows, axis=0),
                                     (((0,), (0,)), ((), ())), preferred_element_type=F32)

    moe = lax.fori_loop(0, nw, window_pass, jnp.zeros((TOKEN_TILE, d), F32))
    m = mod_ref[0]
    o_ref[0] = _ln(alpha * x1_ref[0] + m[:, 5 * d:6 * d] * moe) * g_ref[0] + b_ref[0]


def _combine(starts, ys, pos, gate, x1, mod2, ln_g, ln_b, l, *, slots, win, n_lat_tiles, alpha, out_tokens):
    b, _, d = x1.shape
    kern = functools.partial(_combine_kernel, slots=slots, win=win, d=d, alpha=alpha)
    tiled = lambda: pl.BlockSpec((1, 1, N_EXPERTS, TOKEN_TILE), lambda i, t, s: (i, t, 0, 0))
    tok = lambda: pl.BlockSpec((1, TOKEN_TILE, d), lambda i, t, s: (i, t, 0))
    return pl.pallas_call(
        kern,
        grid_spec=pltpu.PrefetchScalarGridSpec(
            num_scalar_prefetch=1,
            grid=(b, out_tokens // TOKEN_TILE),
            in_specs=[
                pl.BlockSpec((N_EXPERTS, slots, d), lambda i, t, s: (0, i, 0)),
                tiled(), tiled(), tok(),
                pl.BlockSpec((1, 1, 6 * d), lambda i, t, s: (2 * i + (t >= n_lat_tiles).astype(I32), 0, 0)),
                pl.BlockSpec((1, 1, d), lambda i, t, s: (l, 0, 0)),
                pl.BlockSpec((1, 1, d), lambda i, t, s: (l, 0, 0)),
            ],
            out_specs=tok(),
        ),
        out_shape=jax.ShapeDtypeStruct((b, out_tokens, d), F32),
        compiler_params=_params(("arbitrary", "arbitrary")),
        name="moe_combine",
    )(starts, ys, pos, gate, x1, mod2, ln_g, ln_b)


def _rope_tables(n, cn):
    lane = np.arange(LANES)
    within = lane % QK_DIM
    use_col = (within // 32) == 1
    first_half = (within % 32) < 16
    inv = ROPE_BASE ** (-(within % 16).astype(np.float64) / 16.0)
    pos = np.arange(n)
    coord = np.where(use_col[None, :], (pos % GRID_W)[:, None], (pos // GRID_W)[:, None]).astype(np.float32)
    ang = jnp.asarray(coord) * jnp.asarray(inv.astype(np.float32))[None, :]
    cos = jnp.cos(ang)
    sin = jnp.where(jnp.asarray(first_half)[None, :], -jnp.sin(ang), jnp.sin(ang))
    cos = jnp.concatenate([cos, jnp.ones((cn, LANES), F32)], axis=0)
    sin = jnp.concatenate([sin, jnp.zeros((cn, LANES), F32)], axis=0)
    return cos, sin


def _channel_dft():
    idx = np.arange(FNET_W)
    same = (idx[:, None] // FNET_GROUP_W) == (idx[None, :] // FNET_GROUP_W)
    ang = 2.0 * np.pi * ((idx[:, None] % FNET_GROUP_W) * (idx[None, :] % FNET_GROUP_W) % FNET_GROUP_W) / FNET_GROUP_W
    cs = np.concatenate([np.where(same, np.cos(ang), 0.0), np.where(same, np.sin(ang), 0.0)], axis=1)
    return jnp.asarray(cs.astype(np.float32)).astype(BF16)


def _position_dft_mats(t):
    g = math.gcd(t, 64)
    kk = jnp.arange(t, dtype=I32)

    def table(m):
        ph = ((kk[:, None] * m[None, :]) % t).astype(F32) * (2.0 * math.pi / t)
        return jnp.cos(ph), jnp.sin(ph)

    ch, sh = table(jnp.arange(t // g, dtype=I32) * g)
    cl, sl = table(jnp.arange(g, dtype=I32))
    scale = 1.0 / math.sqrt(t * FNET_GROUP_W)
    cmat = (ch[:, :, None] * cl[:, None, :] - sh[:, :, None] * sl[:, None, :]).reshape(t, t) * scale
    smat = (sh[:, :, None] * cl[:, None, :] + ch[:, :, None] * sl[:, None, :]).reshape(t, t) * (-scale)
    return cmat.astype(BF16), smat.astype(BF16)


def _block_diag_gates(wa, wx):
    depth = wa.shape[0]

    def dense(wb):
        eye = jnp.eye(LRU_BLOCKS, dtype=wb.dtype)
        return jnp.einsum('lncd,nm->lncmd', wb, eye).reshape(depth, LRU_W, LRU_W)

    halves = []
    for hh in range(2):
        sl = slice(hh * LANES, (hh + 1) * LANES)
        cols = [dense(wmat[:, dr])[:, sl, sl] for dr in range(2) for wmat in (wa, wx)]
        halves.append(jnp.concatenate(cols, axis=-1))
    return jnp.stack(halves, axis=1).reshape(depth * 2, LANES, 4 * LANES).astype(BF16)


def _gate_bias(ba, bx):
    depth = ba.shape[0]
    halves = []
    for hh in range(2):
        sl = slice(hh * LANES, (hh + 1) * LANES)
        halves.append(jnp.concatenate([bvec[:, dr, sl] for dr in range(2) for bvec in (ba, bx)], axis=-1))
    return jnp.stack(halves, axis=1).reshape(depth * 2, 1, 4 * LANES)


def _split_halves(a):
    depth, r, _ = a.shape
    return a.reshape(depth, r, 2, LANES).transpose(0, 2, 1, 3).reshape(depth * 2, r, LANES)


def kernel(x, c, ctx, c_ctx, w_mod, b_mod, w_in, lam_q1, lam_k1, lam_q2, lam_k2, attn_norm_g, conv_w, conv_b, lru_wa, lru_ba, lru_wx, lru_bx, lru_lam, w_out, ln1_g, ln1_b, w_router, w_gate, w_up, w_down, ln2_g, ln2_b):
    b, n, d = x.shape
    cn = ctx.shape[1]
    depth = w_mod.shape[0]
    tt = n + cn
    tb = _largest_divisor(math.gcd(n, cn), (256, 128))
    assert n % tb == 0 and cn % tb == 0 and n % GRID_W == 0 and tt % LANES == 0
    nbl = n // tb
    cap_l = CAPACITY_FACTOR * n // N_EXPERTS
    cap_c = CAPACITY_FACTOR * cn // N_EXPERTS
    assert cap_l % BF16_ROWS == 0 and cap_c % BF16_ROWS == 0 and n % TOKEN_TILE == 0 and cn % TOKEN_TILE == 0
    alpha = (2 * depth) ** 0.25

    rows = -(-(b + 1) // SUBLANES) * SUBLANES
    cc = jnp.concatenate([c, c_ctx[None, :], jnp.zeros((rows - b - 1, d), F32)], axis=0)
    mod = _modulation(cc, w_mod, b_mod)

    cos_t, sin_t = _rope_tables(n, cn)
    cs = _channel_dft()
    dft_l = _position_dft_mats(n)
    dft_c = _position_dft_mats(cn)

    w_in_bf = w_in.astype(BF16)
    w_out_bf = w_out.astype(BF16)
    wg_bf = w_gate.astype(BF16)
    wu_bf = w_up.astype(BF16)
    wd_bf = w_down.astype(BF16)
    wr_t = jnp.swapaxes(w_router, 1, 2).astype(BF16)
    lamv = jnp.stack([lam_q1, lam_k1, lam_q2, lam_k2], axis=1).astype(F32)
    lam_init = np.array([0.8 - 0.6 * math.exp(-0.3 * l) for l in range(depth)], np.float32)
    lin = jnp.asarray(np.broadcast_to(lam_init[:, None, None], (depth, 1, LANES)).copy())
    gain = attn_norm_g.reshape(depth * N_HEADS, 1, HEAD_V)
    conv_w2 = _split_halves(conv_w)
    conv_b2 = _split_halves(conv_b[:, None, :])
    lam2 = _split_halves(lru_lam)
    wgate = _block_diag_gates(lru_wa, lru_wx)
    bgate = _gate_bias(lru_ba, lru_bx)
    ln1g, ln1b = ln1_g[:, None, :], ln1_b[:, None, :]
    ln2g, ln2b = ln2_g[:, None, :], ln2_b[:, None, :]

    xc = jnp.concatenate([x, ctx], axis=1)
    for l in range(depth):
        ml = mod[l]
        mod2 = jnp.stack([ml[:b], jnp.broadcast_to(ml[b][None, :], (b, 6 * d))], axis=1).reshape(2 * b, 1, 6 * d)
        qx, k, v, ux, gg, gc, gs = _in_proj(xc, mod2, w_in_bf, l, cos_t, sin_t, cs, nbl=nbl, tb=tb)
        keep_ctx = l < depth - 1
        tokens = tt if keep_ctx else n
        att_l, att_c = _attention(qx, k, v, lamv, lin, gain, l, n=n, tb=tb)
        y = _rglru(ux, gg, conv_w2, conv_b2, wgate, bgate, lam2, l, n=n, cn=cn, tb=tb)
        fy_l = _position_dft(*dft_l, gc, gs, row0=0)
        fy_c = _position_dft(*dft_c, gc, gs, row0=n) if keep_ctx else fy_l
        x1, h2, logits_t = _out_proj(att_l, att_c, y, fy_l, fy_c, xc, mod2, w_out_bf, ln1g, ln1b, wr_t, l,
                                     nbl=nbl, tb=tb, alpha=alpha, out_tokens=tokens)
        segments = ((0, n, cap_l, 0), (n, cn, cap_c, cap_l)) if keep_ctx else ((0, n, cap_l, 0),)
        slots = cap_l + cap_c if keep_ctx else cap_l
        win = min(SLOT_WINDOW, slots)
        pos, gate, starts = _route(logits_t, segments=segments, n_slots=slots)
        starts = starts.reshape(-1)
        xs = _gather(starts, h2, pos, slots=slots, win=win)
        ys = _expert_ffn(xs, wg_bf, wu_bf, wd_bf, l, slots=slots)
        xc = _combine(starts, ys, pos, gate, x1, mod2, ln2g, ln2b, l, slots=slots, win=win,
                      n_lat_tiles=n // TOKEN_TILE, alpha=alpha, out_tokens=tokens)
    return xc
```

```python
import functools
import math

import numpy as np
import jax
import jax.numpy as jnp
from jax import lax
from jax.experimental import pallas as pl
from jax.experimental.pallas import tpu as pltpu

F32 = jnp.float32
BF16 = jnp.bfloat16
I32 = jnp.int32

GRID_W = 64
QK_DIM = 64
N_HEADS = 4
HEAD_V = 128
QK_W = 512
ATT_W = 512
LRU_W = 256
LRU_BLOCKS = 4
LRU_BLOCK_W = 64
LRU_C = 8.0
FNET_W = 256
FNET_GROUP_W = 64
IN_W = 2304
ROPE_BASE = 10000.0
N_EXPERTS = 16
CAPACITY_FACTOR = 2
LN_EPS = 1e-5
RMS_EPS = 1e-6
GELU_C = math.sqrt(2.0 / math.pi)
LOG2E = math.log2(math.e)

LANES = 128
SUBLANES = 8
BF16_ROWS = 16
KEY_CHUNK = 256
OUT_PIECES = 4
SCAN_UNROLL = 4
TOKEN_TILE = 256
SLOT_WINDOW = 64
VMEM_LIMIT = 56 << 20


def _params(sem, vmem=VMEM_LIMIT):
    return pltpu.CompilerParams(dimension_semantics=sem, vmem_limit_bytes=vmem)


def _ln(x):
    mu = jnp.mean(x, axis=-1, keepdims=True)
    xc = x - mu
    var = jnp.mean(xc * xc, axis=-1, keepdims=True)
    return xc * lax.rsqrt(var + LN_EPS)


def _largest_divisor(n, candidates):
    for c in candidates:
        if c <= n and n % c == 0:
            return c
    return n


def _mod_kernel(c_ref, w_ref, b_ref, o_ref):
    c = c_ref[...]
    s = c * jax.nn.sigmoid(c)
    o_ref[0] = jnp.dot(s, w_ref[0], precision=lax.Precision.HIGHEST, preferred_element_type=F32) + b_ref[0]


def _modulation(cc, w_mod, b_mod):
    depth, d, d6 = w_mod.shape
    rows = cc.shape[0]
    tn = 1024
    return pl.pallas_call(
        _mod_kernel,
        grid=(depth, d6 // tn),
        in_specs=[
            pl.BlockSpec((rows, d), lambda l, n: (0, 0)),
            pl.BlockSpec((1, d, tn), lambda l, n: (l, 0, n)),
            pl.BlockSpec((1, 1, tn), lambda l, n: (l, 0, n)),
        ],
        out_specs=pl.BlockSpec((1, rows, tn), lambda l, n: (l, 0, n)),
        out_shape=jax.ShapeDtypeStruct((depth, rows, d6), F32),
        compiler_params=_params(("arbitrary", "arbitrary")),
        name="modulation",
    )(cc, w_mod, b_mod.reshape(depth, 1, d6))


def _in_kernel(x_ref, mod_ref, w_ref, cos_ref, sin_ref, cs_ref,
               qx_ref, k_ref, v_ref, ux_ref, gg_ref, gc_ref, gs_ref, *, d):
    x = x_ref[0]
    m = mod_ref[0]
    h = _ln(x) * (1.0 + m[:, d:2 * d]) + m[:, 0:d]
    z = jnp.dot(h.astype(BF16), w_ref[0], preferred_element_type=F32)

    tb = x.shape[0]
    cos = cos_ref[...]
    sin = sin_ref[...]
    lane = lax.broadcasted_iota(I32, (tb, LANES), 1)
    first_half = (lane & 31) < 16
    low_map = lane < QK_DIM

    def rope(t):
        partner = jnp.where(first_half, pltpu.roll(t, LANES - 16, 1), pltpu.roll(t, 16, 1))
        return t * cos + partner * sin

    for p in range(N_HEADS):
        qp = rope(z[:, p * LANES:(p + 1) * LANES] * (QK_DIM ** -0.5 * LOG2E))
        qx_ref[0, :, (2 * p) * LANES:(2 * p + 1) * LANES] = jnp.where(low_map, qp, 0.0).astype(BF16)
        qx_ref[0, :, (2 * p + 1) * LANES:(2 * p + 2) * LANES] = jnp.where(low_map, 0.0, qp).astype(BF16)
        kp = rope(z[:, QK_W + p * LANES:QK_W + (p + 1) * LANES])
        k_ref[0, :, p * LANES:(p + 1) * LANES] = kp.astype(BF16)

    o = 2 * QK_W
    v_ref[0] = z[:, o:o + ATT_W].astype(BF16)
    o += ATT_W
    ux_ref[0] = z[:, o:o + LRU_W]
    o += LRU_W
    g = z[:, o:o + LRU_W]
    gg_ref[0] = 0.5 * g * (1.0 + jnp.tanh(GELU_C * (g + 0.044715 * (g * g * g))))
    o += LRU_W
    uf = z[:, o:o + FNET_W].astype(BF16)
    gcs = jnp.dot(uf, cs_ref[...], preferred_element_type=F32)
    gc_ref[...] = gcs[:, :FNET_W].astype(BF16)
    gs_ref[...] = gcs[:, FNET_W:].astype(BF16)


def _in_proj(xc, mod2, w_in_bf, l, cos_t, sin_t, cs, *, nbl, tb):
    b, tt, d = xc.shape
    nbt = tt // tb
    kern = functools.partial(_in_kernel, d=d)
    tok = lambda w: pl.BlockSpec((1, tb, w), lambda i, j: (i, j, 0))
    return pl.pallas_call(
        kern,
        grid=(b, nbt),
        in_specs=[
            tok(d),
            pl.BlockSpec((1, 1, 6 * d), lambda i, j: (2 * i + (j >= nbl).astype(I32), 0, 0)),
            pl.BlockSpec((1, d, IN_W), lambda i, j: (l, 0, 0)),
            pl.BlockSpec((tb, LANES), lambda i, j: (j, 0)),
            pl.BlockSpec((tb, LANES), lambda i, j: (j, 0)),
            pl.BlockSpec((FNET_W, 2 * FNET_W), lambda i, j: (0, 0)),
        ],
        out_specs=[
            tok(2 * QK_W), tok(QK_W), tok(ATT_W), tok(LRU_W), tok(LRU_W),
            pl.BlockSpec((tb, FNET_W), lambda i, j: (j, i)),
            pl.BlockSpec((tb, FNET_W), lambda i, j: (j, i)),
        ],
        out_shape=[
            jax.ShapeDtypeStruct((b, tt, 2 * QK_W), BF16),
            jax.ShapeDtypeStruct((b, tt, QK_W), BF16),
            jax.ShapeDtypeStruct((b, tt, ATT_W), BF16),
            jax.ShapeDtypeStruct((b, tt, LRU_W), F32),
            jax.ShapeDtypeStruct((b, tt, LRU_W), F32),
            jax.ShapeDtypeStruct((tt, b * FNET_W), BF16),
            jax.ShapeDtypeStruct((tt, b * FNET_W), BF16),
        ],
        compiler_params=_params(("arbitrary", "arbitrary")),
        name="in_proj",
    )(xc, mod2, w_in_bf, cos_t, sin_t, cs)


def _attn_kernel(ql_ref, qc_ref, k_ref, v_ref, lamv_ref, lin_ref, g_ref, ol_ref, oc_ref, vt_ref, s0_ref, s1_ref,
                 *, n, lat_steps, qb):
    j = pl.program_id(2)
    lv = lamv_ref[0]
    lam_init = lin_ref[0][:, 0:1]
    lam = (jnp.exp(jnp.sum(lv[0:1] * lv[1:2], axis=1, keepdims=True))
           - jnp.exp(jnp.sum(lv[2:3] * lv[3:4], axis=1, keepdims=True)) + lam_init)
    gain = g_ref[0] * (1.0 - lam_init)
    nt = (((1,), (1,)), ((), ()))
    tt = k_ref.shape[1]

    @pl.when(j == 0)
    def _():
        vt_ref[0:HEAD_V, :] = v_ref[0].astype(F32).T.astype(BF16)
        row = lax.broadcasted_iota(I32, (BF16_ROWS, tt), 0)
        vt_ref[HEAD_V:, :] = jnp.where(row == 0, 1.0, 0.0).astype(BF16)

    def attend(q_ref, o_ref, blocks, lo):
        chunks = [(c, min(KEY_CHUNK, tt - c)) for c in range(lo, tt, KEY_CHUNK)]
        tq = s0_ref.shape[1]
        streams = [(blk, mp) for blk in range(blocks) for mp in range(2)]
        s_refs = (s0_ref, s1_ref)

        def scores(i, c, w, m):
            blk, mp = streams[i]
            q = q_ref[0, blk * tq:(blk + 1) * tq, mp * LANES:(mp + 1) * LANES]
            st = lax.dot_general(k_ref[0, c:c + w, :], q, nt, preferred_element_type=F32)
            s_refs[i % 2][c:c + w, :] = st
            mc = jnp.max(st, axis=0, keepdims=True)
            return mc if m is None else jnp.maximum(m, mc)

        def weighted_values(i, c, w, m, acc):
            pt = jnp.exp2(s_refs[i % 2][c:c + w, :] - m).astype(BF16)
            part = jnp.dot(vt_ref[:, c:c + w], pt, preferred_element_type=F32)
            return part if acc is None else acc + part

        ms = [None] * len(streams)
        accs = [None] * len(streams)
        for stage in range(len(streams) + 1):
            for c, w in chunks:
                if stage >= 1:
                    accs[stage - 1] = weighted_values(stage - 1, c, w, ms[stage - 1], accs[stage - 1])
                if stage < len(streams):
                    ms[stage] = scores(stage, c, w, ms[stage])
            if stage >= 2 and stage % 2 == 0:
                blk = stage // 2 - 1
                a0, a1 = accs[stage - 2], accs[stage - 1]
                ot = (a0[0:HEAD_V] / a0[HEAD_V:HEAD_V + 1]) - lam * (a1[0:HEAD_V] / a1[HEAD_V:HEAD_V + 1])
                rt = ot * lax.rsqrt(jnp.mean(ot * ot, axis=0, keepdims=True) + RMS_EPS)
                o_ref[0, blk * tq:(blk + 1) * tq, :] = (rt.T * gain).astype(BF16)

    @pl.when(j < lat_steps)
    def _():
        attend(ql_ref, ol_ref, qb, 0)

    @pl.when(j >= lat_steps)
    def _():
        attend(qc_ref, oc_ref, qc_ref.shape[1] // s0_ref.shape[1], n)


def _attention(qx, k, v, lamv, lin, gain, l, *, n, tb):
    b, tt, _ = k.shape
    cn = tt - n
    qb = _largest_divisor(n // tb, (16, 8, 4, 2, 1))
    lat_steps = n // (qb * tb)
    assert n % cn == 0 and cn % tb == 0
    kern = functools.partial(_attn_kernel, n=n, lat_steps=lat_steps, qb=qb)
    lat = lambda w: pl.BlockSpec((1, qb * tb, w), lambda i, h, j: (i, jnp.minimum(j, lat_steps - 1), h))
    ctx = lambda w: pl.BlockSpec((1, cn, w), lambda i, h, j: (i, n // cn, h))
    return pl.pallas_call(
        kern,
        grid=(b, N_HEADS, lat_steps + 1),
        in_specs=[
            lat(2 * LANES), ctx(2 * LANES),
            pl.BlockSpec((1, tt, LANES), lambda i, h, j: (i, 0, h)),
            pl.BlockSpec((1, tt, LANES), lambda i, h, j: (i, 0, h)),
            pl.BlockSpec((1, 4, QK_DIM), lambda i, h, j: (l, 0, 0)),
            pl.BlockSpec((1, 1, LANES), lambda i, h, j: (l, 0, 0)),
            pl.BlockSpec((1, 1, HEAD_V), lambda i, h, j: (l * N_HEADS + h, 0, 0)),
        ],
        out_specs=[lat(LANES), pl.BlockSpec((1, cn, LANES), lambda i, h, j: (i, 0, h))],
        out_shape=[jax.ShapeDtypeStruct((b, n, ATT_W), BF16), jax.ShapeDtypeStruct((b, cn, ATT_W), BF16)],
        scratch_shapes=[pltpu.VMEM((HEAD_V + BF16_ROWS, tt), BF16),
                        pltpu.VMEM((tt, tb), F32), pltpu.VMEM((tt, tb), F32)],
        compiler_params=_params(("arbitrary", "arbitrary", "arbitrary")),
        name="diff_attention",
    )(qx, qx, k, v, lamv, lin, gain)


def _lru_kernel(ux_ref, gg_ref, cw_ref, cb_ref, wg_ref, bg_ref, lam_ref, y_ref,
                a_f, b_f, a_b, b_b, *, n, cn, r):
    tt = n + cn
    w = LANES
    cw = cw_ref[0]
    cb = cb_ref[0]
    bg = bg_ref[0]
    neg_lam = -lam_ref[0]
    softplus = jnp.maximum(neg_lam, 0.0) + jnp.log(1.0 + jnp.exp(-jnp.abs(neg_lam)))
    row8 = lax.broadcasted_iota(I32, (r // SUBLANES, SUBLANES, w), 1)
    ext_rows = r + 2 * SUBLANES

    def local_scan(a, bb, reverse):
        a = a.reshape(r // SUBLANES, SUBLANES, w)
        bb = bb.reshape(r // SUBLANES, SUBLANES, w)
        for s in (1, 2, 4):
            if reverse:
                a_sh = pltpu.roll(a, SUBLANES - s, 1)
                b_sh = pltpu.roll(bb, SUBLANES - s, 1)
                valid = row8 < SUBLANES - s
            else:
                a_sh = pltpu.roll(a, s, 1)
                b_sh = pltpu.roll(bb, s, 1)
                valid = row8 >= s
            bb = jnp.where(valid, a * b_sh + bb, bb)
            a = jnp.where(valid, a * a_sh, a)
        return a.reshape(r, w), bb.reshape(r, w)

    def gates_chunk(c, carry):
        r0 = pl.multiple_of(c * r, r)
        seg_start = jnp.logical_or(r0 == 0, r0 == n)
        seg_end = jnp.logical_or(r0 + r == n, r0 + r == tt)
        main = ux_ref[0, pl.ds(r0, r), :]
        prev = ux_ref[0, pl.ds(pl.multiple_of(jnp.maximum(r0 - SUBLANES, 0), SUBLANES), SUBLANES), :]
        nxt = ux_ref[0, pl.ds(pl.multiple_of(jnp.minimum(r0 + r, tt - SUBLANES), SUBLANES), SUBLANES), :]
        prev = jnp.where(seg_start, 0.0, prev)
        nxt = jnp.where(seg_end, 0.0, nxt)
        ext = jnp.concatenate([prev, main, nxt], axis=0)
        u = cb
        for t in range(4):
            sh = (2 - t) % ext_rows
            win = ext if sh == 0 else pltpu.roll(ext, sh, 0)
            u = u + cw[t:t + 1, :] * win[SUBLANES:SUBLANES + r, :]
        zz = jnp.dot(u.astype(BF16), wg_ref[0], preferred_element_type=F32) + bg
        for dr, (a_s, b_s) in enumerate(((a_f, b_f), (a_b, b_b))):
            rg = 0.5 + 0.5 * jnp.tanh(0.5 * zz[:, (2 * dr) * w:(2 * dr + 1) * w])
            ig = 0.5 + 0.5 * jnp.tanh(0.5 * zz[:, (2 * dr + 1) * w:(2 * dr + 2) * w])
            a = jnp.exp(-LRU_C * rg * softplus[dr:dr + 1, :])
            bb = jnp.sqrt(1.0 - a * a) * ig * u
            a, bb = local_scan(a, bb, reverse=(dr == 1))
            a_s[pl.ds(r0, r), :] = a
            b_s[pl.ds(r0, r), :] = bb
        return carry

    lax.fori_loop(0, tt // r, gates_chunk, 0)

    def seg_scan(first_tile, ntiles, cf, cbk):
        def body(i, carry):
            cf, cbk = carry
            rf = pl.multiple_of((first_tile + i) * SUBLANES, SUBLANES)
            hf = b_f[pl.ds(rf, SUBLANES), :] + a_f[pl.ds(rf, SUBLANES), :] * cf
            b_f[pl.ds(rf, SUBLANES), :] = hf
            rb = pl.multiple_of((first_tile + ntiles - 1 - i) * SUBLANES, SUBLANES)
            hb = b_b[pl.ds(rb, SUBLANES), :] + a_b[pl.ds(rb, SUBLANES), :] * cbk
            b_b[pl.ds(rb, SUBLANES), :] = hb
            return hf[SUBLANES - 1:SUBLANES, :], hb[0:1, :]
        return lax.fori_loop(0, ntiles, body, (cf, cbk), unroll=SCAN_UNROLL)

    zero = jnp.zeros((1, w), F32)
    cf, cbk = seg_scan(n // SUBLANES, cn // SUBLANES, zero, zero)
    seg_scan(0, n // SUBLANES, cf, cbk)

    def out_chunk(c, carry):
        r0 = pl.multiple_of(c * r, r)
        y = (b_f[pl.ds(r0, r), :] + b_b[pl.ds(r0, r), :]) * gg_ref[0, pl.ds(r0, r), :]
        y_ref[0, pl.ds(r0, r), :] = y.astype(BF16)
        return carry

    lax.fori_loop(0, tt // r, out_chunk, 0)


def _rglru(ux, gg, conv_w2, conv_b2, wgate, bgate, lam2, l, *, n, cn, tb):
    b, tt, _ = ux.shape
    kern = functools.partial(_lru_kernel, n=n, cn=cn, r=tb)
    half = lambda: pl.BlockSpec((1, tt, LANES), lambda i, hh: (i, 0, hh))
    return pl.pallas_call(
        kern,
        grid=(b, 2),
        in_specs=[
            half(), half(),
            pl.BlockSpec((1, 4, LANES), lambda i, hh: (2 * l + hh, 0, 0)),
            pl.BlockSpec((1, 1, LANES), lambda i, hh: (2 * l + hh, 0, 0)),
            pl.BlockSpec((1, LANES, 4 * LANES), lambda i, hh: (2 * l + hh, 0, 0)),
            pl.BlockSpec((1, 1, 4 * LANES), lambda i, hh: (2 * l + hh, 0, 0)),
            pl.BlockSpec((1, 2, LANES), lambda i, hh: (2 * l + hh, 0, 0)),
        ],
        out_specs=half(),
        out_shape=jax.ShapeDtypeStruct((b, tt, LRU_W), BF16),
        scratch_shapes=[pltpu.VMEM((tt, LANES), F32)] * 4,
        compiler_params=_params(("arbitrary", "arbitrary")),
        name="rglru",
    )(ux, gg, conv_w2, conv_b2, wgate, bgate, lam2)


def _dft_kernel(c_ref, s_ref, gc_ref, gs_ref, o_ref, acc_ref):
    kk = pl.program_id(2)

    def part():
        return (jnp.dot(c_ref[...], gc_ref[...], preferred_element_type=F32)
                + jnp.dot(s_ref[...], gs_ref[...], preferred_element_type=F32))

    @pl.when(kk == 0)
    def _():
        acc_ref[...] = part()

    @pl.when(kk > 0)
    def _():
        acc_ref[...] += part()

    @pl.when(kk == pl.num_programs(2) - 1)
    def _():
        o_ref[...] = acc_ref[...].astype(BF16)


def _position_dft(cmat, smat, gc, gs, *, row0):
    t = cmat.shape[0]
    nn = gc.shape[1]
    tm = _largest_divisor(t, (1024, 512, 256, 128))
    tk = _largest_divisor(t, (2048, 1024, 512, 256, 128))
    tn = _largest_divisor(nn, (512, 256))
    assert row0 % tk == 0
    return pl.pallas_call(
        _dft_kernel,
        grid=(t // tm, nn // tn, t // tk),
        in_specs=[
            pl.BlockSpec((tm, tk), lambda i, j, k: (i, k)),
            pl.BlockSpec((tm, tk), lambda i, j, k: (i, k)),
            pl.BlockSpec((tk, tn), lambda i, j, k: (row0 // tk + k, j)),
            pl.BlockSpec((tk, tn), lambda i, j, k: (row0 // tk + k, j)),
        ],
        out_specs=pl.BlockSpec((tm, tn), lambda i, j, k: (i, j)),
        out_shape=jax.ShapeDtypeStruct((t, nn), BF16),
        scratch_shapes=[pltpu.VMEM((tm, tn), F32)],
        compiler_params=_params(("arbitrary", "arbitrary", "arbitrary")),
        name="position_dft",
    )(cmat, smat, gc, gs)


def _out_kernel(attl_ref, attc_ref, y_ref, fl_ref, fc_ref, x_ref, mod_ref, w_ref, g_ref, b_ref, wr_ref,
                x1_ref, h2_ref, lg_ref, mix0_ref, mix1_ref, *, d, alpha, nbl, nblocks):
    j = pl.program_id(1)

    @pl.when(j == 0)
    def _():
        mix1_ref[...] = jnp.zeros_like(mix1_ref)

    def step(new_ref, old_ref):
        m = mod_ref[0]
        tb = x_ref.shape[1]
        is_ctx = jnp.minimum(j, nblocks - 1) >= nbl
        att = jnp.where(is_ctx, attc_ref[0], attl_ref[0])
        fy = jnp.where(is_ctx, fc_ref[...], fl_ref[...])
        for c in range(OUT_PIECES):
            rows = slice(c * tb // OUT_PIECES, (c + 1) * tb // OUT_PIECES)
            cols = slice(c * d // OUT_PIECES, (c + 1) * d // OUT_PIECES)
            x1 = _ln(alpha * x_ref[0, rows, :] + m[:, 2 * d:3 * d] * old_ref[rows, :]) * g_ref[0] + b_ref[0]
            x1_ref[0, rows, :] = x1
            h2_ref[0, rows, :] = (_ln(x1) * (1.0 + m[:, 4 * d:5 * d]) + m[:, 3 * d:4 * d]).astype(BF16)
            new_ref[:, cols] = (
                jnp.dot(att, w_ref[0, 0:ATT_W, cols], preferred_element_type=F32)
                + jnp.dot(y_ref[0], w_ref[0, ATT_W:ATT_W + LRU_W, cols], preferred_element_type=F32)
                + jnp.dot(fy, w_ref[0, ATT_W + LRU_W:, cols], preferred_element_type=F32))
        lg_ref[0] = lax.dot_general(wr_ref[0], h2_ref[0], (((1,), (1,)), ((), ())), preferred_element_type=F32)

    @pl.when(j % 2 == 0)
    def _():
        step(mix0_ref, mix1_ref)

    @pl.when(j % 2 == 1)
    def _():
        step(mix1_ref, mix0_ref)


def _out_proj(att_l, att_c, y, fy_l, fy_c, xc, mod2, w_out_bf, ln_g, ln_b, wr_t, l, *, nbl, tb, alpha, out_tokens):
    b, _, d = xc.shape
    nblocks = out_tokens // tb
    kern = functools.partial(_out_kernel, d=d, alpha=alpha, nbl=nbl, nblocks=nblocks)
    cur = lambda j: jnp.minimum(j, nblocks - 1)
    prv = lambda j: jnp.maximum(j - 1, 0)
    tok = lambda w: pl.BlockSpec((1, tb, w), lambda i, j: (i, prv(j), 0))
    return pl.pallas_call(
        kern,
        grid=(b, nblocks + 1),
        in_specs=[
            pl.BlockSpec((1, tb, ATT_W), lambda i, j: (i, jnp.minimum(cur(j), nbl - 1), 0)),
            pl.BlockSpec((1, tb, ATT_W), lambda i, j: (i, jnp.maximum(cur(j) - nbl, 0), 0)),
            pl.BlockSpec((1, tb, LRU_W), lambda i, j: (i, cur(j), 0)),
            pl.BlockSpec((tb, FNET_W), lambda i, j: (jnp.minimum(cur(j), nbl - 1), i)),
            pl.BlockSpec((tb, FNET_W), lambda i, j: (jnp.maximum(cur(j) - nbl, 0), i)),
            tok(d),
            pl.BlockSpec((1, 1, 6 * d), lambda i, j: (2 * i + (prv(j) >= nbl).astype(I32), 0, 0)),
            pl.BlockSpec((1, d, d), lambda i, j: (l, 0, 0)),
            pl.BlockSpec((1, 1, d), lambda i, j: (l, 0, 0)),
            pl.BlockSpec((1, 1, d), lambda i, j: (l, 0, 0)),
            pl.BlockSpec((1, N_EXPERTS, d), lambda i, j: (l, 0, 0)),
        ],
        out_specs=[tok(d), tok(d), pl.BlockSpec((1, N_EXPERTS, tb), lambda i, j: (i, 0, prv(j)))],
        out_shape=[
            jax.ShapeDtypeStruct((b, out_tokens, d), F32),
            jax.ShapeDtypeStruct((b, out_tokens, d), BF16),
            jax.ShapeDtypeStruct((b, N_EXPERTS, out_tokens), F32),
        ],
        scratch_shapes=[pltpu.VMEM((tb, d), F32), pltpu.VMEM((tb, d), F32)],
        compiler_params=_params(("arbitrary", "arbitrary")),
        name="out_proj",
    )(att_l, att_c, y, fy_l, fy_c, xc, mod2, w_out_bf, ln_g, ln_b, wr_t)


def _route_kernel(lg_ref, pos_ref, gate_ref, starts_ref, *, segments, n_slots):
    lg = lg_ref[0]
    lane_id = lax.broadcasted_iota(I32, (N_EXPERTS, LANES), 1)
    starts = jnp.zeros((N_EXPERTS, LANES), I32)
    chunks_per_tile = TOKEN_TILE // LANES
    e = jnp.exp(lg - jnp.max(lg, axis=0, keepdims=True))
    s = e / jnp.sum(e, axis=0, keepdims=True)
    ri = lax.broadcasted_iota(I32, (LANES, LANES), 0)
    ci = lax.broadcasted_iota(I32, (LANES, LANES), 1)
    strict_upper = jnp.where(ri < ci, 1.0, 0.0).astype(BF16)

    for lo, t, cap, base in segments:
        ss = s[:, lo:lo + t]
        bits = pltpu.bitcast(ss, I32)
        capf = float(cap)

        def search(i, thr, bits=bits, capf=capf):
            cand = thr | jnp.left_shift(jnp.int32(1), 30 - i)
            cnt = jnp.sum(jnp.where(bits >= cand, 1.0, 0.0), axis=1, keepdims=True)
            return jnp.where(cnt >= capf, cand, thr)

        thr = lax.fori_loop(0, 31, search, jnp.zeros((N_EXPERTS, 1), I32))
        need = capf - jnp.sum(jnp.where(bits > thr, 1.0, 0.0), axis=1, keepdims=True)
        off_eq = jnp.zeros((N_EXPERTS, 1), F32)
        off_sel = jnp.zeros((N_EXPERTS, 1), F32)
        for c in range(t // LANES):
            sl = slice(c * LANES, (c + 1) * LANES)
            tile, sub = divmod(lo // LANES + c, chunks_per_tile)
            if sub == 0:
                starts = jnp.where(lane_id == tile, off_sel.astype(I32) + base, starts)
            bits_c = bits[:, sl]
            eq = bits_c == thr
            eq_c = jnp.where(eq, 1.0, 0.0)
            rank_eq = jnp.dot(eq_c.astype(BF16), strict_upper, preferred_element_type=F32) + off_eq
            off_eq = off_eq + jnp.sum(eq_c, axis=1, keepdims=True)
            sel = jnp.logical_or(bits_c > thr, jnp.logical_and(eq, rank_eq < need))
            sel_c = jnp.where(sel, 1.0, 0.0)
            slot = jnp.dot(sel_c.astype(BF16), strict_upper, preferred_element_type=F32) + off_sel
            off_sel = off_sel + jnp.sum(sel_c, axis=1, keepdims=True)
            osl = slice(sub * LANES, (sub + 1) * LANES)
            pos_ref[0, tile, :, osl] = jnp.where(sel, slot.astype(I32) + base, -1)
            gate_ref[0, tile, :, osl] = jnp.where(sel, ss[:, sl], 0.0)

    n_tiles = lg.shape[1] // TOKEN_TILE
    starts_ref[0] = jnp.where(lane_id == n_tiles, n_slots, starts)


def _route(logits_t, *, segments, n_slots):
    b, ne, tt = logits_t.shape
    nt = tt // TOKEN_TILE
    assert nt < LANES and all(lo % TOKEN_TILE == 0 and t % TOKEN_TILE == 0 for lo, t, _, _ in segments)
    kern = functools.partial(_route_kernel, segments=segments, n_slots=n_slots)
    tiled = lambda: pl.BlockSpec((1, nt, ne, TOKEN_TILE), lambda i: (i, 0, 0, 0))
    return pl.pallas_call(
        kern,
        grid=(b,),
        in_specs=[pl.BlockSpec((1, ne, tt), lambda i: (i, 0, 0))],
        out_specs=[tiled(), tiled(), pl.BlockSpec((1, ne, LANES), lambda i: (i, 0, 0))],
        out_shape=[jax.ShapeDtypeStruct((b, nt, ne, TOKEN_TILE), I32),
                   jax.ShapeDtypeStruct((b, nt, ne, TOKEN_TILE), F32),
                   jax.ShapeDtypeStruct((b, ne, LANES), I32)],
        compiler_params=_params(("arbitrary",)),
        name="route",
    )(logits_t)


def _window_start(starts_ref, base_idx, tile, w, win, limit):
    c0 = starts_ref[base_idx + tile]
    lo = ((c0 >> 4) << 4) + w * win
    return lo, pl.multiple_of(jnp.minimum(lo, limit), BF16_ROWS)


def _window_count(starts_ref, base_idx, tiles, win):
    nw = jnp.int32(1)
    for tile in tiles:
        c0 = starts_ref[base_idx + tile]
        c1 = starts_ref[base_idx + tile + 1]
        nw = jnp.maximum(nw, (c1 - ((c0 >> 4) << 4) + win - 1) // win)
    return nw


def _gather_kernel(starts_ref, h_ref, pos_ref, xs_ref, acc_ref, *, slots, win, group):
    b = pl.program_id(0)
    eg = pl.program_id(1)
    nt = pos_ref.shape[1]
    acc_ref[...] = jnp.zeros_like(acc_ref)
    rel = lax.broadcasted_iota(I32, (win, TOKEN_TILE), 0)
    bases = [(b * N_EXPERTS + eg * group + i) * LANES for i in range(group)]
    nw = jnp.int32(1)
    for i in range(group):
        nw = jnp.maximum(nw, _window_count(starts_ref, bases[i], range(nt), win))

    def window_pass(w, carry):
        for tile in range(nt):
            onehots, offs = [], []
            for i in range(group):
                lo, start = _window_start(starts_ref, bases[i], tile, w, win, slots)
                prow = pos_ref[0, tile, pl.ds(eg * group + i, 1), :] - start
                onehots.append(jnp.where(rel == prow, 1.0, 0.0).astype(BF16))
                offs.append(start)
            res = jnp.dot(jnp.concatenate(onehots, axis=0), h_ref[0, tile * TOKEN_TILE:(tile + 1) * TOKEN_TILE, :],
                          preferred_element_type=F32)
            for i in range(group):
                acc_ref[i, pl.ds(offs[i], win), :] += res[i * win:(i + 1) * win].astype(BF16)
        return carry

    lax.fori_loop(0, nw, window_pass, 0)
    xs_ref[...] = acc_ref[:, 0:slots, :]


def _gather(starts, h2, pos, *, slots, win, group=8):
    b, tt, d = h2.shape
    nt = tt // TOKEN_TILE
    kern = functools.partial(_gather_kernel, slots=slots, win=win, group=group)
    return pl.pallas_call(
        kern,
        grid_spec=pltpu.PrefetchScalarGridSpec(
            num_scalar_prefetch=1,
            grid=(b, N_EXPERTS // group),
            in_specs=[
                pl.BlockSpec((1, tt, d), lambda i, g, s: (i, 0, 0)),
                pl.BlockSpec((1, nt, N_EXPERTS, TOKEN_TILE), lambda i, g, s: (i, 0, 0, 0)),
            ],
            out_specs=pl.BlockSpec((group, slots, d), lambda i, g, s: (g, i, 0)),
            scratch_shapes=[pltpu.VMEM((group, slots + win, d), BF16)],
        ),
        out_shape=jax.ShapeDtypeStruct((N_EXPERTS, b * slots, d), BF16),
        compiler_params=_params(("arbitrary", "arbitrary")),
        name="moe_gather",
    )(starts, h2, pos)


def _ffn_kernel(xs_ref, wg_ref, wu_ref, wd_ref, ys_ref, *, fchunk):
    xs = xs_ref[0]
    f = wg_ref.shape[-1]
    acc = None
    for c in range(f // fchunk):
        sl = slice(c * fchunk, (c + 1) * fchunk)
        a = jnp.dot(xs, wg_ref[0, 0, :, sl], preferred_element_type=F32)
        u = jnp.dot(xs, wu_ref[0, 0, :, sl], preferred_element_type=F32)
        hm = (a * jax.nn.sigmoid(a) * u).astype(BF16)
        y = jnp.dot(hm, wd_ref[0, 0, sl, :], preferred_element_type=F32)
        acc = y if acc is None else acc + y
    ys_ref[0] = acc.astype(BF16)


def _expert_ffn(xs, wg, wu, wd, l, *, slots):
    ne, rows, d = xs.shape
    f = wg.shape[-1]
    nb = rows // slots
    kern = functools.partial(_ffn_kernel, fchunk=_largest_divisor(f, (512,)))
    return pl.pallas_call(
        kern,
        grid=(ne, nb),
        in_specs=[
            pl.BlockSpec((1, slots, d), lambda e, i: (e, i, 0)),
            pl.BlockSpec((1, 1, d, f), lambda e, i: (l, e, 0, 0)),
            pl.BlockSpec((1, 1, d, f), lambda e, i: (l, e, 0, 0)),
            pl.BlockSpec((1, 1, f, d), lambda e, i: (l, e, 0, 0)),
        ],
        out_specs=pl.BlockSpec((1, slots, d), lambda e, i: (e, i, 0)),
        out_shape=jax.ShapeDtypeStruct((ne, rows, d), BF16),
        compiler_params=_params(("arbitrary", "arbitrary")),
        name="expert_ffn",
    )(xs, wg, wu, wd)


def _combine_kernel(starts_ref, ys_ref, pos_ref, gate_ref, x1_ref, mod_ref, g_ref, b_ref, o_ref,
                    *, slots, win, d, alpha):
    b = pl.program_id(0)
    tile = pl.program_id(1)
    rel = lax.broadcasted_iota(I32, (win, TOKEN_TILE), 0)
    bases = [(b * N_EXPERTS + e) * LANES for e in range(N_EXPERTS)]
    nw = jnp.int32(1)
    for e in range(N_EXPERTS):
        c0 = starts_ref[bases[e] + tile]
        c1 = starts_ref[bases[e] + tile + 1]
        nw = jnp.maximum(nw, (c1 - ((c0 >> 4) << 4) + win - 1) // win)

    def window_pass(w, moe):
        gated, rows = [], []
        for e in range(N_EXPERTS):
            lo, start = _window_start(starts_ref, bases[e], tile, w, win, slots - win)
            prow = pos_ref[0, 0, e:e + 1, :]
            prow = jnp.where(jnp.logical_and(prow >= lo, prow < lo + win), prow - start, -1)
            gated.append(jnp.where(rel == prow, gate_ref[0, 0, e:e + 1, :], 0.0).astype(BF16))
            rows.append(ys_ref[e, pl.ds(start, win), :])
        return moe + lax.dot_general(jnp.concatenate(gated, axis=0), jnp.concatenate(rows, axis=0),
                                     (((0,), (0,)), ((), ())), preferred_element_type=F32)

    moe = lax.fori_loop(0, nw, window_pass, jnp.zeros((TOKEN_TILE, d), F32))
    m = mod_ref[0]
    o_ref[0] = _ln(alpha * x1_ref[0] + m[:, 5 * d:6 * d] * moe) * g_ref[0] + b_ref[0]


def _combine(starts, ys, pos, gate, x1, mod2, ln_g, ln_b, l, *, slots, win, n_lat_tiles, alpha, out_tokens):
    b, _, d = x1.shape
    kern = functools.partial(_combine_kernel, slots=slots, win=win, d=d, alpha=alpha)
    tiled = lambda: pl.BlockSpec((1, 1, N_EXPERTS, TOKEN_TILE), lambda i, t, s: (i, t, 0, 0))
    tok = lambda: pl.BlockSpec((1, TOKEN_TILE, d), lambda i, t, s: (i, t, 0))
    return pl.pallas_call(
        kern,
        grid_spec=pltpu.PrefetchScalarGridSpec(
            num_scalar_prefetch=1,
            grid=(b, out_tokens // TOKEN_TILE),
            in_specs=[
                pl.BlockSpec((N_EXPERTS, slots, d), lambda i, t, s: (0, i, 0)),
                tiled(), tiled(), tok(),
                pl.BlockSpec((1, 1, 6 * d), lambda i, t, s: (2 * i + (t >= n_lat_tiles).astype(I32), 0, 0)),
                pl.BlockSpec((1, 1, d), lambda i, t, s: (l, 0, 0)),
                pl.BlockSpec((1, 1, d), lambda i, t, s: (l, 0, 0)),
            ],
            out_specs=tok(),
        ),
        out_shape=jax.ShapeDtypeStruct((b, out_tokens, d), F32),
        compiler_params=_params(("arbitrary", "arbitrary")),
        name="moe_combine",
    )(starts, ys, pos, gate, x1, mod2, ln_g, ln_b)


def _rope_tables(n, cn):
    lane = np.arange(LANES)
    within = lane % QK_DIM
    use_col = (within // 32) == 1
    first_half = (within % 32) < 16
    inv = ROPE_BASE ** (-(within % 16).astype(np.float64) / 16.0)
    pos = np.arange(n)
    coord = np.where(use_col[None, :], (pos % GRID_W)[:, None], (pos // GRID_W)[:, None]).astype(np.float32)
    ang = jnp.asarray(coord) * jnp.asarray(inv.astype(np.float32))[None, :]
    cos = jnp.cos(ang)
    sin = jnp.where(jnp.asarray(first_half)[None, :], -jnp.sin(ang), jnp.sin(ang))
    cos = jnp.concatenate([cos, jnp.ones((cn, LANES), F32)], axis=0)
    sin = jnp.concatenate([sin, jnp.zeros((cn, LANES), F32)], axis=0)
    return cos, sin


def _channel_dft():
    idx = np.arange(FNET_W)
    same = (idx[:, None] // FNET_GROUP_W) == (idx[None, :] // FNET_GROUP_W)
    ang = 2.0 * np.pi * ((idx[:, None] % FNET_GROUP_W) * (idx[None, :] % FNET_GROUP_W) % FNET_GROUP_W) / FNET_GROUP_W
    cs = np.concatenate([np.where(same, np.cos(ang), 0.0), np.where(same, np.sin(ang), 0.0)], axis=1)
    return jnp.asarray(cs.astype(np.float32)).astype(BF16)


def _position_dft_mats(t):
    g = math.gcd(t, 64)
    kk = jnp.arange(t, dtype=I32)

    def table(m):
        ph = ((kk[:, None] * m[None, :]) % t).astype(F32) * (2.0 * math.pi / t)
        return jnp.cos(ph), jnp.sin(ph)

    ch, sh = table(jnp.arange(t // g, dtype=I32) * g)
    cl, sl = table(jnp.arange(g, dtype=I32))
    scale = 1.0 / math.sqrt(t * FNET_GROUP_W)
    cmat = (ch[:, :, None] * cl[:, None, :] - sh[:, :, None] * sl[:, None, :]).reshape(t, t) * scale
    smat = (sh[:, :, None] * cl[:, None, :] + ch[:, :, None] * sl[:, None, :]).reshape(t, t) * (-scale)
    return cmat.astype(BF16), smat.astype(BF16)


def _block_diag_gates(wa, wx):
    depth = wa.shape[0]

    def dense(wb):
        eye = jnp.eye(LRU_BLOCKS, dtype=wb.dtype)
        return jnp.einsum('lncd,nm->lncmd', wb, eye).reshape(depth, LRU_W, LRU_W)

    halves = []
    for hh in range(2):
        sl = slice(hh * LANES, (hh + 1) * LANES)
        cols = [dense(wmat[:, dr])[:, sl, sl] for dr in range(2) for wmat in (wa, wx)]
        halves.append(jnp.concatenate(cols, axis=-1))
    return jnp.stack(halves, axis=1).reshape(depth * 2, LANES, 4 * LANES).astype(BF16)


def _gate_bias(ba, bx):
    depth = ba.shape[0]
    halves = []
    for hh in range(2):
        sl = slice(hh * LANES, (hh + 1) * LANES)
        halves.append(jnp.concatenate([bvec[:, dr, sl] for dr in range(2) for bvec in (ba, bx)], axis=-1))
    return jnp.stack(halves, axis=1).reshape(depth * 2, 1, 4 * LANES)


def _split_halves(a):
    depth, r, _ = a.shape
    return a.reshape(depth, r, 2, LANES).transpose(0, 2, 1, 3).reshape(depth * 2, r, LANES)


def kernel(x, c, ctx, c_ctx, w_mod, b_mod, w_in, lam_q1, lam_k1, lam_q2, lam_k2, attn_norm_g, conv_w, conv_b, lru_wa, lru_ba, lru_wx, lru_bx, lru_lam, w_out, ln1_g, ln1_b, w_router, w_gate, w_up, w_down, ln2_g, ln2_b):
    b, n, d = x.shape
    cn = ctx.shape[1]
    depth = w_mod.shape[0]
    tt = n + cn
    tb = _largest_divisor(math.gcd(n, cn), (256, 128))
    assert n % tb == 0 and cn % tb == 0 and n % GRID_W == 0 and tt % LANES == 0
    nbl = n // tb
    cap_l = CAPACITY_FACTOR * n // N_EXPERTS
    cap_c = CAPACITY_FACTOR * cn // N_EXPERTS
    assert cap_l % BF16_ROWS == 0 and cap_c % BF16_ROWS == 0 and n % TOKEN_TILE == 0 and cn % TOKEN_TILE == 0
    alpha = (2 * depth) ** 0.25

    rows = -(-(b + 1) // SUBLANES) * SUBLANES
    cc = jnp.concatenate([c, c_ctx[None, :], jnp.zeros((rows - b - 1, d), F32)], axis=0)
    mod = _modulation(cc, w_mod, b_mod)

    cos_t, sin_t = _rope_tables(n, cn)
    cs = _channel_dft()
    dft_l = _position_dft_mats(n)
    dft_c = _position_dft_mats(cn)

    w_in_bf = w_in.astype(BF16)
    w_out_bf = w_out.astype(BF16)
    wg_bf = w_gate.astype(BF16)
    wu_bf = w_up.astype(BF16)
    wd_bf = w_down.astype(BF16)
    wr_t = jnp.swapaxes(w_router, 1, 2).astype(BF16)
    lamv = jnp.stack([lam_q1, lam_k1, lam_q2, lam_k2], axis=1).astype(F32)
    lam_init = np.array([0.8 - 0.6 * math.exp(-0.3 * l) for l in range(depth)], np.float32)
    lin = jnp.asarray(np.broadcast_to(lam_init[:, None, None], (depth, 1, LANES)).copy())
    gain = attn_norm_g.reshape(depth * N_HEADS, 1, HEAD_V)
    conv_w2 = _split_halves(conv_w)
    conv_b2 = _split_halves(conv_b[:, None, :])
    lam2 = _split_halves(lru_lam)
    wgate = _block_diag_gates(lru_wa, lru_wx)
    bgate = _gate_bias(lru_ba, lru_bx)
    ln1g, ln1b = ln1_g[:, None, :], ln1_b[:, None, :]
    ln2g, ln2b = ln2_g[:, None, :], ln2_b[:, None, :]

    xc = jnp.concatenate([x, ctx], axis=1)
    for l in range(depth):
        ml = mod[l]
        mod2 = jnp.stack([ml[:b], jnp.broadcast_to(ml[b][None, :], (b, 6 * d))], axis=1).reshape(2 * b, 1, 6 * d)
        qx, k, v, ux, gg, gc, gs = _in_proj(xc, mod2, w_in_bf, l, cos_t, sin_t, cs, nbl=nbl, tb=tb)
        keep_ctx = l < depth - 1
        tokens = tt if keep_ctx else n
        att_l, att_c = _attention(qx, k, v, lamv, lin, gain, l, n=n, tb=tb)
        y = _rglru(ux, gg, conv_w2, conv_b2, wgate, bgate, lam2, l, n=n, cn=cn, tb=tb)
        fy_l = _position_dft(*dft_l, gc, gs, row0=0)
        fy_c = _position_dft(*dft_c, gc, gs, row0=n) if keep_ctx else fy_l
        x1, h2, logits_t = _out_proj(att_l, att_c, y, fy_l, fy_c, xc, mod2, w_out_bf, ln1g, ln1b, wr_t, l,
                                     nbl=nbl, tb=tb, alpha=alpha, out_tokens=tokens)
        segments = ((0, n, cap_l, 0), (n, cn, cap_c, cap_l)) if keep_ctx else ((0, n, cap_l, 0),)
        slots = cap_l + cap_c if keep_ctx else cap_l
        win = min(SLOT_WINDOW, slots)
        pos, gate, starts = _route(logits_t, segments=segments, n_slots=slots)
        starts = starts.reshape(-1)
        xs = _gather(starts, h2, pos, slots=slots, win=win)
        ys = _expert_ffn(xs, wg_bf, wu_bf, wd_bf, l, slots=slots)
        xc = _combine(starts, ys, pos, gate, x1, mod2, ln2g, ln2b, l, slots=slots, win=win,
                      n_lat_tiles=n // TOKEN_TILE, alpha=alpha, out_tokens=tokens)
    return xc
```

```python
import functools
import math

import numpy as np
import jax
import jax.numpy as jnp
from jax import lax
from jax.experimental import pallas as pl
from jax.experimental.pallas import tpu as pltpu

F32 = jnp.float32
BF16 = jnp.bfloat16
I32 = jnp.int32

GRID_W = 64
QK_DIM = 64
N_HEADS = 4
HEAD_V = 128
QK_W = 512
ATT_W = 512
LRU_W = 256
LRU_BLOCKS = 4
LRU_BLOCK_W = 64
LRU_C = 8.0
FNET_W = 256
FNET_GROUP_W = 64
IN_W = 2304
ROPE_BASE = 10000.0
N_EXPERTS = 16
CAPACITY_FACTOR = 2
LN_EPS = 1e-5
RMS_EPS = 1e-6
GELU_C = math.sqrt(2.0 / math.pi)
LOG2E = math.log2(math.e)

LANES = 128
SUBLANES = 8
BF16_ROWS = 16
KEY_CHUNK = 256
OUT_PIECES = 4
SCAN_UNROLL = 8
TOKEN_TILE = 256
SLOT_WINDOW = 64
VMEM_LIMIT = 56 << 20


def _params(sem, vmem=VMEM_LIMIT):
    return pltpu.CompilerParams(dimension_semantics=sem, vmem_limit_bytes=vmem)


def _ln(x):
    mu = jnp.mean(x, axis=-1, keepdims=True)
    xc = x - mu
    var = jnp.mean(xc * xc, axis=-1, keepdims=True)
    return xc * lax.rsqrt(var + LN_EPS)


def _largest_divisor(n, candidates):
    for c in candidates:
        if c <= n and n % c == 0:
            return c
    return n


def _mod_kernel(c_ref, w_ref, b_ref, o_ref):
    c = c_ref[...]
    s = c * jax.nn.sigmoid(c)
    o_ref[0] = jnp.dot(s, w_ref[0], precision=lax.Precision.HIGHEST, preferred_element_type=F32) + b_ref[0]


def _modulation(cc, w_mod, b_mod):
    depth, d, d6 = w_mod.shape
    rows = cc.shape[0]
    tn = 1024
    return pl.pallas_call(
        _mod_kernel,
        grid=(depth, d6 // tn),
        in_specs=[
            pl.BlockSpec((rows, d), lambda l, n: (0, 0)),
            pl.BlockSpec((1, d, tn), lambda l, n: (l, 0, n)),
            pl.BlockSpec((1, 1, tn), lambda l, n: (l, 0, n)),
        ],
        out_specs=pl.BlockSpec((1, rows, tn), lambda l, n: (l, 0, n)),
        out_shape=jax.ShapeDtypeStruct((depth, rows, d6), F32),
        compiler_params=_params(("arbitrary", "arbitrary")),
        name="modulation",
    )(cc, w_mod, b_mod.reshape(depth, 1, d6))


def _in_kernel(x_ref, mod_ref, w_ref, cos_ref, sin_ref, cs_ref,
               qx_ref, k_ref, v_ref, ux_ref, gg_ref, gc_ref, gs_ref, *, d):
    x = x_ref[0]
    m = mod_ref[0]
    h = _ln(x) * (1.0 + m[:, d:2 * d]) + m[:, 0:d]
    z = jnp.dot(h.astype(BF16), w_ref[0], preferred_element_type=F32)

    tb = x.shape[0]
    cos = cos_ref[...]
    sin = sin_ref[...]
    lane = lax.broadcasted_iota(I32, (tb, LANES), 1)
    first_half = (lane & 31) < 16
    low_map = lane < QK_DIM

    def rope(t):
        partner = jnp.where(first_half, pltpu.roll(t, LANES - 16, 1), pltpu.roll(t, 16, 1))
        return t * cos + partner * sin

    for p in range(N_HEADS):
        qp = rope(z[:, p * LANES:(p + 1) * LANES] * (QK_DIM ** -0.5 * LOG2E))
        qx_ref[0, :, (2 * p) * LANES:(2 * p + 1) * LANES] = jnp.where(low_map, qp, 0.0).astype(BF16)
        qx_ref[0, :, (2 * p + 1) * LANES:(2 * p + 2) * LANES] = jnp.where(low_map, 0.0, qp).astype(BF16)
        kp = rope(z[:, QK_W + p * LANES:QK_W + (p + 1) * LANES])
        k_ref[0, :, p * LANES:(p + 1) * LANES] = kp.astype(BF16)

    o = 2 * QK_W
    v_ref[0] = z[:, o:o + ATT_W].astype(BF16)
    o += ATT_W
    ux_ref[0] = z[:, o:o + LRU_W]
    o += LRU_W
    g = z[:, o:o + LRU_W]
    gg_ref[0] = 0.5 * g * (1.0 + jnp.tanh(GELU_C * (g + 0.044715 * (g * g * g))))
    o += LRU_W
    uf = z[:, o:o + FNET_W].astype(BF16)
    gcs = jnp.dot(uf, cs_ref[...], preferred_element_type=F32)
    gc_ref[...] = gcs[:, :FNET_W].astype(BF16)
    gs_ref[...] = gcs[:, FNET_W:].astype(BF16)


def _in_proj(xc, mod2, w_in_bf, l, cos_t, sin_t, cs, *, nbl, tb):
    b, tt, d = xc.shape
    nbt = tt // tb
    kern = functools.partial(_in_kernel, d=d)
    tok = lambda w: pl.BlockSpec((1, tb, w), lambda i, j: (i, j, 0))
    return pl.pallas_call(
        kern,
        grid=(b, nbt),
        in_specs=[
            tok(d),
            pl.BlockSpec((1, 1, 6 * d), lambda i, j: (2 * i + (j >= nbl).astype(I32), 0, 0)),
            pl.BlockSpec((1, d, IN_W), lambda i, j: (l, 0, 0)),
            pl.BlockSpec((tb, LANES), lambda i, j: (j, 0)),
            pl.BlockSpec((tb, LANES), lambda i, j: (j, 0)),
            pl.BlockSpec((FNET_W, 2 * FNET_W), lambda i, j: (0, 0)),
        ],
        out_specs=[
            tok(2 * QK_W), tok(QK_W), tok(ATT_W), tok(LRU_W), tok(LRU_W),
            pl.BlockSpec((tb, FNET_W), lambda i, j: (j, i)),
            pl.BlockSpec((tb, FNET_W), lambda i, j: (j, i)),
        ],
        out_shape=[
            jax.ShapeDtypeStruct((b, tt, 2 * QK_W), BF16),
            jax.ShapeDtypeStruct((b, tt, QK_W), BF16),
            jax.ShapeDtypeStruct((b, tt, ATT_W), BF16),
            jax.ShapeDtypeStruct((b, tt, LRU_W), F32),
            jax.ShapeDtypeStruct((b, tt, LRU_W), F32),
            jax.ShapeDtypeStruct((tt, b * FNET_W), BF16),
            jax.ShapeDtypeStruct((tt, b * FNET_W), BF16),
        ],
        compiler_params=_params(("arbitrary", "arbitrary")),
        name="in_proj",
    )(xc, mod2, w_in_bf, cos_t, sin_t, cs)


def _attn_kernel(ql_ref, qc_ref, k_ref, v_ref, lamv_ref, lin_ref, g_ref, ol_ref, oc_ref, vt_ref, s0_ref, s1_ref,
                 *, n, lat_steps, qb):
    j = pl.program_id(2)
    lv = lamv_ref[0]
    lam_init = lin_ref[0][:, 0:1]
    lam = (jnp.exp(jnp.sum(lv[0:1] * lv[1:2], axis=1, keepdims=True))
           - jnp.exp(jnp.sum(lv[2:3] * lv[3:4], axis=1, keepdims=True)) + lam_init)
    gain = g_ref[0] * (1.0 - lam_init)
    nt = (((1,), (1,)), ((), ()))
    tt = k_ref.shape[1]

    @pl.when(j == 0)
    def _():
        vt_ref[0:HEAD_V, :] = v_ref[0].astype(F32).T.astype(BF16)
        row = lax.broadcasted_iota(I32, (BF16_ROWS, tt), 0)
        vt_ref[HEAD_V:, :] = jnp.where(row == 0, 1.0, 0.0).astype(BF16)

    def attend(q_ref, o_ref, blocks, lo):
        chunks = [(c, min(KEY_CHUNK, tt - c)) for c in range(lo, tt, KEY_CHUNK)]
        tq = s0_ref.shape[1]
        streams = [(blk, mp) for blk in range(blocks) for mp in range(2)]
        s_refs = (s0_ref, s1_ref)

        def scores(i, c, w, m):
            blk, mp = streams[i]
            q = q_ref[0, blk * tq:(blk + 1) * tq, mp * LANES:(mp + 1) * LANES]
            st = lax.dot_general(k_ref[0, c:c + w, :], q, nt, preferred_element_type=F32)
            s_refs[i % 2][c:c + w, :] = st
            mc = jnp.max(st, axis=0, keepdims=True)
            return mc if m is None else jnp.maximum(m, mc)

        def weighted_values(i, c, w, m, acc):
            pt = jnp.exp2(s_refs[i % 2][c:c + w, :] - m).astype(BF16)
            part = jnp.dot(vt_ref[:, c:c + w], pt, preferred_element_type=F32)
            return part if acc is None else acc + part

        ms = [None] * len(streams)
        accs = [None] * len(streams)
        for stage in range(len(streams) + 1):
            for c, w in chunks:
                if stage >= 1:
                    accs[stage - 1] = weighted_values(stage - 1, c, w, ms[stage - 1], accs[stage - 1])
                if stage < len(streams):
                    ms[stage] = scores(stage, c, w, ms[stage])
            if stage >= 2 and stage % 2 == 0:
                blk = stage // 2 - 1
                a0, a1 = accs[stage - 2], accs[stage - 1]
                ot = (a0[0:HEAD_V] / a0[HEAD_V:HEAD_V + 1]) - lam * (a1[0:HEAD_V] / a1[HEAD_V:HEAD_V + 1])
                rt = ot * lax.rsqrt(jnp.mean(ot * ot, axis=0, keepdims=True) + RMS_EPS)
                o_ref[0, blk * tq:(blk + 1) * tq, :] = (rt.T * gain).astype(BF16)

    @pl.when(j < lat_steps)
    def _():
        attend(ql_ref, ol_ref, qb, 0)

    @pl.when(j >= lat_steps)
    def _():
        attend(qc_ref, oc_ref, qc_ref.shape[1] // s0_ref.shape[1], n)


def _attention(qx, k, v, lamv, lin, gain, l, *, n, tb):
    b, tt, _ = k.shape
    cn = tt - n
    qb = _largest_divisor(n // tb, (16, 8, 4, 2, 1))
    lat_steps = n // (qb * tb)
    assert n % cn == 0 and cn % tb == 0
    kern = functools.partial(_attn_kernel, n=n, lat_steps=lat_steps, qb=qb)
    lat = lambda w: pl.BlockSpec((1, qb * tb, w), lambda i, h, j: (i, jnp.minimum(j, lat_steps - 1), h))
    ctx = lambda w: pl.BlockSpec((1, cn, w), lambda i, h, j: (i, n // cn, h))
    return pl.pallas_call(
        kern,
        grid=(b, N_HEADS, lat_steps + 1),
        in_specs=[
            lat(2 * LANES), ctx(2 * LANES),
            pl.BlockSpec((1, tt, LANES), lambda i, h, j: (i, 0, h)),
            pl.BlockSpec((1, tt, LANES), lambda i, h, j: (i, 0, h)),
            pl.BlockSpec((1, 4, QK_DIM), lambda i, h, j: (l, 0, 0)),
            pl.BlockSpec((1, 1, LANES), lambda i, h, j: (l, 0, 0)),
            pl.BlockSpec((1, 1, HEAD_V), lambda i, h, j: (l * N_HEADS + h, 0, 0)),
        ],
        out_specs=[lat(LANES), pl.BlockSpec((1, cn, LANES), lambda i, h, j: (i, 0, h))],
        out_shape=[jax.ShapeDtypeStruct((b, n, ATT_W), BF16), jax.ShapeDtypeStruct((b, cn, ATT_W), BF16)],
        scratch_shapes=[pltpu.VMEM((HEAD_V + BF16_ROWS, tt), BF16),
                        pltpu.VMEM((tt, tb), F32), pltpu.VMEM((tt, tb), F32)],
        compiler_params=_params(("arbitrary", "arbitrary", "arbitrary")),
        name="diff_attention",
    )(qx, qx, k, v, lamv, lin, gain)


def _lru_kernel(ux_ref, gg_ref, cw_ref, cb_ref, wg_ref, bg_ref, lam_ref, y_ref,
                a_f, b_f, a_b, b_b, *, n, cn, r):
    tt = n + cn
    w = LANES
    cw = cw_ref[0]
    cb = cb_ref[0]
    bg = bg_ref[0]
    neg_lam = -lam_ref[0]
    softplus = jnp.maximum(neg_lam, 0.0) + jnp.log(1.0 + jnp.exp(-jnp.abs(neg_lam)))
    row8 = lax.broadcasted_iota(I32, (r // SUBLANES, SUBLANES, w), 1)
    ext_rows = r + 2 * SUBLANES

    def local_scan(a, bb, reverse):
        a = a.reshape(r // SUBLANES, SUBLANES, w)
        bb = bb.reshape(r // SUBLANES, SUBLANES, w)
        for s in (1, 2, 4):
            if reverse:
                a_sh = pltpu.roll(a, SUBLANES - s, 1)
                b_sh = pltpu.roll(bb, SUBLANES - s, 1)
                valid = row8 < SUBLANES - s
            else:
                a_sh = pltpu.roll(a, s, 1)
                b_sh = pltpu.roll(bb, s, 1)
                valid = row8 >= s
            bb = jnp.where(valid, a * b_sh + bb, bb)
            a = jnp.where(valid, a * a_sh, a)
        return a.reshape(r, w), bb.reshape(r, w)

    def gates_chunk(c, carry):
        r0 = pl.multiple_of(c * r, r)
        seg_start = jnp.logical_or(r0 == 0, r0 == n)
        seg_end = jnp.logical_or(r0 + r == n, r0 + r == tt)
        main = ux_ref[0, pl.ds(r0, r), :]
        prev = ux_ref[0, pl.ds(pl.multiple_of(jnp.maximum(r0 - SUBLANES, 0), SUBLANES), SUBLANES), :]
        nxt = ux_ref[0, pl.ds(pl.multiple_of(jnp.minimum(r0 + r, tt - SUBLANES), SUBLANES), SUBLANES), :]
        prev = jnp.where(seg_start, 0.0, prev)
        nxt = jnp.where(seg_end, 0.0, nxt)
        ext = jnp.concatenate([prev, main, nxt], axis=0)
        u = cb
        for t in range(4):
            sh = (2 - t) % ext_rows
            win = ext if sh == 0 else pltpu.roll(ext, sh, 0)
            u = u + cw[t:t + 1, :] * win[SUBLANES:SUBLANES + r, :]
        zz = jnp.dot(u.astype(BF16), wg_ref[0], preferred_element_type=F32) + bg
        for dr, (a_s, b_s) in enumerate(((a_f, b_f), (a_b, b_b))):
            rg = 0.5 + 0.5 * jnp.tanh(0.5 * zz[:, (2 * dr) * w:(2 * dr + 1) * w])
            ig = 0.5 + 0.5 * jnp.tanh(0.5 * zz[:, (2 * dr + 1) * w:(2 * dr + 2) * w])
            a = jnp.exp(-LRU_C * rg * softplus[dr:dr + 1, :])
            bb = jnp.sqrt(1.0 - a * a) * ig * u
            a, bb = local_scan(a, bb, reverse=(dr == 1))
            a_s[pl.ds(r0, r), :] = a
            b_s[pl.ds(r0, r), :] = bb
        return carry

    lax.fori_loop(0, tt // r, gates_chunk, 0)

    def seg_scan(first_tile, ntiles, cf, cbk):
        def body(i, carry):
            cf, cbk = carry
            rf = pl.multiple_of((first_tile + i) * SUBLANES, SUBLANES)
            hf = b_f[pl.ds(rf, SUBLANES), :] + a_f[pl.ds(rf, SUBLANES), :] * cf
            b_f[pl.ds(rf, SUBLANES), :] = hf
            rb = pl.multiple_of((first_tile + ntiles - 1 - i) * SUBLANES, SUBLANES)
            hb = b_b[pl.ds(rb, SUBLANES), :] + a_b[pl.ds(rb, SUBLANES), :] * cbk
            b_b[pl.ds(rb, SUBLANES), :] = hb
            return hf[SUBLANES - 1:SUBLANES, :], hb[0:1, :]
        return lax.fori_loop(0, ntiles, body, (cf, cbk), unroll=SCAN_UNROLL)

    zero = jnp.zeros((1, w), F32)
    cf, cbk = seg_scan(n // SUBLANES, cn // SUBLANES, zero, zero)
    seg_scan(0, n // SUBLANES, cf, cbk)

    def out_chunk(c, carry):
        r0 = pl.multiple_of(c * r, r)
        y = (b_f[pl.ds(r0, r), :] + b_b[pl.ds(r0, r), :]) * gg_ref[0, pl.ds(r0, r), :]
        y_ref[0, pl.ds(r0, r), :] = y.astype(BF16)
        return carry

    lax.fori_loop(0, tt // r, out_chunk, 0)


def _rglru(ux, gg, conv_w2, conv_b2, wgate, bgate, lam2, l, *, n, cn, tb):
    b, tt, _ = ux.shape
    kern = functools.partial(_lru_kernel, n=n, cn=cn, r=tb)
    half = lambda: pl.BlockSpec((1, tt, LANES), lambda i, hh: (i, 0, hh))
    return pl.pallas_call(
        kern,
        grid=(b, 2),
        in_specs=[
            half(), half(),
            pl.BlockSpec((1, 4, LANES), lambda i, hh: (2 * l + hh, 0, 0)),
            pl.BlockSpec((1, 1, LANES), lambda i, hh: (2 * l + hh, 0, 0)),
            pl.BlockSpec((1, LANES, 4 * LANES), lambda i, hh: (2 * l + hh, 0, 0)),
            pl.BlockSpec((1, 1, 4 * LANES), lambda i, hh: (2 * l + hh, 0, 0)),
            pl.BlockSpec((1, 2, LANES), lambda i, hh: (2 * l + hh, 0, 0)),
        ],
        out_specs=half(),
        out_shape=jax.ShapeDtypeStruct((b, tt, LRU_W), BF16),
        scratch_shapes=[pltpu.VMEM((tt, LANES), F32)] * 4,
        compiler_params=_params(("arbitrary", "arbitrary")),
        name="rglru",
    )(ux, gg, conv_w2, conv_b2, wgate, bgate, lam2)


def _dft_kernel(c_ref, s_ref, gc_ref, gs_ref, o_ref, acc_ref):
    kk = pl.program_id(2)

    def part():
        return (jnp.dot(c_ref[...], gc_ref[...], preferred_element_type=F32)
                + jnp.dot(s_ref[...], gs_ref[...], preferred_element_type=F32))

    @pl.when(kk == 0)
    def _():
        acc_ref[...] = part()

    @pl.when(kk > 0)
    def _():
        acc_ref[...] += part()

    @pl.when(kk == pl.num_programs(2) - 1)
    def _():
        o_ref[...] = acc_ref[...].astype(BF16)


def _position_dft(cmat, smat, gc, gs, *, row0):
    t = cmat.shape[0]
    nn = gc.shape[1]
    tm = _largest_divisor(t, (1024, 512, 256, 128))
    tk = _largest_divisor(t, (2048, 1024, 512, 256, 128))
    tn = _largest_divisor(nn, (512, 256))
    assert row0 % tk == 0
    return pl.pallas_call(
        _dft_kernel,
        grid=(t // tm, nn // tn, t // tk),
        in_specs=[
            pl.BlockSpec((tm, tk), lambda i, j, k: (i, k)),
            pl.BlockSpec((tm, tk), lambda i, j, k: (i, k)),
            pl.BlockSpec((tk, tn), lambda i, j, k: (row0 // tk + k, j)),
            pl.BlockSpec((tk, tn), lambda i, j, k: (row0 // tk + k, j)),
        ],
        out_specs=pl.BlockSpec((tm, tn), lambda i, j, k: (i, j)),
        out_shape=jax.ShapeDtypeStruct((t, nn), BF16),
        scratch_shapes=[pltpu.VMEM((tm, tn), F32)],
        compiler_params=_params(("arbitrary", "arbitrary", "arbitrary")),
        name="position_dft",
    )(cmat, smat, gc, gs)


def _out_kernel(attl_ref, attc_ref, y_ref, fl_ref, fc_ref, x_ref, mod_ref, w_ref, g_ref, b_ref, wr_ref,
                x1_ref, h2_ref, lg_ref, mix0_ref, mix1_ref, *, d, alpha, nbl, nblocks):
    j = pl.program_id(1)

    @pl.when(j == 0)
    def _():
        mix1_ref[...] = jnp.zeros_like(mix1_ref)

    def step(new_ref, old_ref):
        m = mod_ref[0]
        tb = x_ref.shape[1]
        is_ctx = jnp.minimum(j, nblocks - 1) >= nbl
        att = jnp.where(is_ctx, attc_ref[0], attl_ref[0])
        fy = jnp.where(is_ctx, fc_ref[...], fl_ref[...])
        for c in range(OUT_PIECES):
            rows = slice(c * tb // OUT_PIECES, (c + 1) * tb // OUT_PIECES)
            cols = slice(c * d // OUT_PIECES, (c + 1) * d // OUT_PIECES)
            x1 = _ln(alpha * x_ref[0, rows, :] + m[:, 2 * d:3 * d] * old_ref[rows, :]) * g_ref[0] + b_ref[0]
            x1_ref[0, rows, :] = x1
            h2_ref[0, rows, :] = (_ln(x1) * (1.0 + m[:, 4 * d:5 * d]) + m[:, 3 * d:4 * d]).astype(BF16)
            new_ref[:, cols] = (
                jnp.dot(att, w_ref[0, 0:ATT_W, cols], preferred_element_type=F32)
                + jnp.dot(y_ref[0], w_ref[0, ATT_W:ATT_W + LRU_W, cols], preferred_element_type=F32)
                + jnp.dot(fy, w_ref[0, ATT_W + LRU_W:, cols], preferred_element_type=F32))
        lg_ref[0] = lax.dot_general(wr_ref[0], h2_ref[0], (((1,), (1,)), ((), ())), preferred_element_type=F32)

    @pl.when(j % 2 == 0)
    def _():
        step(mix0_ref, mix1_ref)

    @pl.when(j % 2 == 1)
    def _():
        step(mix1_ref, mix0_ref)


def _out_proj(att_l, att_c, y, fy_l, fy_c, xc, mod2, w_out_bf, ln_g, ln_b, wr_t, l, *, nbl, tb, alpha, out_tokens):
    b, _, d = xc.shape
    nblocks = out_tokens // tb
    kern = functools.partial(_out_kernel, d=d, alpha=alpha, nbl=nbl, nblocks=nblocks)
    cur = lambda j: jnp.minimum(j, nblocks - 1)
    prv = lambda j: jnp.maximum(j - 1, 0)
    tok = lambda w: pl.BlockSpec((1, tb, w), lambda i, j: (i, prv(j), 0))
    return pl.pallas_call(
        kern,
        grid=(b, nblocks + 1),
        in_specs=[
            pl.BlockSpec((1, tb, ATT_W), lambda i, j: (i, jnp.minimum(cur(j), nbl - 1), 0)),
            pl.BlockSpec((1, tb, ATT_W), lambda i, j: (i, jnp.maximum(cur(j) - nbl, 0), 0)),
            pl.BlockSpec((1, tb, LRU_W), lambda i, j: (i, cur(j), 0)),
            pl.BlockSpec((tb, FNET_W), lambda i, j: (jnp.minimum(cur(j), nbl - 1), i)),
            pl.BlockSpec((tb, FNET_W), lambda i, j: (jnp.maximum(cur(j) - nbl, 0), i)),
            tok(d),
            pl.BlockSpec((1, 1, 6 * d), lambda i, j: (2 * i + (prv(j) >= nbl).astype(I32), 0, 0)),
            pl.BlockSpec((1, d, d), lambda i, j: (l, 0, 0)),
            pl.BlockSpec((1, 1, d), lambda i, j: (l, 0, 0)),
            pl.BlockSpec((1, 1, d), lambda i, j: (l, 0, 0)),
            pl.BlockSpec((1, N_EXPERTS, d), lambda i, j: (l, 0, 0)),
        ],
        out_specs=[tok(d), tok(d), pl.BlockSpec((1, N_EXPERTS, tb), lambda i, j: (i, 0, prv(j)))],
        out_shape=[
            jax.ShapeDtypeStruct((b, out_tokens, d), F32),
            jax.ShapeDtypeStruct((b, out_tokens, d), BF16),
            jax.ShapeDtypeStruct((b, N_EXPERTS, out_tokens), F32),
        ],
        scratch_shapes=[pltpu.VMEM((tb, d), F32), pltpu.VMEM((tb, d), F32)],
        compiler_params=_params(("arbitrary", "arbitrary")),
        name="out_proj",
    )(att_l, att_c, y, fy_l, fy_c, xc, mod2, w_out_bf, ln_g, ln_b, wr_t)


def _route_kernel(lg_ref, pos_ref, gate_ref, starts_ref, *, segments, n_slots):
    lg = lg_ref[0]
    lane_id = lax.broadcasted_iota(I32, (N_EXPERTS, LANES), 1)
    starts = jnp.zeros((N_EXPERTS, LANES), I32)
    chunks_per_tile = TOKEN_TILE // LANES
    e = jnp.exp(lg - jnp.max(lg, axis=0, keepdims=True))
    s = e / jnp.sum(e, axis=0, keepdims=True)
    ri = lax.broadcasted_iota(I32, (LANES, LANES), 0)
    ci = lax.broadcasted_iota(I32, (LANES, LANES), 1)
    strict_upper = jnp.where(ri < ci, 1.0, 0.0).astype(BF16)

    for lo, t, cap, base in segments:
        ss = s[:, lo:lo + t]
        bits = pltpu.bitcast(ss, I32)
        capf = float(cap)

        def search(i, thr, bits=bits, capf=capf):
            cand = thr | jnp.left_shift(jnp.int32(1), 30 - i)
            cnt = jnp.sum(jnp.where(bits >= cand, 1.0, 0.0), axis=1, keepdims=True)
            return jnp.where(cnt >= capf, cand, thr)

        thr = lax.fori_loop(0, 31, search, jnp.zeros((N_EXPERTS, 1), I32))
        need = capf - jnp.sum(jnp.where(bits > thr, 1.0, 0.0), axis=1, keepdims=True)
        off_eq = jnp.zeros((N_EXPERTS, 1), F32)
        off_sel = jnp.zeros((N_EXPERTS, 1), F32)
        for c in range(t // LANES):
            sl = slice(c * LANES, (c + 1) * LANES)
            tile, sub = divmod(lo // LANES + c, chunks_per_tile)
            if sub == 0:
                starts = jnp.where(lane_id == tile, off_sel.astype(I32) + base, starts)
            bits_c = bits[:, sl]
            eq = bits_c == thr
            eq_c = jnp.where(eq, 1.0, 0.0)
            rank_eq = jnp.dot(eq_c.astype(BF16), strict_upper, preferred_element_type=F32) + off_eq
            off_eq = off_eq + jnp.sum(eq_c, axis=1, keepdims=True)
            sel = jnp.logical_or(bits_c > thr, jnp.logical_and(eq, rank_eq < need))
            sel_c = jnp.where(sel, 1.0, 0.0)
            slot = jnp.dot(sel_c.astype(BF16), strict_upper, preferred_element_type=F32) + off_sel
            off_sel = off_sel + jnp.sum(sel_c, axis=1, keepdims=True)
            osl = slice(sub * LANES, (sub + 1) * LANES)
            pos_ref[0, tile, :, osl] = jnp.where(sel, slot.astype(I32) + base, -1)
            gate_ref[0, tile, :, osl] = jnp.where(sel, ss[:, sl], 0.0)

    n_tiles = lg.shape[1] // TOKEN_TILE
    starts_ref[0] = jnp.where(lane_id == n_tiles, n_slots, starts)


def _route(logits_t, *, segments, n_slots):
    b, ne, tt = logits_t.shape
    nt = tt // TOKEN_TILE
    assert nt < LANES and all(lo % TOKEN_TILE == 0 and t % TOKEN_TILE == 0 for lo, t, _, _ in segments)
    kern = functools.partial(_route_kernel, segments=segments, n_slots=n_slots)
    tiled = lambda: pl.BlockSpec((1, nt, ne, TOKEN_TILE), lambda i: (i, 0, 0, 0))
    return pl.pallas_call(
        kern,
        grid=(b,),
        in_specs=[pl.BlockSpec((1, ne, tt), lambda i: (i, 0, 0))],
        out_specs=[tiled(), tiled(), pl.BlockSpec((1, ne, LANES), lambda i: (i, 0, 0))],
        out_shape=[jax.ShapeDtypeStruct((b, nt, ne, TOKEN_TILE), I32),
                   jax.ShapeDtypeStruct((b, nt, ne, TOKEN_TILE), F32),
                   jax.ShapeDtypeStruct((b, ne, LANES), I32)],
        compiler_params=_params(("arbitrary",)),
        name="route",
    )(logits_t)


def _window_start(starts_ref, base_idx, tile, w, win, limit):
    c0 = starts_ref[base_idx + tile]
    lo = ((c0 >> 4) << 4) + w * win
    return lo, pl.multiple_of(jnp.minimum(lo, limit), BF16_ROWS)


def _window_count(starts_ref, base_idx, tiles, win):
    nw = jnp.int32(1)
    for tile in tiles:
        c0 = starts_ref[base_idx + tile]
        c1 = starts_ref[base_idx + tile + 1]
        nw = jnp.maximum(nw, (c1 - ((c0 >> 4) << 4) + win - 1) // win)
    return nw


def _gather_kernel(starts_ref, h_ref, pos_ref, xs_ref, acc_ref, *, slots, win, group):
    b = pl.program_id(0)
    eg = pl.program_id(1)
    nt = pos_ref.shape[1]
    acc_ref[...] = jnp.zeros_like(acc_ref)
    rel = lax.broadcasted_iota(I32, (win, TOKEN_TILE), 0)
    bases = [(b * N_EXPERTS + eg * group + i) * LANES for i in range(group)]
    nw = jnp.int32(1)
    for i in range(group):
        nw = jnp.maximum(nw, _window_count(starts_ref, bases[i], range(nt), win))

    def window_pass(w, carry):
        for tile in range(nt):
            onehots, offs = [], []
            for i in range(group):
                lo, start = _window_start(starts_ref, bases[i], tile, w, win, slots)
                prow = pos_ref[0, tile, pl.ds(eg * group + i, 1), :] - start
                onehots.append(jnp.where(rel == prow, 1.0, 0.0).astype(BF16))
                offs.append(start)
            res = jnp.dot(jnp.concatenate(onehots, axis=0), h_ref[0, tile * TOKEN_TILE:(tile + 1) * TOKEN_TILE, :],
                          preferred_element_type=F32)
            for i in range(group):
                acc_ref[i, pl.ds(offs[i], win), :] += res[i * win:(i + 1) * win].astype(BF16)
        return carry

    lax.fori_loop(0, nw, window_pass, 0)
    xs_ref[...] = acc_ref[:, 0:slots, :]


def _gather(starts, h2, pos, *, slots, win, group=8):
    b, tt, d = h2.shape
    nt = tt // TOKEN_TILE
    kern = functools.partial(_gather_kernel, slots=slots, win=win, group=group)
    return pl.pallas_call(
        kern,
        grid_spec=pltpu.PrefetchScalarGridSpec(
            num_scalar_prefetch=1,
            grid=(b, N_EXPERTS // group),
            in_specs=[
                pl.BlockSpec((1, tt, d), lambda i, g, s: (i, 0, 0)),
                pl.BlockSpec((1, nt, N_EXPERTS, TOKEN_TILE), lambda i, g, s: (i, 0, 0, 0)),
            ],
            out_specs=pl.BlockSpec((group, slots, d), lambda i, g, s: (g, i, 0)),
            scratch_shapes=[pltpu.VMEM((group, slots + win, d), BF16)],
        ),
        out_shape=jax.ShapeDtypeStruct((N_EXPERTS, b * slots, d), BF16),
        compiler_params=_params(("arbitrary", "arbitrary")),
        name="moe_gather",
    )(starts, h2, pos)


def _ffn_kernel(xs_ref, wg_ref, wu_ref, wd_ref, ys_ref, *, fchunk):
    xs = xs_ref[0]
    f = wg_ref.shape[-1]
    acc = None
    for c in range(f // fchunk):
        sl = slice(c * fchunk, (c + 1) * fchunk)
        a = jnp.dot(xs, wg_ref[0, 0, :, sl], preferred_element_type=F32)
        u = jnp.dot(xs, wu_ref[0, 0, :, sl], preferred_element_type=F32)
        hm = (a * jax.nn.sigmoid(a) * u).astype(BF16)
        y = jnp.dot(hm, wd_ref[0, 0, sl, :], preferred_element_type=F32)
        acc = y if acc is None else acc + y
    ys_ref[0] = acc.astype(BF16)


def _expert_ffn(xs, wg, wu, wd, l, *, slots):
    ne, rows, d = xs.shape
    f = wg.shape[-1]
    nb = rows // slots
    kern = functools.partial(_ffn_kernel, fchunk=_largest_divisor(f, (512,)))
    return pl.pallas_call(
        kern,
        grid=(ne, nb),
        in_specs=[
            pl.BlockSpec((1, slots, d), lambda e, i: (e, i, 0)),
            pl.BlockSpec((1, 1, d, f), lambda e, i: (l, e, 0, 0)),
            pl.BlockSpec((1, 1, d, f), lambda e, i: (l, e, 0, 0)),
            pl.BlockSpec((1, 1, f, d), lambda e, i: (l, e, 0, 0)),
        ],
        out_specs=pl.BlockSpec((1, slots, d), lambda e, i: (e, i, 0)),
        out_shape=jax.ShapeDtypeStruct((ne, rows, d), BF16),
        compiler_params=_params(("arbitrary", "arbitrary")),
        name="expert_ffn",
    )(xs, wg, wu, wd)


def _combine_kernel(starts_ref, ys_ref, pos_ref, gate_ref, x1_ref, mod_ref, g_ref, b_ref, o_ref,
                    *, slots, win, d, alpha):
    b = pl.program_id(0)
    tile = pl.program_id(1)
    rel = lax.broadcasted_iota(I32, (win, TOKEN_TILE), 0)
    bases = [(b * N_EXPERTS + e) * LANES for e in range(N_EXPERTS)]
    nw = jnp.int32(1)
    for e in range(N_EXPERTS):
        c0 = starts_ref[bases[e] + tile]
        c1 = starts_ref[bases[e] + tile + 1]
        nw = jnp.maximum(nw, (c1 - ((c0 >> 4) << 4) + win - 1) // win)

    def window_pass(w, moe):
        gated, rows = [], []
        for e in range(N_EXPERTS):
            lo, start = _window_start(starts_ref, bases[e], tile, w, win, slots - win)
            prow = pos_ref[0, 0, e:e + 1, :]
            prow = jnp.where(jnp.logical_and(prow >= lo, prow < lo + win), prow - start, -1)
            gated.append(jnp.where(rel == prow, gate_ref[0, 0, e:e + 1, :], 0.0).astype(BF16))
            rows.append(ys_ref[e, pl.ds(start, win), :])
        return moe + lax.dot_general(jnp.concatenate(gated, axis=0), jnp.concatenate(rows, axis=0),
                                     (((0,), (0,)), ((), ())), preferred_element_type=F32)

    moe = lax.fori_loop(0, nw, window_pass, jnp.zeros((TOKEN_TILE, d), F32))
    m = mod_ref[0]
    o_ref[0] = _ln(alpha * x1_ref[0] + m[:, 5 * d:6 * d] * moe) * g_ref[0] + b_ref[0]


def _combine(starts, ys, pos, gate, x1, mod2, ln_g, ln_b, l, *, slots, win, n_lat_tiles, alpha, out_tokens):
    b, _, d = x1.shape
    kern = functools.partial(_combine_kernel, slots=slots, win=win, d=d, alpha=alpha)
    tiled = lambda: pl.BlockSpec((1, 1, N_EXPERTS, TOKEN_TILE), lambda i, t, s: (i, t, 0, 0))
    tok = lambda: pl.BlockSpec((1, TOKEN_TILE, d), lambda i, t, s: (i, t, 0))
    return pl.pallas_call(
        kern,
        grid_spec=pltpu.PrefetchScalarGridSpec(
            num_scalar_prefetch=1,
            grid=(b, out_tokens // TOKEN_TILE),
            in_specs=[
                pl.BlockSpec((N_EXPERTS, slots, d), lambda i, t, s: (0, i, 0)),
                tiled(), tiled(), tok(),
                pl.BlockSpec((1, 1, 6 * d), lambda i, t, s: (2 * i + (t >= n_lat_tiles).astype(I32), 0, 0)),
                pl.BlockSpec((1, 1, d), lambda i, t, s: (l, 0, 0)),
                pl.BlockSpec((1, 1, d), lambda i, t, s: (l, 0, 0)),
            ],
            out_specs=tok(),
        ),
        out_shape=jax.ShapeDtypeStruct((b, out_tokens, d), F32),
        compiler_params=_params(("arbitrary", "arbitrary")),
        name="moe_combine",
    )(starts, ys, pos, gate, x1, mod2, ln_g, ln_b)


def _rope_tables(n, cn):
    lane = np.arange(LANES)
    within = lane % QK_DIM
    use_col = (within // 32) == 1
    first_half = (within % 32) < 16
    inv = ROPE_BASE ** (-(within % 16).astype(np.float64) / 16.0)
    pos = np.arange(n)
    coord = np.where(use_col[None, :], (pos % GRID_W)[:, None], (pos // GRID_W)[:, None]).astype(np.float32)
    ang = jnp.asarray(coord) * jnp.asarray(inv.astype(np.float32))[None, :]
    cos = jnp.cos(ang)
    sin = jnp.where(jnp.asarray(first_half)[None, :], -jnp.sin(ang), jnp.sin(ang))
    cos = jnp.concatenate([cos, jnp.ones((cn, LANES), F32)], axis=0)
    sin = jnp.concatenate([sin, jnp.zeros((cn, LANES), F32)], axis=0)
    return cos, sin


def _channel_dft():
    idx = np.arange(FNET_W)
    same = (idx[:, None] // FNET_GROUP_W) == (idx[None, :] // FNET_GROUP_W)
    ang = 2.0 * np.pi * ((idx[:, None] % FNET_GROUP_W) * (idx[None, :] % FNET_GROUP_W) % FNET_GROUP_W) / FNET_GROUP_W
    cs = np.concatenate([np.where(same, np.cos(ang), 0.0), np.where(same, np.sin(ang), 0.0)], axis=1)
    return jnp.asarray(cs.astype(np.float32)).astype(BF16)


def _position_dft_mats(t):
    g = math.gcd(t, 64)
    kk = jnp.arange(t, dtype=I32)

    def table(m):
        ph = ((kk[:, None] * m[None, :]) % t).astype(F32) * (2.0 * math.pi / t)
        return jnp.cos(ph), jnp.sin(ph)

    ch, sh = table(jnp.arange(t // g, dtype=I32) * g)
    cl, sl = table(jnp.arange(g, dtype=I32))
    scale = 1.0 / math.sqrt(t * FNET_GROUP_W)
    cmat = (ch[:, :, None] * cl[:, None, :] - sh[:, :, None] * sl[:, None, :]).reshape(t, t) * scale
    smat = (sh[:, :, None] * cl[:, None, :] + ch[:, :, None] * sl[:, None, :]).reshape(t, t) * (-scale)
    return cmat.astype(BF16), smat.astype(BF16)


def _block_diag_gates(wa, wx):
    depth = wa.shape[0]

    def dense(wb):
        eye = jnp.eye(LRU_BLOCKS, dtype=wb.dtype)
        return jnp.einsum('lncd,nm->lncmd', wb, eye).reshape(depth, LRU_W, LRU_W)

    halves = []
    for hh in range(2):
        sl = slice(hh * LANES, (hh + 1) * LANES)
        cols = [dense(wmat[:, dr])[:, sl, sl] for dr in range(2) for wmat in (wa, wx)]
        halves.append(jnp.concatenate(cols, axis=-1))
    return jnp.stack(halves, axis=1).reshape(depth * 2, LANES, 4 * LANES).astype(BF16)


def _gate_bias(ba, bx):
    depth = ba.shape[0]
    halves = []
    for hh in range(2):
        sl = slice(hh * LANES, (hh + 1) * LANES)
        halves.append(jnp.concatenate([bvec[:, dr, sl] for dr in range(2) for bvec in (ba, bx)], axis=-1))
    return jnp.stack(halves, axis=1).reshape(depth * 2, 1, 4 * LANES)


def _split_halves(a):
    depth, r, _ = a.shape
    return a.reshape(depth, r, 2, LANES).transpose(0, 2, 1, 3).reshape(depth * 2, r, LANES)


def kernel(x, c, ctx, c_ctx, w_mod, b_mod, w_in, lam_q1, lam_k1, lam_q2, lam_k2, attn_norm_g, conv_w, conv_b, lru_wa, lru_ba, lru_wx, lru_bx, lru_lam, w_out, ln1_g, ln1_b, w_router, w_gate, w_up, w_down, ln2_g, ln2_b):
    b, n, d = x.shape
    cn = ctx.shape[1]
    depth = w_mod.shape[0]
    tt = n + cn
    tb = _largest_divisor(math.gcd(n, cn), (256, 128))
    assert n % tb == 0 and cn % tb == 0 and n % GRID_W == 0 and tt % LANES == 0
    nbl = n // tb
    cap_l = CAPACITY_FACTOR * n // N_EXPERTS
    cap_c = CAPACITY_FACTOR * cn // N_EXPERTS
    assert cap_l % BF16_ROWS == 0 and cap_c % BF16_ROWS == 0 and n % TOKEN_TILE == 0 and cn % TOKEN_TILE == 0
    alpha = (2 * depth) ** 0.25

    rows = -(-(b + 1) // SUBLANES) * SUBLANES
    cc = jnp.concatenate([c, c_ctx[None, :], jnp.zeros((rows - b - 1, d), F32)], axis=0)
    mod = _modulation(cc, w_mod, b_mod)

    cos_t, sin_t = _rope_tables(n, cn)
    cs = _channel_dft()
    dft_l = _position_dft_mats(n)
    dft_c = _position_dft_mats(cn)

    w_in_bf = w_in.astype(BF16)
    w_out_bf = w_out.astype(BF16)
    wg_bf = w_gate.astype(BF16)
    wu_bf = w_up.astype(BF16)
    wd_bf = w_down.astype(BF16)
    wr_t = jnp.swapaxes(w_router, 1, 2).astype(BF16)
    lamv = jnp.stack([lam_q1, lam_k1, lam_q2, lam_k2], axis=1).astype(F32)
    lam_init = np.array([0.8 - 0.6 * math.exp(-0.3 * l) for l in range(depth)], np.float32)
    lin = jnp.asarray(np.broadcast_to(lam_init[:, None, None], (depth, 1, LANES)).copy())
    gain = attn_norm_g.reshape(depth * N_HEADS, 1, HEAD_V)
    conv_w2 = _split_halves(conv_w)
    conv_b2 = _split_halves(conv_b[:, None, :])
    lam2 = _split_halves(lru_lam)
    wgate = _block_diag_gates(lru_wa, lru_wx)
    bgate = _gate_bias(lru_ba, lru_bx)
    ln1g, ln1b = ln1_g[:, None, :], ln1_b[:, None, :]
    ln2g, ln2b = ln2_g[:, None, :], ln2_b[:, None, :]

    xc = jnp.concatenate([x, ctx], axis=1)
    for l in range(depth):
        ml = mod[l]
        mod2 = jnp.stack([ml[:b], jnp.broadcast_to(ml[b][None, :], (b, 6 * d))], axis=1).reshape(2 * b, 1, 6 * d)
        qx, k, v, ux, gg, gc, gs = _in_proj(xc, mod2, w_in_bf, l, cos_t, sin_t, cs, nbl=nbl, tb=tb)
        keep_ctx = l < depth - 1
        tokens = tt if keep_ctx else n
        att_l, att_c = _attention(qx, k, v, lamv, lin, gain, l, n=n, tb=tb)
        y = _rglru(ux, gg, conv_w2, conv_b2, wgate, bgate, lam2, l, n=n, cn=cn, tb=tb)
        fy_l = _position_dft(*dft_l, gc, gs, row0=0)
        fy_c = _position_dft(*dft_c, gc, gs, row0=n) if keep_ctx else fy_l
        x1, h2, logits_t = _out_proj(att_l, att_c, y, fy_l, fy_c, xc, mod2, w_out_bf, ln1g, ln1b, wr_t, l,
                                     nbl=nbl, tb=tb, alpha=alpha, out_tokens=tokens)
        segments = ((0, n, cap_l, 0), (n, cn, cap_c, cap_l)) if keep_ctx else ((0, n, cap_l, 0),)
        slots = cap_l + cap_c if keep_ctx else cap_l
        win = min(SLOT_WINDOW, slots)
        pos, gate, starts = _route(logits_t, segments=segments, n_slots=slots)
        starts = starts.reshape(-1)
        xs = _gather(starts, h2, pos, slots=slots, win=win)
        ys = _expert_ffn(xs, wg_bf, wu_bf, wd_bf, l, slots=slots)
        xc = _combine(starts, ys, pos, gate, x1, mod2, ln2g, ln2b, l, slots=slots, win=win,
                      n_lat_tiles=n // TOKEN_TILE, alpha=alpha, out_tokens=tokens)
    return xc
```

```python
import functools
import math

import numpy as np
import jax
import jax.numpy as jnp
from jax import lax
from jax.experimental import pallas as pl
from jax.experimental.pallas import tpu as pltpu

F32 = jnp.float32
BF16 = jnp.bfloat16
I32 = jnp.int32

GRID_W = 64
QK_DIM = 64
N_HEADS = 4
HEAD_V = 128
QK_W = 512
ATT_W = 512
LRU_W = 256
LRU_BLOCKS = 4
LRU_BLOCK_W = 64
LRU_C = 8.0
FNET_W = 256
FNET_GROUP_W = 64
IN_W = 2304
ROPE_BASE = 10000.0
N_EXPERTS = 16
CAPACITY_FACTOR = 2
LN_EPS = 1e-5
RMS_EPS = 1e-6
GELU_C = math.sqrt(2.0 / math.pi)
LOG2E = math.log2(math.e)

LANES = 128
SUBLANES = 8
BF16_ROWS = 16
KEY_CHUNK = 256
OUT_PIECES = 4
SCAN_UNROLL = 4
TOKEN_TILE = 256
SLOT_WINDOW = 64
VMEM_LIMIT = 56 << 20


def _params(sem, vmem=VMEM_LIMIT):
    return pltpu.CompilerParams(dimension_semantics=sem, vmem_limit_bytes=vmem)


def _ln(x):
    mu = jnp.mean(x, axis=-1, keepdims=True)
    xc = x - mu
    var = jnp.mean(xc * xc, axis=-1, keepdims=True)
    return xc * lax.rsqrt(var + LN_EPS)


def _largest_divisor(n, candidates):
    for c in candidates:
        if c <= n and n % c == 0:
            return c
    return n


def _mod_kernel(c_ref, w_ref, b_ref, o_ref):
    c = c_ref[...]
    s = c * jax.nn.sigmoid(c)
    o_ref[0] = jnp.dot(s, w_ref[0], precision=lax.Precision.HIGHEST, preferred_element_type=F32) + b_ref[0]


def _modulation(cc, w_mod, b_mod):
    depth, d, d6 = w_mod.shape
    rows = cc.shape[0]
    tn = 1024
    return pl.pallas_call(
        _mod_kernel,
        grid=(depth, d6 // tn),
        in_specs=[
            pl.BlockSpec((rows, d), lambda l, n: (0, 0)),
            pl.BlockSpec((1, d, tn), lambda l, n: (l, 0, n)),
            pl.BlockSpec((1, 1, tn), lambda l, n: (l, 0, n)),
        ],
        out_specs=pl.BlockSpec((1, rows, tn), lambda l, n: (l, 0, n)),
        out_shape=jax.ShapeDtypeStruct((depth, rows, d6), F32),
        compiler_params=_params(("arbitrary", "arbitrary")),
        name="modulation",
    )(cc, w_mod, b_mod.reshape(depth, 1, d6))


def _in_kernel(x_ref, mod_ref, w_ref, cos_ref, sin_ref, cs_ref,
               qx_ref, k_ref, v_ref, ux_ref, gg_ref, gc_ref, gs_ref, *, d):
    x = x_ref[0]
    m = mod_ref[0]
    h = _ln(x) * (1.0 + m[:, d:2 * d]) + m[:, 0:d]
    z = jnp.dot(h.astype(BF16), w_ref[0], preferred_element_type=F32)

    tb = x.shape[0]
    cos = cos_ref[...]
    sin = sin_ref[...]
    lane = lax.broadcasted_iota(I32, (tb, LANES), 1)
    first_half = (lane & 31) < 16
    low_map = lane < QK_DIM

    def rope(t):
        partner = jnp.where(first_half, pltpu.roll(t, LANES - 16, 1), pltpu.roll(t, 16, 1))
        return t * cos + partner * sin

    for p in range(N_HEADS):
        qp = rope(z[:, p * LANES:(p + 1) * LANES] * (QK_DIM ** -0.5 * LOG2E))
        qx_ref[0, :, (2 * p) * LANES:(2 * p + 1) * LANES] = jnp.where(low_map, qp, 0.0).astype(BF16)
        qx_ref[0, :, (2 * p + 1) * LANES:(2 * p + 2) * LANES] = jnp.where(low_map, 0.0, qp).astype(BF16)
        kp = rope(z[:, QK_W + p * LANES:QK_W + (p + 1) * LANES])
        k_ref[0, :, p * LANES:(p + 1) * LANES] = kp.astype(BF16)

    o = 2 * QK_W
    v_ref[0] = z[:, o:o + ATT_W].astype(BF16)
    o += ATT_W
    ux_ref[0] = z[:, o:o + LRU_W]
    o += LRU_W
    g = z[:, o:o + LRU_W]
    gg_ref[0] = 0.5 * g * (1.0 + jnp.tanh(GELU_C * (g + 0.044715 * (g * g * g))))
    o += LRU_W
    uf = z[:, o:o + FNET_W].astype(BF16)
    gcs = jnp.dot(uf, cs_ref[...], preferred_element_type=F32)
    gc_ref[...] = gcs[:, :FNET_W].astype(BF16)
    gs_ref[...] = gcs[:, FNET_W:].astype(BF16)


def _in_proj(xc, mod2, w_in_bf, l, cos_t, sin_t, cs, *, nbl, tb):
    b, tt, d = xc.shape
    nbt = tt // tb
    kern = functools.partial(_in_kernel, d=d)
    tok = lambda w: pl.BlockSpec((1, tb, w), lambda i, j: (i, j, 0))
    return pl.pallas_call(
        kern,
        grid=(b, nbt),
        in_specs=[
            tok(d),
            pl.BlockSpec((1, 1, 6 * d), lambda i, j: (2 * i + (j >= nbl).astype(I32), 0, 0)),
            pl.BlockSpec((1, d, IN_W), lambda i, j: (l, 0, 0)),
            pl.BlockSpec((tb, LANES), lambda i, j: (j, 0)),
            pl.BlockSpec((tb, LANES), lambda i, j: (j, 0)),
            pl.BlockSpec((FNET_W, 2 * FNET_W), lambda i, j: (0, 0)),
        ],
        out_specs=[
            tok(2 * QK_W), tok(QK_W), tok(ATT_W), tok(LRU_W), tok(LRU_W),
            pl.BlockSpec((tb, FNET_W), lambda i, j: (j, i)),
            pl.BlockSpec((tb, FNET_W), lambda i, j: (j, i)),
        ],
        out_shape=[
            jax.ShapeDtypeStruct((b, tt, 2 * QK_W), BF16),
            jax.ShapeDtypeStruct((b, tt, QK_W), BF16),
            jax.ShapeDtypeStruct((b, tt, ATT_W), BF16),
            jax.ShapeDtypeStruct((b, tt, LRU_W), F32),
            jax.ShapeDtypeStruct((b, tt, LRU_W), F32),
            jax.ShapeDtypeStruct((tt, b * FNET_W), BF16),
            jax.ShapeDtypeStruct((tt, b * FNET_W), BF16),
        ],
        compiler_params=_params(("arbitrary", "arbitrary")),
        name="in_proj",
    )(xc, mod2, w_in_bf, cos_t, sin_t, cs)


def _attn_kernel(ql_ref, qc_ref, k_ref, v_ref, lamv_ref, lin_ref, g_ref, ol_ref, oc_ref, vt_ref, s0_ref, s1_ref,
                 *, n, lat_steps, qb):
    j = pl.program_id(2)
    lv = lamv_ref[0]
    lam_init = lin_ref[0][:, 0:1]
    lam = (jnp.exp(jnp.sum(lv[0:1] * lv[1:2], axis=1, keepdims=True))
           - jnp.exp(jnp.sum(lv[2:3] * lv[3:4], axis=1, keepdims=True)) + lam_init)
    gain = g_ref[0] * (1.0 - lam_init)
    nt = (((1,), (1,)), ((), ()))
    tt = k_ref.shape[1]

    @pl.when(j == 0)
    def _():
        vt_ref[0:HEAD_V, :] = v_ref[0].astype(F32).T.astype(BF16)
        row = lax.broadcasted_iota(I32, (BF16_ROWS, tt), 0)
        vt_ref[HEAD_V:, :] = jnp.where(row == 0, 1.0, 0.0).astype(BF16)

    def attend(q_ref, o_ref, blocks, lo):
        chunks = [(c, min(KEY_CHUNK, tt - c)) for c in range(lo, tt, KEY_CHUNK)]
        tq = s0_ref.shape[1]
        streams = [(blk, mp) for blk in range(blocks) for mp in range(2)]
        s_refs = (s0_ref, s1_ref)

        def scores(i, c, w, m):
            blk, mp = streams[i]
            q = q_ref[0, blk * tq:(blk + 1) * tq, mp * LANES:(mp + 1) * LANES]
            st = lax.dot_general(k_ref[0, c:c + w, :], q, nt, preferred_element_type=F32)
            s_refs[i % 2][c:c + w, :] = st
            mc = jnp.max(st, axis=0, keepdims=True)
            return mc if m is None else jnp.maximum(m, mc)

        def weighted_values(i, c, w, m, acc):
            pt = jnp.exp2(s_refs[i % 2][c:c + w, :] - m).astype(BF16)
            part = jnp.dot(vt_ref[:, c:c + w], pt, preferred_element_type=F32)
            return part if acc is None else acc + part

        ms = [None] * len(streams)
        accs = [None] * len(streams)
        for stage in range(len(streams) + 1):
            for c, w in chunks:
                if stage >= 1:
                    accs[stage - 1] = weighted_values(stage - 1, c, w, ms[stage - 1], accs[stage - 1])
                if stage < len(streams):
                    ms[stage] = scores(stage, c, w, ms[stage])
            if stage >= 2 and stage % 2 == 0:
                blk = stage // 2 - 1
                a0, a1 = accs[stage - 2], accs[stage - 1]
                ot = (a0[0:HEAD_V] / a0[HEAD_V:HEAD_V + 1]) - lam * (a1[0:HEAD_V] / a1[HEAD_V:HEAD_V + 1])
                rt = ot * lax.rsqrt(jnp.mean(ot * ot, axis=0, keepdims=True) + RMS_EPS)
                o_ref[0, blk * tq:(blk + 1) * tq, :] = (rt.T * gain).astype(BF16)

    @pl.when(j < lat_steps)
    def _():
        attend(ql_ref, ol_ref, qb, 0)

    @pl.when(j >= lat_steps)
    def _():
        attend(qc_ref, oc_ref, qc_ref.shape[1] // s0_ref.shape[1], n)


def _attention(qx, k, v, lamv, lin, gain, l, *, n, tb):
    b, tt, _ = k.shape
    cn = tt - n
    qb = _largest_divisor(n // tb, (16, 8, 4, 2, 1))
    lat_steps = n // (qb * tb)
    assert n % cn == 0 and cn % tb == 0
    kern = functools.partial(_attn_kernel, n=n, lat_steps=lat_steps, qb=qb)
    lat = lambda w: pl.BlockSpec((1, qb * tb, w), lambda i, h, j: (i, jnp.minimum(j, lat_steps - 1), h))
    ctx = lambda w: pl.BlockSpec((1, cn, w), lambda i, h, j: (i, n // cn, h))
    return pl.pallas_call(
        kern,
        grid=(b, N_HEADS, lat_steps + 1),
        in_specs=[
            lat(2 * LANES), ctx(2 * LANES),
            pl.BlockSpec((1, tt, LANES), lambda i, h, j: (i, 0, h)),
            pl.BlockSpec((1, tt, LANES), lambda i, h, j: (i, 0, h)),
            pl.BlockSpec((1, 4, QK_DIM), lambda i, h, j: (l, 0, 0)),
            pl.BlockSpec((1, 1, LANES), lambda i, h, j: (l, 0, 0)),
            pl.BlockSpec((1, 1, HEAD_V), lambda i, h, j: (l * N_HEADS + h, 0, 0)),
        ],
        out_specs=[lat(LANES), pl.BlockSpec((1, cn, LANES), lambda i, h, j: (i, 0, h))],
        out_shape=[jax.ShapeDtypeStruct((b, n, ATT_W), BF16), jax.ShapeDtypeStruct((b, cn, ATT_W), BF16)],
        scratch_shapes=[pltpu.VMEM((HEAD_V + BF16_ROWS, tt), BF16),
                        pltpu.VMEM((tt, tb), F32), pltpu.VMEM((tt, tb), F32)],
        compiler_params=_params(("arbitrary", "arbitrary", "arbitrary")),
        name="diff_attention",
    )(qx, qx, k, v, lamv, lin, gain)


def _lru_kernel(ux_ref, gg_ref, cw_ref, cb_ref, wg_ref, bg_ref, lam_ref, y_ref,
                a_f, b_f, a_b, b_b, *, n, cn, r):
    tt = n + cn
    w = LANES
    cw = cw_ref[0]
    cb = cb_ref[0]
    bg = bg_ref[0]
    neg_lam = -lam_ref[0]
    softplus = jnp.maximum(neg_lam, 0.0) + jnp.log(1.0 + jnp.exp(-jnp.abs(neg_lam)))
    row8 = lax.broadcasted_iota(I32, (r // SUBLANES, SUBLANES, w), 1)
    ext_rows = r + 2 * SUBLANES

    def local_scan(a, bb, reverse):
        a = a.reshape(r // SUBLANES, SUBLANES, w)
        bb = bb.reshape(r // SUBLANES, SUBLANES, w)
        for s in (1, 2, 4):
            if reverse:
                a_sh = pltpu.roll(a, SUBLANES - s, 1)
                b_sh = pltpu.roll(bb, SUBLANES - s, 1)
                valid = row8 < SUBLANES - s
            else:
                a_sh = pltpu.roll(a, s, 1)
                b_sh = pltpu.roll(bb, s, 1)
                valid = row8 >= s
            bb = jnp.where(valid, a * b_sh + bb, bb)
            a = jnp.where(valid, a * a_sh, a)
        return a.reshape(r, w), bb.reshape(r, w)

    def gates_chunk(c, carry):
        r0 = pl.multiple_of(c * r, r)
        seg_start = jnp.logical_or(r0 == 0, r0 == n)
        seg_end = jnp.logical_or(r0 + r == n, r0 + r == tt)
        main = ux_ref[0, pl.ds(r0, r), :]
        prev = ux_ref[0, pl.ds(pl.multiple_of(jnp.maximum(r0 - SUBLANES, 0), SUBLANES), SUBLANES), :]
        nxt = ux_ref[0, pl.ds(pl.multiple_of(jnp.minimum(r0 + r, tt - SUBLANES), SUBLANES), SUBLANES), :]
        prev = jnp.where(seg_start, 0.0, prev)
        nxt = jnp.where(seg_end, 0.0, nxt)
        ext = jnp.concatenate([prev, main, nxt], axis=0)
        u = cb
        for t in range(4):
            sh = (2 - t) % ext_rows
            win = ext if sh == 0 else pltpu.roll(ext, sh, 0)
            u = u + cw[t:t + 1, :] * win[SUBLANES:SUBLANES + r, :]
        zz = jnp.dot(u.astype(BF16), wg_ref[0], preferred_element_type=F32) + bg
        for dr, (a_s, b_s) in enumerate(((a_f, b_f), (a_b, b_b))):
            rg = 0.5 + 0.5 * jnp.tanh(0.5 * zz[:, (2 * dr) * w:(2 * dr + 1) * w])
            ig = 0.5 + 0.5 * jnp.tanh(0.5 * zz[:, (2 * dr + 1) * w:(2 * dr + 2) * w])
            a = jnp.exp(-LRU_C * rg * softplus[dr:dr + 1, :])
            bb = jnp.sqrt(1.0 - a * a) * ig * u
            a, bb = local_scan(a, bb, reverse=(dr == 1))
            a_s[pl.ds(r0, r), :] = a
            b_s[pl.ds(r0, r), :] = bb
        return carry

    lax.fori_loop(0, tt // r, gates_chunk, 0)

    def seg_scan(first_tile, ntiles, cf, cbk):
        def body(i, carry):
            cf, cbk = carry
            rf = pl.multiple_of((first_tile + i) * SUBLANES, SUBLANES)
            hf = b_f[pl.ds(rf, SUBLANES), :] + a_f[pl.ds(rf, SUBLANES), :] * cf
            b_f[pl.ds(rf, SUBLANES), :] = hf
            rb = pl.multiple_of((first_tile + ntiles - 1 - i) * SUBLANES, SUBLANES)
            hb = b_b[pl.ds(rb, SUBLANES), :] + a_b[pl.ds(rb, SUBLANES), :] * cbk
            b_b[pl.ds(rb, SUBLANES), :] = hb
            return hf[SUBLANES - 1:SUBLANES, :], hb[0:1, :]
        return lax.fori_loop(0, ntiles, body, (cf, cbk), unroll=SCAN_UNROLL)

    zero = jnp.zeros((1, w), F32)
    cf, cbk = seg_scan(n // SUBLANES, cn // SUBLANES, zero, zero)
    seg_scan(0, n // SUBLANES, cf, cbk)

    def out_chunk(c, carry):
        r0 = pl.multiple_of(c * r, r)
        y = (b_f[pl.ds(r0, r), :] + b_b[pl.ds(r0, r), :]) * gg_ref[0, pl.ds(r0, r), :]
        y_ref[0, pl.ds(r0, r), :] = y.astype(BF16)
        return carry

    lax.fori_loop(0, tt // r, out_chunk, 0)


def _rglru(ux, gg, conv_w2, conv_b2, wgate, bgate, lam2, l, *, n, cn, tb):
    b, tt, _ = ux.shape
    kern = functools.partial(_lru_kernel, n=n, cn=cn, r=tb)
    half = lambda: pl.BlockSpec((1, tt, LANES), lambda i, hh: (i, 0, hh))
    return pl.pallas_call(
        kern,
        grid=(b, 2),
        in_specs=[
            half(), half(),
            pl.BlockSpec((1, 4, LANES), lambda i, hh: (2 * l + hh, 0, 0)),
            pl.BlockSpec((1, 1, LANES), lambda i, hh: (2 * l + hh, 0, 0)),
            pl.BlockSpec((1, LANES, 4 * LANES), lambda i, hh: (2 * l + hh, 0, 0)),
            pl.BlockSpec((1, 1, 4 * LANES), lambda i, hh: (2 * l + hh, 0, 0)),
            pl.BlockSpec((1, 2, LANES), lambda i, hh: (2 * l + hh, 0, 0)),
        ],
        out_specs=half(),
        out_shape=jax.ShapeDtypeStruct((b, tt, LRU_W), BF16),
        scratch_shapes=[pltpu.VMEM((tt, LANES), F32)] * 4,
        compiler_params=_params(("arbitrary", "arbitrary")),
        name="rglru",
    )(ux, gg, conv_w2, conv_b2, wgate, bgate, lam2)


def _dft_kernel(c_ref, s_ref, gc_ref, gs_ref, o_ref, acc_ref):
    kk = pl.program_id(2)

    def part():
        return (jnp.dot(c_ref[...], gc_ref[...], preferred_element_type=F32)
                + jnp.dot(s_ref[...], gs_ref[...], preferred_element_type=F32))

    @pl.when(kk == 0)
    def _():
        acc_ref[...] = part()

    @pl.when(kk > 0)
    def _():
        acc_ref[...] += part()

    @pl.when(kk == pl.num_programs(2) - 1)
    def _():
        o_ref[...] = acc_ref[...].astype(BF16)


def _position_dft(cmat, smat, gc, gs, *, row0):
    t = cmat.shape[0]
    nn = gc.shape[1]
    tm = _largest_divisor(t, (1024, 512, 256, 128))
    tk = _largest_divisor(t, (2048, 1024, 512, 256, 128))
    tn = _largest_divisor(nn, (512, 256))
    assert row0 % tk == 0
    return pl.pallas_call(
        _dft_kernel,
        grid=(t // tm, nn // tn, t // tk),
        in_specs=[
            pl.BlockSpec((tm, tk), lambda i, j, k: (i, k)),
            pl.BlockSpec((tm, tk), lambda i, j, k: (i, k)),
            pl.BlockSpec((tk, tn), lambda i, j, k: (row0 // tk + k, j)),
            pl.BlockSpec((tk, tn), lambda i, j, k: (row0 // tk + k, j)),
        ],
        out_specs=pl.BlockSpec((tm, tn), lambda i, j, k: (i, j)),
        out_shape=jax.ShapeDtypeStruct((t, nn), BF16),
        scratch_shapes=[pltpu.VMEM((tm, tn), F32)],
        compiler_params=_params(("arbitrary", "arbitrary", "arbitrary")),
        name="position_dft",
    )(cmat, smat, gc, gs)


def _out_kernel(attl_ref, attc_ref, y_ref, fl_ref, fc_ref, x_ref, mod_ref, w_ref, g_ref, b_ref, wr_ref,
                x1_ref, h2_ref, lg_ref, mix0_ref, mix1_ref, *, d, alpha, nbl, nblocks):
    j = pl.program_id(1)

    @pl.when(j == 0)
    def _():
        mix1_ref[...] = jnp.zeros_like(mix1_ref)

    def step(new_ref, old_ref):
        m = mod_ref[0]
        tb = x_ref.shape[1]
        is_ctx = jnp.minimum(j, nblocks - 1) >= nbl
        att = jnp.where(is_ctx, attc_ref[0], attl_ref[0])
        fy = jnp.where(is_ctx, fc_ref[...], fl_ref[...])
        for c in range(OUT_PIECES):
            rows = slice(c * tb // OUT_PIECES, (c + 1) * tb // OUT_PIECES)
            cols = slice(c * d // OUT_PIECES, (c + 1) * d // OUT_PIECES)
            x1 = _ln(alpha * x_ref[0, rows, :] + m[:, 2 * d:3 * d] * old_ref[rows, :]) * g_ref[0] + b_ref[0]
            x1_ref[0, rows, :] = x1
            h2_ref[0, rows, :] = (_ln(x1) * (1.0 + m[:, 4 * d:5 * d]) + m[:, 3 * d:4 * d]).astype(BF16)
            new_ref[:, cols] = (
                jnp.dot(att, w_ref[0, 0:ATT_W, cols], preferred_element_type=F32)
                + jnp.dot(y_ref[0], w_ref[0, ATT_W:ATT_W + LRU_W, cols], preferred_element_type=F32)
                + jnp.dot(fy, w_ref[0, ATT_W + LRU_W:, cols], preferred_element_type=F32))
        lg_ref[0] = lax.dot_general(wr_ref[0], h2_ref[0], (((1,), (1,)), ((), ())), preferred_element_type=F32)

    @pl.when(j % 2 == 0)
    def _():
        step(mix0_ref, mix1_ref)

    @pl.when(j % 2 == 1)
    def _():
        step(mix1_ref, mix0_ref)


def _out_proj(att_l, att_c, y, fy_l, fy_c, xc, mod2, w_out_bf, ln_g, ln_b, wr_t, l, *, nbl, tb, alpha, out_tokens):
    b, _, d = xc.shape
    nblocks = out_tokens // tb
    kern = functools.partial(_out_kernel, d=d, alpha=alpha, nbl=nbl, nblocks=nblocks)
    cur = lambda j: jnp.minimum(j, nblocks - 1)
    prv = lambda j: jnp.maximum(j - 1, 0)
    tok = lambda w: pl.BlockSpec((1, tb, w), lambda i, j: (i, prv(j), 0))
    return pl.pallas_call(
        kern,
        grid=(b, nblocks + 1),
        in_specs=[
            pl.BlockSpec((1, tb, ATT_W), lambda i, j: (i, jnp.minimum(cur(j), nbl - 1), 0)),
            pl.BlockSpec((1, tb, ATT_W), lambda i, j: (i, jnp.maximum(cur(j) - nbl, 0), 0)),
            pl.BlockSpec((1, tb, LRU_W), lambda i, j: (i, cur(j), 0)),
            pl.BlockSpec((tb, FNET_W), lambda i, j: (jnp.minimum(cur(j), nbl - 1), i)),
            pl.BlockSpec((tb, FNET_W), lambda i, j: (jnp.maximum(cur(j) - nbl, 0), i)),
            tok(d),
            pl.BlockSpec((1, 1, 6 * d), lambda i, j: (2 * i + (prv(j) >= nbl).astype(I32), 0, 0)),
            pl.BlockSpec((1, d, d), lambda i, j: (l, 0, 0)),
            pl.BlockSpec((1, 1, d), lambda i, j: (l, 0, 0)),
            pl.BlockSpec((1, 1, d), lambda i, j: (l, 0, 0)),
            pl.BlockSpec((1, N_EXPERTS, d), lambda i, j: (l, 0, 0)),
        ],
        out_specs=[tok(d), tok(d), pl.BlockSpec((1, N_EXPERTS, tb), lambda i, j: (i, 0, prv(j)))],
        out_shape=[
            jax.ShapeDtypeStruct((b, out_tokens, d), F32),
            jax.ShapeDtypeStruct((b, out_tokens, d), BF16),
            jax.ShapeDtypeStruct((b, N_EXPERTS, out_tokens), F32),
        ],
        scratch_shapes=[pltpu.VMEM((tb, d), F32), pltpu.VMEM((tb, d), F32)],
        compiler_params=_params(("arbitrary", "arbitrary")),
        name="out_proj",
    )(att_l, att_c, y, fy_l, fy_c, xc, mod2, w_out_bf, ln_g, ln_b, wr_t)


def _route_kernel(lg_ref, pos_ref, gate_ref, starts_ref, *, segments, n_slots):
    lg = lg_ref[0]
    lane_id = lax.broadcasted_iota(I32, (N_EXPERTS, LANES), 1)
    starts = jnp.zeros((N_EXPERTS, LANES), I32)
    chunks_per_tile = TOKEN_TILE // LANES
    e = jnp.exp(lg - jnp.max(lg, axis=0, keepdims=True))
    s = e / jnp.sum(e, axis=0, keepdims=True)
    ri = lax.broadcasted_iota(I32, (LANES, LANES), 0)
    ci = lax.broadcasted_iota(I32, (LANES, LANES), 1)
    strict_upper = jnp.where(ri < ci, 1.0, 0.0).astype(BF16)

    for lo, t, cap, base in segments:
        ss = s[:, lo:lo + t]
        bits = pltpu.bitcast(ss, I32)
        capf = float(cap)

        def search(i, thr, bits=bits, capf=capf):
            cand = thr | jnp.left_shift(jnp.int32(1), 30 - i)
            cnt = jnp.sum(jnp.where(bits >= cand, 1.0, 0.0), axis=1, keepdims=True)
            return jnp.where(cnt >= capf, cand, thr)

        thr = lax.fori_loop(0, 31, search, jnp.zeros((N_EXPERTS, 1), I32))
        need = capf - jnp.sum(jnp.where(bits > thr, 1.0, 0.0), axis=1, keepdims=True)
        off_eq = jnp.zeros((N_EXPERTS, 1), F32)
        off_sel = jnp.zeros((N_EXPERTS, 1), F32)
        for c in range(t // LANES):
            sl = slice(c * LANES, (c + 1) * LANES)
            tile, sub = divmod(lo // LANES + c, chunks_per_tile)
            if sub == 0:
                starts = jnp.where(lane_id == tile, off_sel.astype(I32) + base, starts)
            bits_c = bits[:, sl]
            eq = bits_c == thr
            eq_c = jnp.where(eq, 1.0, 0.0)
            rank_eq = jnp.dot(eq_c.astype(BF16), strict_upper, preferred_element_type=F32) + off_eq
            off_eq = off_eq + jnp.sum(eq_c, axis=1, keepdims=True)
            sel = jnp.logical_or(bits_c > thr, jnp.logical_and(eq, rank_eq < need))
            sel_c = jnp.where(sel, 1.0, 0.0)
            slot = jnp.dot(sel_c.astype(BF16), strict_upper, preferred_element_type=F32) + off_sel
            off_sel = off_sel + jnp.sum(sel_c, axis=1, keepdims=True)
            osl = slice(sub * LANES, (sub + 1) * LANES)
            pos_ref[0, tile, :, osl] = jnp.where(sel, slot.astype(I32) + base, -1)
            gate_ref[0, tile, :, osl] = jnp.where(sel, ss[:, sl], 0.0)

    n_tiles = lg.shape[1] // TOKEN_TILE
    starts_ref[0] = jnp.where(lane_id == n_tiles, n_slots, starts)


def _route(logits_t, *, segments, n_slots):
    b, ne, tt = logits_t.shape
    nt = tt // TOKEN_TILE
    assert nt < LANES and all(lo % TOKEN_TILE == 0 and t % TOKEN_TILE == 0 for lo, t, _, _ in segments)
    kern = functools.partial(_route_kernel, segments=segments, n_slots=n_slots)
    tiled = lambda: pl.BlockSpec((1, nt, ne, TOKEN_TILE), lambda i: (i, 0, 0, 0))
    return pl.pallas_call(
        kern,
        grid=(b,),
        in_specs=[pl.BlockSpec((1, ne, tt), lambda i: (i, 0, 0))],
        out_specs=[tiled(), tiled(), pl.BlockSpec((1, ne, LANES), lambda i: (i, 0, 0))],
        out_shape=[jax.ShapeDtypeStruct((b, nt, ne, TOKEN_TILE), I32),
                   jax.ShapeDtypeStruct((b, nt, ne, TOKEN_TILE), F32),
                   jax.ShapeDtypeStruct((b, ne, LANES), I32)],
        compiler_params=_params(("arbitrary",)),
        name="route",
    )(logits_t)


def _window_start(starts_ref, base_idx, tile, w, win, limit):
    c0 = starts_ref[base_idx + tile]
    lo = ((c0 >> 4) << 4) + w * win
    return lo, pl.multiple_of(jnp.minimum(lo, limit), BF16_ROWS)


def _window_count(starts_ref, base_idx, tiles, win):
    nw = jnp.int32(1)
    for tile in tiles:
        c0 = starts_ref[base_idx + tile]
        c1 = starts_ref[base_idx + tile + 1]
        nw = jnp.maximum(nw, (c1 - ((c0 >> 4) << 4) + win - 1) // win)
    return nw


def _gather_kernel(starts_ref, h_ref, pos_ref, xs_ref, acc_ref, *, slots, win, group):
    b = pl.program_id(0)
    eg = pl.program_id(1)
    nt = pos_ref.shape[1]
    acc_ref[...] = jnp.zeros_like(acc_ref)
    rel = lax.broadcasted_iota(I32, (win, TOKEN_TILE), 0)
    bases = [(b * N_EXPERTS + eg * group + i) * LANES for i in range(group)]
    nw = jnp.int32(1)
    for i in range(group):
        nw = jnp.maximum(nw, _window_count(starts_ref, bases[i], range(nt), win))

    def window_pass(w, carry):
        for tile in range(nt):
            onehots, offs = [], []
            for i in range(group):
                lo, start = _window_start(starts_ref, bases[i], tile, w, win, slots)
                prow = pos_ref[0, tile, pl.ds(eg * group + i, 1), :] - start
                onehots.append(jnp.where(rel == prow, 1.0, 0.0).astype(BF16))
                offs.append(start)
            res = jnp.dot(jnp.concatenate(onehots, axis=0), h_ref[0, tile * TOKEN_TILE:(tile + 1) * TOKEN_TILE, :],
                          preferred_element_type=F32)
            for i in range(group):
                acc_ref[i, pl.ds(offs[i], win), :] += res[i * win:(i + 1) * win].astype(BF16)
        return carry

    lax.fori_loop(0, nw, window_pass, 0)
    xs_ref[...] = acc_ref[:, 0:slots, :]


def _gather(starts, h2, pos, *, slots, win, group=8):
    b, tt, d = h2.shape
    nt = tt // TOKEN_TILE
    kern = functools.partial(_gather_kernel, slots=slots, win=win, group=group)
    return pl.pallas_call(
        kern,
        grid_spec=pltpu.PrefetchScalarGridSpec(
            num_scalar_prefetch=1,
            grid=(b, N_EXPERTS // group),
            in_specs=[
                pl.BlockSpec((1, tt, d), lambda i, g, s: (i, 0, 0)),
                pl.BlockSpec((1, nt, N_EXPERTS, TOKEN_TILE), lambda i, g, s: (i, 0, 0, 0)),
            ],
            out_specs=pl.BlockSpec((group, slots, d), lambda i, g, s: (g, i, 0)),
            scratch_shapes=[pltpu.VMEM((group, slots + win, d), BF16)],
        ),
        out_shape=jax.ShapeDtypeStruct((N_EXPERTS, b * slots, d), BF16),
        compiler_params=_params(("arbitrary", "arbitrary")),
        name="moe_gather",
    )(starts, h2, pos)


def _ffn_kernel(xs_ref, wg_ref, wu_ref, wd_ref, ys_ref, *, fchunk):
    xs = xs_ref[0]
    f = wg_ref.shape[-1]
    acc = None
    for c in range(f // fchunk):
        sl = slice(c * fchunk, (c + 1) * fchunk)
        mm = (((1,), (0,)), ((), ()))
        a = lax.dot_general(xs, wg_ref[0, 0, :, sl], mm, preferred_element_type=F32)
        u = lax.dot_general(xs, wu_ref[0, 0, :, sl], mm, preferred_element_type=F32)
        hm = (a * jax.nn.sigmoid(a) * u).astype(BF16)
        y = lax.dot_general(hm, wd_ref[0, 0, sl, :], mm, preferred_element_type=F32)
        acc = y if acc is None else acc + y
    ys_ref[0] = acc.astype(BF16)


def _expert_ffn(xs, wg, wu, wd, l, *, slots):
    ne, rows, d = xs.shape
    f = wg.shape[-1]
    nb = rows // slots
    kern = functools.partial(_ffn_kernel, fchunk=_largest_divisor(f, (512,)))
    return pl.pallas_call(
        kern,
        grid=(ne, nb),
        in_specs=[
            pl.BlockSpec((1, slots, d), lambda e, i: (e, i, 0)),
            pl.BlockSpec((1, 1, d, f), lambda e, i: (l, e, 0, 0), pipeline_mode=pl.Buffered(1)),
            pl.BlockSpec((1, 1, d, f), lambda e, i: (l, e, 0, 0), pipeline_mode=pl.Buffered(1)),
            pl.BlockSpec((1, 1, f, d), lambda e, i: (l, e, 0, 0), pipeline_mode=pl.Buffered(1)),
        ],
        out_specs=pl.BlockSpec((1, slots, d), lambda e, i: (e, i, 0)),
        out_shape=jax.ShapeDtypeStruct((ne, rows, d), BF16),
        compiler_params=_params(("arbitrary", "arbitrary")),
        name="expert_ffn",
    )(xs, wg, wu, wd)


def _combine_kernel(starts_ref, ys_ref, pos_ref, gate_ref, x1_ref, mod_ref, g_ref, b_ref, o_ref,
                    *, slots, win, d, alpha):
    b = pl.program_id(0)
    tile = pl.program_id(1)
    rel = lax.broadcasted_iota(I32, (win, TOKEN_TILE), 0)
    bases = [(b * N_EXPERTS + e) * LANES for e in range(N_EXPERTS)]
    nw = jnp.int32(1)
    for e in range(N_EXPERTS):
        c0 = starts_ref[bases[e] + tile]
        c1 = starts_ref[bases[e] + tile + 1]
        nw = jnp.maximum(nw, (c1 - ((c0 >> 4) << 4) + win - 1) // win)

    def window_pass(w, moe):
        gated, rows = [], []
        for e in range(N_EXPERTS):
            lo, start = _window_start(starts_ref, bases[e], tile, w, win, slots - win)
            prow = pos_ref[0, 0, e:e + 1, :]
            prow = jnp.where(jnp.logical_and(prow >= lo, prow < lo + win), prow - start, -1)
            gated.append(jnp.where(rel == prow, gate_ref[0, 0, e:e + 1, :], 0.0).astype(BF16))
            rows.append(ys_ref[e, pl.ds(start, win), :])
        return moe + lax.dot_general(jnp.concatenate(gated, axis=0), jnp.concatenate(rows, axis=0),
                                     (((0,), (0,)), ((), ())), preferred_element_type=F32)

    moe = lax.fori_loop(0, nw, window_pass, jnp.zeros((TOKEN_TILE, d), F32))
    m = mod_ref[0]
    o_ref[0] = _ln(alpha * x1_ref[0] + m[:, 5 * d:6 * d] * moe) * g_ref[0] + b_ref[0]


def _combine(starts, ys, pos, gate, x1, mod2, ln_g, ln_b, l, *, slots, win, n_lat_tiles, alpha, out_tokens):
    b, _, d = x1.shape
    kern = functools.partial(_combine_kernel, slots=slots, win=win, d=d, alpha=alpha)
    tiled = lambda: pl.BlockSpec((1, 1, N_EXPERTS, TOKEN_TILE), lambda i, t, s: (i, t, 0, 0))
    tok = lambda: pl.BlockSpec((1, TOKEN_TILE, d), lambda i, t, s: (i, t, 0))
    return pl.pallas_call(
        kern,
        grid_spec=pltpu.PrefetchScalarGridSpec(
            num_scalar_prefetch=1,
            grid=(b, out_tokens // TOKEN_TILE),
            in_specs=[
                pl.BlockSpec((N_EXPERTS, slots, d), lambda i, t, s: (0, i, 0)),
                tiled(), tiled(), tok(),
                pl.BlockSpec((1, 1, 6 * d), lambda i, t, s: (2 * i + (t >= n_lat_tiles).astype(I32), 0, 0)),
                pl.BlockSpec((1, 1, d), lambda i, t, s: (l, 0, 0)),
                pl.BlockSpec((1, 1, d), lambda i, t, s: (l, 0, 0)),
            ],
            out_specs=tok(),
        ),
        out_shape=jax.ShapeDtypeStruct((b, out_tokens, d), F32),
        compiler_params=_params(("arbitrary", "arbitrary")),
        name="moe_combine",
    )(starts, ys, pos, gate, x1, mod2, ln_g, ln_b)


def _rope_tables(n, cn):
    lane = np.arange(LANES)
    within = lane % QK_DIM
    use_col = (within // 32) == 1
    first_half = (within % 32) < 16
    inv = ROPE_BASE ** (-(within % 16).astype(np.float64) / 16.0)
    pos = np.arange(n)
    coord = np.where(use_col[None, :], (pos % GRID_W)[:, None], (pos // GRID_W)[:, None]).astype(np.float32)
    ang = jnp.asarray(coord) * jnp.asarray(inv.astype(np.float32))[None, :]
    cos = jnp.cos(ang)
    sin = jnp.where(jnp.asarray(first_half)[None, :], -jnp.sin(ang), jnp.sin(ang))
    cos = jnp.concatenate([cos, jnp.ones((cn, LANES), F32)], axis=0)
    sin = jnp.concatenate([sin, jnp.zeros((cn, LANES), F32)], axis=0)
    return cos, sin


def _channel_dft():
    idx = np.arange(FNET_W)
    same = (idx[:, None] // FNET_GROUP_W) == (idx[None, :] // FNET_GROUP_W)
    ang = 2.0 * np.pi * ((idx[:, None] % FNET_GROUP_W) * (idx[None, :] % FNET_GROUP_W) % FNET_GROUP_W) / FNET_GROUP_W
    cs = np.concatenate([np.where(same, np.cos(ang), 0.0), np.where(same, np.sin(ang), 0.0)], axis=1)
    return jnp.asarray(cs.astype(np.float32)).astype(BF16)


def _position_dft_mats(t):
    g = math.gcd(t, 64)
    kk = jnp.arange(t, dtype=I32)

    def table(m):
        ph = ((kk[:, None] * m[None, :]) % t).astype(F32) * (2.0 * math.pi / t)
        return jnp.cos(ph), jnp.sin(ph)

    ch, sh = table(jnp.arange(t // g, dtype=I32) * g)
    cl, sl = table(jnp.arange(g, dtype=I32))
    scale = 1.0 / math.sqrt(t * FNET_GROUP_W)
    cmat = (ch[:, :, None] * cl[:, None, :] - sh[:, :, None] * sl[:, None, :]).reshape(t, t) * scale
    smat = (sh[:, :, None] * cl[:, None, :] + ch[:, :, None] * sl[:, None, :]).reshape(t, t) * (-scale)
    return cmat.astype(BF16), smat.astype(BF16)


def _block_diag_gates(wa, wx):
    depth = wa.shape[0]

    def dense(wb):
        eye = jnp.eye(LRU_BLOCKS, dtype=wb.dtype)
        return jnp.einsum('lncd,nm->lncmd', wb, eye).reshape(depth, LRU_W, LRU_W)

    halves = []
    for hh in range(2):
        sl = slice(hh * LANES, (hh + 1) * LANES)
        cols = [dense(wmat[:, dr])[:, sl, sl] for dr in range(2) for wmat in (wa, wx)]
        halves.append(jnp.concatenate(cols, axis=-1))
    return jnp.stack(halves, axis=1).reshape(depth * 2, LANES, 4 * LANES).astype(BF16)


def _gate_bias(ba, bx):
    depth = ba.shape[0]
    halves = []
    for hh in range(2):
        sl = slice(hh * LANES, (hh + 1) * LANES)
        halves.append(jnp.concatenate([bvec[:, dr, sl] for dr in range(2) for bvec in (ba, bx)], axis=-1))
    return jnp.stack(halves, axis=1).reshape(depth * 2, 1, 4 * LANES)


def _split_halves(a):
    depth, r, _ = a.shape
    return a.reshape(depth, r, 2, LANES).transpose(0, 2, 1, 3).reshape(depth * 2, r, LANES)


def kernel(x, c, ctx, c_ctx, w_mod, b_mod, w_in, lam_q1, lam_k1, lam_q2, lam_k2, attn_norm_g, conv_w, conv_b, lru_wa, lru_ba, lru_wx, lru_bx, lru_lam, w_out, ln1_g, ln1_b, w_router, w_gate, w_up, w_down, ln2_g, ln2_b):
    b, n, d = x.shape
    cn = ctx.shape[1]
    depth = w_mod.shape[0]
    tt = n + cn
    tb = _largest_divisor(math.gcd(n, cn), (256, 128))
    assert n % tb == 0 and cn % tb == 0 and n % GRID_W == 0 and tt % LANES == 0
    nbl = n // tb
    cap_l = CAPACITY_FACTOR * n // N_EXPERTS
    cap_c = CAPACITY_FACTOR * cn // N_EXPERTS
    assert cap_l % BF16_ROWS == 0 and cap_c % BF16_ROWS == 0 and n % TOKEN_TILE == 0 and cn % TOKEN_TILE == 0
    alpha = (2 * depth) ** 0.25

    rows = -(-(b + 1) // SUBLANES) * SUBLANES
    cc = jnp.concatenate([c, c_ctx[None, :], jnp.zeros((rows - b - 1, d), F32)], axis=0)
    mod = _modulation(cc, w_mod, b_mod)

    cos_t, sin_t = _rope_tables(n, cn)
    cs = _channel_dft()
    dft_l = _position_dft_mats(n)
    dft_c = _position_dft_mats(cn)

    w_in_bf = w_in.astype(BF16)
    w_out_bf = w_out.astype(BF16)
    wr_t = jnp.swapaxes(w_router, 1, 2).astype(BF16)
    lamv = jnp.stack([lam_q1, lam_k1, lam_q2, lam_k2], axis=1).astype(F32)
    lam_init = np.array([0.8 - 0.6 * math.exp(-0.3 * l) for l in range(depth)], np.float32)
    lin = jnp.asarray(np.broadcast_to(lam_init[:, None, None], (depth, 1, LANES)).copy())
    gain = attn_norm_g.reshape(depth * N_HEADS, 1, HEAD_V)
    conv_w2 = _split_halves(conv_w)
    conv_b2 = _split_halves(conv_b[:, None, :])
    lam2 = _split_halves(lru_lam)
    wgate = _block_diag_gates(lru_wa, lru_wx)
    bgate = _gate_bias(lru_ba, lru_bx)
    ln1g, ln1b = ln1_g[:, None, :], ln1_b[:, None, :]
    ln2g, ln2b = ln2_g[:, None, :], ln2_b[:, None, :]

    xc = jnp.concatenate([x, ctx], axis=1)
    for l in range(depth):
        ml = mod[l]
        mod2 = jnp.stack([ml[:b], jnp.broadcast_to(ml[b][None, :], (b, 6 * d))], axis=1).reshape(2 * b, 1, 6 * d)
        qx, k, v, ux, gg, gc, gs = _in_proj(xc, mod2, w_in_bf, l, cos_t, sin_t, cs, nbl=nbl, tb=tb)
        keep_ctx = l < depth - 1
        tokens = tt if keep_ctx else n
        att_l, att_c = _attention(qx, k, v, lamv, lin, gain, l, n=n, tb=tb)
        y = _rglru(ux, gg, conv_w2, conv_b2, wgate, bgate, lam2, l, n=n, cn=cn, tb=tb)
        fy_l = _position_dft(*dft_l, gc, gs, row0=0)
        fy_c = _position_dft(*dft_c, gc, gs, row0=n) if keep_ctx else fy_l
        x1, h2, logits_t = _out_proj(att_l, att_c, y, fy_l, fy_c, xc, mod2, w_out_bf, ln1g, ln1b, wr_t, l,
                                     nbl=nbl, tb=tb, alpha=alpha, out_tokens=tokens)
        segments = ((0, n, cap_l, 0), (n, cn, cap_c, cap_l)) if keep_ctx else ((0, n, cap_l, 0),)
        slots = cap_l + cap_c if keep_ctx else cap_l
        win = min(SLOT_WINDOW, slots)
        pos, gate, starts = _route(logits_t, segments=segments, n_slots=slots)
        starts = starts.reshape(-1)
        xs = _gather(starts, h2, pos, slots=slots, win=win)
        ys = _expert_ffn(xs, w_gate, w_up, w_down, l, slots=slots)
        xc = _combine(starts, ys, pos, gate, x1, mod2, ln2g, ln2b, l, slots=slots, win=win,
                      n_lat_tiles=n // TOKEN_TILE, alpha=alpha, out_tokens=tokens)
    return xc
```

```python
import functools
import math

import numpy as np
import jax
import jax.numpy as jnp
from jax import lax
from jax.experimental import pallas as pl
from jax.experimental.pallas import tpu as pltpu

F32 = jnp.float32
BF16 = jnp.bfloat16
I32 = jnp.int32

GRID_W = 64
QK_DIM = 64
N_HEADS = 4
HEAD_V = 128
QK_W = 512
ATT_W = 512
LRU_W = 256
LRU_BLOCKS = 4
LRU_BLOCK_W = 64
LRU_C = 8.0
FNET_W = 256
FNET_GROUP_W = 64
IN_W = 2304
ROPE_BASE = 10000.0
N_EXPERTS = 16
CAPACITY_FACTOR = 2
LN_EPS = 1e-5
RMS_EPS = 1e-6
GELU_C = math.sqrt(2.0 / math.pi)
LOG2E = math.log2(math.e)

LANES = 128
SUBLANES = 8
BF16_ROWS = 16
KEY_CHUNK = 256
OUT_PIECES = 4
SCAN_UNROLL = 4
TOKEN_TILE = 256
SLOT_WINDOW = 64
VMEM_LIMIT = 56 << 20


def _params(sem, vmem=VMEM_LIMIT):
    return pltpu.CompilerParams(dimension_semantics=sem, vmem_limit_bytes=vmem)


def _ln(x):
    mu = jnp.mean(x, axis=-1, keepdims=True)
    xc = x - mu
    var = jnp.mean(xc * xc, axis=-1, keepdims=True)
    return xc * lax.rsqrt(var + LN_EPS)


def _largest_divisor(n, candidates):
    for c in candidates:
        if c <= n and n % c == 0:
            return c
    return n


def _mod_kernel(c_ref, w_ref, b_ref, o_ref):
    c = c_ref[...]
    s = c * jax.nn.sigmoid(c)
    o_ref[0] = jnp.dot(s, w_ref[0], precision=lax.Precision.HIGHEST, preferred_element_type=F32) + b_ref[0]


def _modulation(cc, w_mod, b_mod):
    depth, d, d6 = w_mod.shape
    rows = cc.shape[0]
    tn = 1024
    return pl.pallas_call(
        _mod_kernel,
        grid=(depth, d6 // tn),
        in_specs=[
            pl.BlockSpec((rows, d), lambda l, n: (0, 0)),
            pl.BlockSpec((1, d, tn), lambda l, n: (l, 0, n)),
            pl.BlockSpec((1, 1, tn), lambda l, n: (l, 0, n)),
        ],
        out_specs=pl.BlockSpec((1, rows, tn), lambda l, n: (l, 0, n)),
        out_shape=jax.ShapeDtypeStruct((depth, rows, d6), F32),
        compiler_params=_params(("arbitrary", "arbitrary")),
        name="modulation",
    )(cc, w_mod, b_mod.reshape(depth, 1, d6))


def _in_kernel(x_ref, mod_ref, w_ref, cos_ref, sin_ref, cs_ref,
               qx_ref, k_ref, v_ref, ux_ref, gg_ref, gc_ref, gs_ref, *, d):
    x = x_ref[0]
    m = mod_ref[0]
    h = _ln(x) * (1.0 + m[:, d:2 * d]) + m[:, 0:d]
    z = jnp.dot(h.astype(BF16), w_ref[0], preferred_element_type=F32)

    tb = x.shape[0]
    cos = cos_ref[...]
    sin = sin_ref[...]
    lane = lax.broadcasted_iota(I32, (tb, LANES), 1)
    first_half = (lane & 31) < 16
    low_map = lane < QK_DIM

    def rope(t):
        partner = jnp.where(first_half, pltpu.roll(t, LANES - 16, 1), pltpu.roll(t, 16, 1))
        return t * cos + partner * sin

    for p in range(N_HEADS):
        qp = rope(z[:, p * LANES:(p + 1) * LANES] * (QK_DIM ** -0.5 * LOG2E))
        qx_ref[0, :, (2 * p) * LANES:(2 * p + 1) * LANES] = jnp.where(low_map, qp, 0.0).astype(BF16)
        qx_ref[0, :, (2 * p + 1) * LANES:(2 * p + 2) * LANES] = jnp.where(low_map, 0.0, qp).astype(BF16)
        kp = rope(z[:, QK_W + p * LANES:QK_W + (p + 1) * LANES])
        k_ref[0, :, p * LANES:(p + 1) * LANES] = kp.astype(BF16)

    o = 2 * QK_W
    v_ref[0] = z[:, o:o + ATT_W].astype(BF16)
    o += ATT_W
    ux_ref[0] = z[:, o:o + LRU_W]
    o += LRU_W
    g = z[:, o:o + LRU_W]
    gg_ref[0] = 0.5 * g * (1.0 + jnp.tanh(GELU_C * (g + 0.044715 * (g * g * g))))
    o += LRU_W
    uf = z[:, o:o + FNET_W].astype(BF16)
    gcs = jnp.dot(uf, cs_ref[...], preferred_element_type=F32)
    gc_ref[...] = gcs[:, :FNET_W].astype(BF16)
    gs_ref[...] = gcs[:, FNET_W:].astype(BF16)


def _in_proj(xc, mod2, w_in_bf, l, cos_t, sin_t, cs, *, nbl, tb):
    b, tt, d = xc.shape
    nbt = tt // tb
    kern = functools.partial(_in_kernel, d=d)
    tok = lambda w: pl.BlockSpec((1, tb, w), lambda i, j: (i, j, 0))
    return pl.pallas_call(
        kern,
        grid=(b, nbt),
        in_specs=[
            tok(d),
            pl.BlockSpec((1, 1, 6 * d), lambda i, j: (2 * i + (j >= nbl).astype(I32), 0, 0)),
            pl.BlockSpec((1, d, IN_W), lambda i, j: (l, 0, 0)),
            pl.BlockSpec((tb, LANES), lambda i, j: (j, 0)),
            pl.BlockSpec((tb, LANES), lambda i, j: (j, 0)),
            pl.BlockSpec((FNET_W, 2 * FNET_W), lambda i, j: (0, 0)),
        ],
        out_specs=[
            tok(2 * QK_W), tok(QK_W), tok(ATT_W), tok(LRU_W), tok(LRU_W),
            pl.BlockSpec((tb, FNET_W), lambda i, j: (j, i)),
            pl.BlockSpec((tb, FNET_W), lambda i, j: (j, i)),
        ],
        out_shape=[
            jax.ShapeDtypeStruct((b, tt, 2 * QK_W), BF16),
            jax.ShapeDtypeStruct((b, tt, QK_W), BF16),
            jax.ShapeDtypeStruct((b, tt, ATT_W), BF16),
            jax.ShapeDtypeStruct((b, tt, LRU_W), F32),
            jax.ShapeDtypeStruct((b, tt, LRU_W), F32),
            jax.ShapeDtypeStruct((tt, b * FNET_W), BF16),
            jax.ShapeDtypeStruct((tt, b * FNET_W), BF16),
        ],
        compiler_params=_params(("arbitrary", "arbitrary")),
        name="in_proj",
    )(xc, mod2, w_in_bf, cos_t, sin_t, cs)


def _attn_kernel(ql_ref, qc_ref, k_ref, v_ref, lamv_ref, lin_ref, g_ref, ol_ref, oc_ref, vt_ref, s0_ref, s1_ref,
                 *, n, lat_steps, qb):
    j = pl.program_id(2)
    lv = lamv_ref[0]
    lam_init = lin_ref[0][:, 0:1]
    lam = (jnp.exp(jnp.sum(lv[0:1] * lv[1:2], axis=1, keepdims=True))
           - jnp.exp(jnp.sum(lv[2:3] * lv[3:4], axis=1, keepdims=True)) + lam_init)
    gain = g_ref[0] * (1.0 - lam_init)
    nt = (((1,), (1,)), ((), ()))
    tt = k_ref.shape[1]

    @pl.when(j == 0)
    def _():
        vt_ref[0:HEAD_V, :] = v_ref[0].astype(F32).T.astype(BF16)
        row = lax.broadcasted_iota(I32, (BF16_ROWS, tt), 0)
        vt_ref[HEAD_V:, :] = jnp.where(row == 0, 1.0, 0.0).astype(BF16)

    def attend(q_ref, o_ref, blocks, lo):
        chunks = [(c, min(KEY_CHUNK, tt - c)) for c in range(lo, tt, KEY_CHUNK)]
        tq = s0_ref.shape[1]
        streams = [(blk, mp) for blk in range(blocks) for mp in range(2)]
        s_refs = (s0_ref, s1_ref)

        def scores(i, c, w, m):
            blk, mp = streams[i]
            q = q_ref[0, blk * tq:(blk + 1) * tq, mp * LANES:(mp + 1) * LANES]
            st = lax.dot_general(k_ref[0, c:c + w, :], q, nt, preferred_element_type=F32)
            s_refs[i % 2][c:c + w, :] = st
            mc = jnp.max(st, axis=0, keepdims=True)
            return mc if m is None else jnp.maximum(m, mc)

        def weighted_values(i, c, w, m, acc):
            pt = jnp.exp2(s_refs[i % 2][c:c + w, :] - m).astype(BF16)
            part = jnp.dot(vt_ref[:, c:c + w], pt, preferred_element_type=F32)
            return part if acc is None else acc + part

        ms = [None] * len(streams)
        accs = [None] * len(streams)
        for stage in range(len(streams) + 1):
            for c, w in chunks:
                if stage >= 1:
                    accs[stage - 1] = weighted_values(stage - 1, c, w, ms[stage - 1], accs[stage - 1])
                if stage < len(streams):
                    ms[stage] = scores(stage, c, w, ms[stage])
            if stage >= 2 and stage % 2 == 0:
                blk = stage // 2 - 1
                a0, a1 = accs[stage - 2], accs[stage - 1]
                ot = (a0[0:HEAD_V] / a0[HEAD_V:HEAD_V + 1]) - lam * (a1[0:HEAD_V] / a1[HEAD_V:HEAD_V + 1])
                rt = ot * lax.rsqrt(jnp.mean(ot * ot, axis=0, keepdims=True) + RMS_EPS)
                o_ref[0, blk * tq:(blk + 1) * tq, :] = (rt.T * gain).astype(BF16)

    @pl.when(j < lat_steps)
    def _():
        attend(ql_ref, ol_ref, qb, 0)

    @pl.when(j >= lat_steps)
    def _():
        attend(qc_ref, oc_ref, qc_ref.shape[1] // s0_ref.shape[1], n)


def _attention(qx, k, v, lamv, lin, gain, l, *, n, tb):
    b, tt, _ = k.shape
    cn = tt - n
    qb = _largest_divisor(n // tb, (16, 8, 4, 2, 1))
    lat_steps = n // (qb * tb)
    assert n % cn == 0 and cn % tb == 0
    kern = functools.partial(_attn_kernel, n=n, lat_steps=lat_steps, qb=qb)
    lat = lambda w: pl.BlockSpec((1, qb * tb, w), lambda i, h, j: (i, jnp.minimum(j, lat_steps - 1), h))
    ctx = lambda w: pl.BlockSpec((1, cn, w), lambda i, h, j: (i, n // cn, h))
    return pl.pallas_call(
        kern,
        grid=(b, N_HEADS, lat_steps + 1),
        in_specs=[
            lat(2 * LANES), ctx(2 * LANES),
            pl.BlockSpec((1, tt, LANES), lambda i, h, j: (i, 0, h)),
            pl.BlockSpec((1, tt, LANES), lambda i, h, j: (i, 0, h)),
            pl.BlockSpec((1, 4, QK_DIM), lambda i, h, j: (l, 0, 0)),
            pl.BlockSpec((1, 1, LANES), lambda i, h, j: (l, 0, 0)),
            pl.BlockSpec((1, 1, HEAD_V), lambda i, h, j: (l * N_HEADS + h, 0, 0)),
        ],
        out_specs=[lat(LANES), pl.BlockSpec((1, cn, LANES), lambda i, h, j: (i, 0, h))],
        out_shape=[jax.ShapeDtypeStruct((b, n, ATT_W), BF16), jax.ShapeDtypeStruct((b, cn, ATT_W), BF16)],
        scratch_shapes=[pltpu.VMEM((HEAD_V + BF16_ROWS, tt), BF16),
                        pltpu.VMEM((tt, tb), F32), pltpu.VMEM((tt, tb), F32)],
        compiler_params=_params(("arbitrary", "arbitrary", "arbitrary")),
        name="diff_attention",
    )(qx, qx, k, v, lamv, lin, gain)


def _lru_kernel(ux_ref, gg_ref, cw_ref, cb_ref, wg_ref, bg_ref, lam_ref, y_ref,
                a_f, b_f, a_b, b_b, *, n, cn, r):
    tt = n + cn
    w = LANES
    cw = cw_ref[0]
    cb = cb_ref[0]
    bg = bg_ref[0]
    neg_lam = -lam_ref[0]
    softplus = jnp.maximum(neg_lam, 0.0) + jnp.log(1.0 + jnp.exp(-jnp.abs(neg_lam)))
    row8 = lax.broadcasted_iota(I32, (r // SUBLANES, SUBLANES, w), 1)
    ext_rows = r + 2 * SUBLANES

    def local_scan(a, bb, reverse):
        a = a.reshape(r // SUBLANES, SUBLANES, w)
        bb = bb.reshape(r // SUBLANES, SUBLANES, w)
        for s in (1, 2, 4):
            if reverse:
                a_sh = pltpu.roll(a, SUBLANES - s, 1)
                b_sh = pltpu.roll(bb, SUBLANES - s, 1)
                valid = row8 < SUBLANES - s
            else:
                a_sh = pltpu.roll(a, s, 1)
                b_sh = pltpu.roll(bb, s, 1)
                valid = row8 >= s
            bb = jnp.where(valid, a * b_sh + bb, bb)
            a = jnp.where(valid, a * a_sh, a)
        return a.reshape(r, w), bb.reshape(r, w)

    def gates_chunk(c, carry):
        r0 = pl.multiple_of(c * r, r)
        seg_start = jnp.logical_or(r0 == 0, r0 == n)
        seg_end = jnp.logical_or(r0 + r == n, r0 + r == tt)
        main = ux_ref[0, pl.ds(r0, r), :]
        prev = ux_ref[0, pl.ds(pl.multiple_of(jnp.maximum(r0 - SUBLANES, 0), SUBLANES), SUBLANES), :]
        nxt = ux_ref[0, pl.ds(pl.multiple_of(jnp.minimum(r0 + r, tt - SUBLANES), SUBLANES), SUBLANES), :]
        prev = jnp.where(seg_start, 0.0, prev)
        nxt = jnp.where(seg_end, 0.0, nxt)
        ext = jnp.concatenate([prev, main, nxt], axis=0)
        u = cb
        for t in range(4):
            sh = (2 - t) % ext_rows
            win = ext if sh == 0 else pltpu.roll(ext, sh, 0)
            u = u + cw[t:t + 1, :] * win[SUBLANES:SUBLANES + r, :]
        zz = jnp.dot(u.astype(BF16), wg_ref[0], preferred_element_type=F32) + bg
        for dr, (a_s, b_s) in enumerate(((a_f, b_f), (a_b, b_b))):
            rg = 0.5 + 0.5 * jnp.tanh(0.5 * zz[:, (2 * dr) * w:(2 * dr + 1) * w])
            ig = 0.5 + 0.5 * jnp.tanh(0.5 * zz[:, (2 * dr + 1) * w:(2 * dr + 2) * w])
            a = jnp.exp(-LRU_C * rg * softplus[dr:dr + 1, :])
            bb = jnp.sqrt(1.0 - a * a) * ig * u
            a, bb = local_scan(a, bb, reverse=(dr == 1))
            a_s[pl.ds(r0, r), :] = a
            b_s[pl.ds(r0, r), :] = bb
        return carry

    lax.fori_loop(0, tt // r, gates_chunk, 0)

    def seg_scan(first_tile, ntiles, cf, cbk):
        def body(i, carry):
            cf, cbk = carry
            rf = pl.multiple_of((first_tile + i) * SUBLANES, SUBLANES)
            hf = b_f[pl.ds(rf, SUBLANES), :] + a_f[pl.ds(rf, SUBLANES), :] * cf
            b_f[pl.ds(rf, SUBLANES), :] = hf
            rb = pl.multiple_of((first_tile + ntiles - 1 - i) * SUBLANES, SUBLANES)
            hb = b_b[pl.ds(rb, SUBLANES), :] + a_b[pl.ds(rb, SUBLANES), :] * cbk
            b_b[pl.ds(rb, SUBLANES), :] = hb
            return hf[SUBLANES - 1:SUBLANES, :], hb[0:1, :]
        return lax.fori_loop(0, ntiles, body, (cf, cbk), unroll=SCAN_UNROLL)

    zero = jnp.zeros((1, w), F32)
    cf, cbk = seg_scan(n // SUBLANES, cn // SUBLANES, zero, zero)
    seg_scan(0, n // SUBLANES, cf, cbk)

    def out_chunk(c, carry):
        r0 = pl.multiple_of(c * r, r)
        y = (b_f[pl.ds(r0, r), :] + b_b[pl.ds(r0, r), :]) * gg_ref[0, pl.ds(r0, r), :]
        y_ref[0, pl.ds(r0, r), :] = y.astype(BF16)
        return carry

    lax.fori_loop(0, tt // r, out_chunk, 0)


def _rglru(ux, gg, conv_w2, conv_b2, wgate, bgate, lam2, l, *, n, cn, tb):
    b, tt, _ = ux.shape
    kern = functools.partial(_lru_kernel, n=n, cn=cn, r=tb)
    half = lambda: pl.BlockSpec((1, tt, LANES), lambda i, hh: (i, 0, hh))
    return pl.pallas_call(
        kern,
        grid=(b, 2),
        in_specs=[
            half(), half(),
            pl.BlockSpec((1, 4, LANES), lambda i, hh: (2 * l + hh, 0, 0)),
            pl.BlockSpec((1, 1, LANES), lambda i, hh: (2 * l + hh, 0, 0)),
            pl.BlockSpec((1, LANES, 4 * LANES), lambda i, hh: (2 * l + hh, 0, 0)),
            pl.BlockSpec((1, 1, 4 * LANES), lambda i, hh: (2 * l + hh, 0, 0)),
            pl.BlockSpec((1, 2, LANES), lambda i, hh: (2 * l + hh, 0, 0)),
        ],
        out_specs=half(),
        out_shape=jax.ShapeDtypeStruct((b, tt, LRU_W), BF16),
        scratch_shapes=[pltpu.VMEM((tt, LANES), F32)] * 4,
        compiler_params=_params(("arbitrary", "arbitrary")),
        name="rglru",
    )(ux, gg, conv_w2, conv_b2, wgate, bgate, lam2)


def _dft_kernel(c_ref, s_ref, gc_ref, gs_ref, o_ref, acc_ref):
    kk = pl.program_id(2)

    def part():
        return (jnp.dot(c_ref[...], gc_ref[...], preferred_element_type=F32)
                + jnp.dot(s_ref[...], gs_ref[...], preferred_element_type=F32))

    @pl.when(kk == 0)
    def _():
        acc_ref[...] = part()

    @pl.when(kk > 0)
    def _():
        acc_ref[...] += part()

    @pl.when(kk == pl.num_programs(2) - 1)
    def _():
        o_ref[...] = acc_ref[...].astype(BF16)


def _position_dft(cmat, smat, gc, gs, *, row0):
    t = cmat.shape[0]
    nn = gc.shape[1]
    tm = _largest_divisor(t, (1024, 512, 256, 128))
    tk = _largest_divisor(t, (2048, 1024, 512, 256, 128))
    tn = _largest_divisor(nn, (512, 256))
    assert row0 % tk == 0
    return pl.pallas_call(
        _dft_kernel,
        grid=(t // tm, nn // tn, t // tk),
        in_specs=[
            pl.BlockSpec((tm, tk), lambda i, j, k: (i, k)),
            pl.BlockSpec((tm, tk), lambda i, j, k: (i, k)),
            pl.BlockSpec((tk, tn), lambda i, j, k: (row0 // tk + k, j)),
            pl.BlockSpec((tk, tn), lambda i, j, k: (row0 // tk + k, j)),
        ],
        out_specs=pl.BlockSpec((tm, tn), lambda i, j, k: (i, j)),
        out_shape=jax.ShapeDtypeStruct((t, nn), BF16),
        scratch_shapes=[pltpu.VMEM((tm, tn), F32)],
        compiler_params=_params(("arbitrary", "arbitrary", "arbitrary")),
        name="position_dft",
    )(cmat, smat, gc, gs)


def _out_kernel(attl_ref, attc_ref, y_ref, fl_ref, fc_ref, x_ref, mod_ref, w_ref, g_ref, b_ref, wr_ref,
                x1_ref, h2_ref, lg_ref, mix0_ref, mix1_ref, *, d, alpha, nbl, nblocks):
    j = pl.program_id(1)

    @pl.when(j == 0)
    def _():
        mix1_ref[...] = jnp.zeros_like(mix1_ref)

    def step(new_ref, old_ref):
        m = mod_ref[0]
        tb = x_ref.shape[1]
        is_ctx = jnp.minimum(j, nblocks - 1) >= nbl
        att = jnp.where(is_ctx, attc_ref[0], attl_ref[0])
        fy = jnp.where(is_ctx, fc_ref[...], fl_ref[...])
        for c in range(OUT_PIECES):
            rows = slice(c * tb // OUT_PIECES, (c + 1) * tb // OUT_PIECES)
            cols = slice(c * d // OUT_PIECES, (c + 1) * d // OUT_PIECES)
            x1 = _ln(alpha * x_ref[0, rows, :] + m[:, 2 * d:3 * d] * old_ref[rows, :]) * g_ref[0] + b_ref[0]
            x1_ref[0, rows, :] = x1
            h2_ref[0, rows, :] = (_ln(x1) * (1.0 + m[:, 4 * d:5 * d]) + m[:, 3 * d:4 * d]).astype(BF16)
            new_ref[:, cols] = (
                jnp.dot(att, w_ref[0, 0:ATT_W, cols], preferred_element_type=F32)
                + jnp.dot(y_ref[0], w_ref[0, ATT_W:ATT_W + LRU_W, cols], preferred_element_type=F32)
                + jnp.dot(fy, w_ref[0, ATT_W + LRU_W:, cols], preferred_element_type=F32))
        lg_ref[0] = lax.dot_general(wr_ref[0], h2_ref[0], (((1,), (1,)), ((), ())), preferred_element_type=F32)

    @pl.when(j % 2 == 0)
    def _():
        step(mix0_ref, mix1_ref)

    @pl.when(j % 2 == 1)
    def _():
        step(mix1_ref, mix0_ref)


def _out_proj(att_l, att_c, y, fy_l, fy_c, xc, mod2, w_out_bf, ln_g, ln_b, wr_t, l, *, nbl, tb, alpha, out_tokens):
    b, _, d = xc.shape
    nblocks = out_tokens // tb
    kern = functools.partial(_out_kernel, d=d, alpha=alpha, nbl=nbl, nblocks=nblocks)
    cur = lambda j: jnp.minimum(j, nblocks - 1)
    prv = lambda j: jnp.maximum(j - 1, 0)
    tok = lambda w: pl.BlockSpec((1, tb, w), lambda i, j: (i, prv(j), 0))
    return pl.pallas_call(
        kern,
        grid=(b, nblocks + 1),
        in_specs=[
            pl.BlockSpec((1, tb, ATT_W), lambda i, j: (i, jnp.minimum(cur(j), nbl - 1), 0)),
            pl.BlockSpec((1, tb, ATT_W), lambda i, j: (i, jnp.maximum(cur(j) - nbl, 0), 0)),
            pl.BlockSpec((1, tb, LRU_W), lambda i, j: (i, cur(j), 0)),
            pl.BlockSpec((tb, FNET_W), lambda i, j: (jnp.minimum(cur(j), nbl - 1), i)),
            pl.BlockSpec((tb, FNET_W), lambda i, j: (jnp.maximum(cur(j) - nbl, 0), i)),
            tok(d),
            pl.BlockSpec((1, 1, 6 * d), lambda i, j: (2 * i + (prv(j) >= nbl).astype(I32), 0, 0)),
            pl.BlockSpec((1, d, d), lambda i, j: (l, 0, 0)),
            pl.BlockSpec((1, 1, d), lambda i, j: (l, 0, 0)),
            pl.BlockSpec((1, 1, d), lambda i, j: (l, 0, 0)),
            pl.BlockSpec((1, N_EXPERTS, d), lambda i, j: (l, 0, 0)),
        ],
        out_specs=[tok(d), tok(d), pl.BlockSpec((1, N_EXPERTS, tb), lambda i, j: (i, 0, prv(j)))],
        out_shape=[
            jax.ShapeDtypeStruct((b, out_tokens, d), F32),
            jax.ShapeDtypeStruct((b, out_tokens, d), BF16),
            jax.ShapeDtypeStruct((b, N_EXPERTS, out_tokens), F32),
        ],
        scratch_shapes=[pltpu.VMEM((tb, d), F32), pltpu.VMEM((tb, d), F32)],
        compiler_params=_params(("arbitrary", "arbitrary")),
        name="out_proj",
    )(att_l, att_c, y, fy_l, fy_c, xc, mod2, w_out_bf, ln_g, ln_b, wr_t)


def _route_kernel(lg_ref, pos_ref, gate_ref, starts_ref, *, segments, n_slots):
    lg = lg_ref[0]
    lane_id = lax.broadcasted_iota(I32, (N_EXPERTS, LANES), 1)
    starts = jnp.zeros((N_EXPERTS, LANES), I32)
    chunks_per_tile = TOKEN_TILE // LANES
    e = jnp.exp(lg - jnp.max(lg, axis=0, keepdims=True))
    s = e / jnp.sum(e, axis=0, keepdims=True)
    ri = lax.broadcasted_iota(I32, (LANES, LANES), 0)
    ci = lax.broadcasted_iota(I32, (LANES, LANES), 1)
    strict_upper = jnp.where(ri < ci, 1.0, 0.0).astype(BF16)

    for lo, t, cap, base in segments:
        ss = s[:, lo:lo + t]
        bits = pltpu.bitcast(ss, I32)
        capf = float(cap)

        def search(i, thr, bits=bits, capf=capf):
            cand = thr | jnp.left_shift(jnp.int32(1), 30 - i)
            cnt = jnp.sum(jnp.where(bits >= cand, 1.0, 0.0), axis=1, keepdims=True)
            return jnp.where(cnt >= capf, cand, thr)

        thr = lax.fori_loop(0, 31, search, jnp.zeros((N_EXPERTS, 1), I32))
        need = capf - jnp.sum(jnp.where(bits > thr, 1.0, 0.0), axis=1, keepdims=True)
        off_eq = jnp.zeros((N_EXPERTS, 1), F32)
        off_sel = jnp.zeros((N_EXPERTS, 1), F32)
        for c in range(t // LANES):
            sl = slice(c * LANES, (c + 1) * LANES)
            tile, sub = divmod(lo // LANES + c, chunks_per_tile)
            if sub == 0:
                starts = jnp.where(lane_id == tile, off_sel.astype(I32) + base, starts)
            bits_c = bits[:, sl]
            eq = bits_c == thr
            eq_c = jnp.where(eq, 1.0, 0.0)
            rank_eq = jnp.dot(eq_c.astype(BF16), strict_upper, preferred_element_type=F32) + off_eq
            off_eq = off_eq + jnp.sum(eq_c, axis=1, keepdims=True)
            sel = jnp.logical_or(bits_c > thr, jnp.logical_and(eq, rank_eq < need))
            sel_c = jnp.where(sel, 1.0, 0.0)
            slot = jnp.dot(sel_c.astype(BF16), strict_upper, preferred_element_type=F32) + off_sel
            off_sel = off_sel + jnp.sum(sel_c, axis=1, keepdims=True)
            osl = slice(sub * LANES, (sub + 1) * LANES)
            pos_ref[0, tile, :, osl] = jnp.where(sel, slot.astype(I32) + base, -1)
            gate_ref[0, tile, :, osl] = jnp.where(sel, ss[:, sl], 0.0)

    n_tiles = lg.shape[1] // TOKEN_TILE
    starts_ref[0] = jnp.where(lane_id == n_tiles, n_slots, starts)


def _route(logits_t, *, segments, n_slots):
    b, ne, tt = logits_t.shape
    nt = tt // TOKEN_TILE
    assert nt < LANES and all(lo % TOKEN_TILE == 0 and t % TOKEN_TILE == 0 for lo, t, _, _ in segments)
    kern = functools.partial(_route_kernel, segments=segments, n_slots=n_slots)
    tiled = lambda: pl.BlockSpec((1, nt, ne, TOKEN_TILE), lambda i: (i, 0, 0, 0))
    return pl.pallas_call(
        kern,
        grid=(b,),
        in_specs=[pl.BlockSpec((1, ne, tt), lambda i: (i, 0, 0))],
        out_specs=[tiled(), tiled(), pl.BlockSpec((1, ne, LANES), lambda i: (i, 0, 0))],
        out_shape=[jax.ShapeDtypeStruct((b, nt, ne, TOKEN_TILE), I32),
                   jax.ShapeDtypeStruct((b, nt, ne, TOKEN_TILE), F32),
                   jax.ShapeDtypeStruct((b, ne, LANES), I32)],
        compiler_params=_params(("arbitrary",)),
        name="route",
    )(logits_t)


def _window_start(starts_ref, base_idx, tile, w, win, limit):
    c0 = starts_ref[base_idx + tile]
    lo = ((c0 >> 4) << 4) + w * win
    return lo, pl.multiple_of(jnp.minimum(lo, limit), BF16_ROWS)


def _window_count(starts_ref, base_idx, tiles, win):
    nw = jnp.int32(1)
    for tile in tiles:
        c0 = starts_ref[base_idx + tile]
        c1 = starts_ref[base_idx + tile + 1]
        nw = jnp.maximum(nw, (c1 - ((c0 >> 4) << 4) + win - 1) // win)
    return nw


def _gather_kernel(starts_ref, h_ref, pos_ref, xs_ref, acc_ref, *, slots, win, group):
    b = pl.program_id(0)
    eg = pl.program_id(1)
    nt = pos_ref.shape[1]
    acc_ref[...] = jnp.zeros_like(acc_ref)
    rel = lax.broadcasted_iota(I32, (win, TOKEN_TILE), 0)
    bases = [(b * N_EXPERTS + eg * group + i) * LANES for i in range(group)]
    nw = jnp.int32(1)
    for i in range(group):
        nw = jnp.maximum(nw, _window_count(starts_ref, bases[i], range(nt), win))

    def window_pass(w, carry):
        for tile in range(nt):
            onehots, offs = [], []
            for i in range(group):
                lo, start = _window_start(starts_ref, bases[i], tile, w, win, slots)
                prow = pos_ref[0, tile, pl.ds(eg * group + i, 1), :] - start
                onehots.append(jnp.where(rel == prow, 1.0, 0.0).astype(BF16))
                offs.append(start)
            res = jnp.dot(jnp.concatenate(onehots, axis=0), h_ref[0, tile * TOKEN_TILE:(tile + 1) * TOKEN_TILE, :],
                          preferred_element_type=F32)
            for i in range(group):
                acc_ref[i, pl.ds(offs[i], win), :] += res[i * win:(i + 1) * win].astype(BF16)
        return carry

    lax.fori_loop(0, nw, window_pass, 0)
    xs_ref[...] = acc_ref[:, 0:slots, :]


def _gather(starts, h2, pos, *, slots, win, group=8):
    b, tt, d = h2.shape
    nt = tt // TOKEN_TILE
    kern = functools.partial(_gather_kernel, slots=slots, win=win, group=group)
    return pl.pallas_call(
        kern,
        grid_spec=pltpu.PrefetchScalarGridSpec(
            num_scalar_prefetch=1,
            grid=(b, N_EXPERTS // group),
            in_specs=[
                pl.BlockSpec((1, tt, d), lambda i, g, s: (i, 0, 0)),
                pl.BlockSpec((1, nt, N_EXPERTS, TOKEN_TILE), lambda i, g, s: (i, 0, 0, 0)),
            ],
            out_specs=pl.BlockSpec((group, slots, d), lambda i, g, s: (g, i, 0)),
            scratch_shapes=[pltpu.VMEM((group, slots + win, d), BF16)],
        ),
        out_shape=jax.ShapeDtypeStruct((N_EXPERTS, b * slots, d), BF16),
        compiler_params=_params(("arbitrary", "arbitrary")),
        name="moe_gather",
    )(starts, h2, pos)


def _ffn_kernel(xs_ref, wg_ref, wu_ref, wd_ref, ys_ref, *, fchunk):
    xs = xs_ref[0]
    f = wg_ref.shape[-1]
    acc = None
    for c in range(f // fchunk):
        sl = slice(c * fchunk, (c + 1) * fchunk)
        mm = (((1,), (0,)), ((), ()))
        a = lax.dot_general(xs, wg_ref[0, 0, :, sl], mm, preferred_element_type=F32)
        u = lax.dot_general(xs, wu_ref[0, 0, :, sl], mm, preferred_element_type=F32)
        hm = (a * jax.nn.sigmoid(a) * u).astype(BF16)
        y = lax.dot_general(hm, wd_ref[0, 0, sl, :], mm, preferred_element_type=F32)
        acc = y if acc is None else acc + y
    ys_ref[0] = acc.astype(BF16)


def _expert_ffn(xs, wg, wu, wd, l, *, slots):
    ne, rows, d = xs.shape
    f = wg.shape[-1]
    nb = rows // slots
    kern = functools.partial(_ffn_kernel, fchunk=_largest_divisor(f, (512,)))
    return pl.pallas_call(
        kern,
        grid=(ne, nb),
        in_specs=[
            pl.BlockSpec((1, slots, d), lambda e, i: (e, i, 0)),
            pl.BlockSpec((1, 1, d, f), lambda e, i: (l, e, 0, 0), pipeline_mode=pl.Buffered(1)),
            pl.BlockSpec((1, 1, d, f), lambda e, i: (l, e, 0, 0), pipeline_mode=pl.Buffered(1)),
            pl.BlockSpec((1, 1, f, d), lambda e, i: (l, e, 0, 0)),
        ],
        out_specs=pl.BlockSpec((1, slots, d), lambda e, i: (e, i, 0)),
        out_shape=jax.ShapeDtypeStruct((ne, rows, d), BF16),
        compiler_params=_params(("arbitrary", "arbitrary")),
        name="expert_ffn",
    )(xs, wg, wu, wd)


def _combine_kernel(starts_ref, ys_ref, pos_ref, gate_ref, x1_ref, mod_ref, g_ref, b_ref, o_ref,
                    *, slots, win, d, alpha):
    b = pl.program_id(0)
    tile = pl.program_id(1)
    rel = lax.broadcasted_iota(I32, (win, TOKEN_TILE), 0)
    bases = [(b * N_EXPERTS + e) * LANES for e in range(N_EXPERTS)]
    nw = jnp.int32(1)
    for e in range(N_EXPERTS):
        c0 = starts_ref[bases[e] + tile]
        c1 = starts_ref[bases[e] + tile + 1]
        nw = jnp.maximum(nw, (c1 - ((c0 >> 4) << 4) + win - 1) // win)

    def window_pass(w, moe):
        gated, rows = [], []
        for e in range(N_EXPERTS):
            lo, start = _window_start(starts_ref, bases[e], tile, w, win, slots - win)
            prow = pos_ref[0, 0, e:e + 1, :]
            prow = jnp.where(jnp.logical_and(prow >= lo, prow < lo + win), prow - start, -1)
            gated.append(jnp.where(rel == prow, gate_ref[0, 0, e:e + 1, :], 0.0).astype(BF16))
            rows.append(ys_ref[e, pl.ds(start, win), :])
        return moe + lax.dot_general(jnp.concatenate(gated, axis=0), jnp.concatenate(rows, axis=0),
                                     (((0,), (0,)), ((), ())), preferred_element_type=F32)

    moe = lax.fori_loop(0, nw, window_pass, jnp.zeros((TOKEN_TILE, d), F32))
    m = mod_ref[0]
    o_ref[0] = _ln(alpha * x1_ref[0] + m[:, 5 * d:6 * d] * moe) * g_ref[0] + b_ref[0]


def _combine(starts, ys, pos, gate, x1, mod2, ln_g, ln_b, l, *, slots, win, n_lat_tiles, alpha, out_tokens):
    b, _, d = x1.shape
    kern = functools.partial(_combine_kernel, slots=slots, win=win, d=d, alpha=alpha)
    tiled = lambda: pl.BlockSpec((1, 1, N_EXPERTS, TOKEN_TILE), lambda i, t, s: (i, t, 0, 0))
    tok = lambda: pl.BlockSpec((1, TOKEN_TILE, d), lambda i, t, s: (i, t, 0))
    return pl.pallas_call(
        kern,
        grid_spec=pltpu.PrefetchScalarGridSpec(
            num_scalar_prefetch=1,
            grid=(b, out_tokens // TOKEN_TILE),
            in_specs=[
                pl.BlockSpec((N_EXPERTS, slots, d), lambda i, t, s: (0, i, 0)),
                tiled(), tiled(), tok(),
                pl.BlockSpec((1, 1, 6 * d), lambda i, t, s: (2 * i + (t >= n_lat_tiles).astype(I32), 0, 0)),
                pl.BlockSpec((1, 1, d), lambda i, t, s: (l, 0, 0)),
                pl.BlockSpec((1, 1, d), lambda i, t, s: (l, 0, 0)),
            ],
            out_specs=tok(),
        ),
        out_shape=jax.ShapeDtypeStruct((b, out_tokens, d), F32),
        compiler_params=_params(("arbitrary", "arbitrary")),
        name="moe_combine",
    )(starts, ys, pos, gate, x1, mod2, ln_g, ln_b)


def _rope_tables(n, cn):
    lane = np.arange(LANES)
    within = lane % QK_DIM
    use_col = (within // 32) == 1
    first_half = (within % 32) < 16
    inv = ROPE_BASE ** (-(within % 16).astype(np.float64) / 16.0)
    pos = np.arange(n)
    coord = np.where(use_col[None, :], (pos % GRID_W)[:, None], (pos // GRID_W)[:, None]).astype(np.float32)
    ang = jnp.asarray(coord) * jnp.asarray(inv.astype(np.float32))[None, :]
    cos = jnp.cos(ang)
    sin = jnp.where(jnp.asarray(first_half)[None, :], -jnp.sin(ang), jnp.sin(ang))
    cos = jnp.concatenate([cos, jnp.ones((cn, LANES), F32)], axis=0)
    sin = jnp.concatenate([sin, jnp.zeros((cn, LANES), F32)], axis=0)
    return cos, sin


def _channel_dft():
    idx = np.arange(FNET_W)
    same = (idx[:, None] // FNET_GROUP_W) == (idx[None, :] // FNET_GROUP_W)
    ang = 2.0 * np.pi * ((idx[:, None] % FNET_GROUP_W) * (idx[None, :] % FNET_GROUP_W) % FNET_GROUP_W) / FNET_GROUP_W
    cs = np.concatenate([np.where(same, np.cos(ang), 0.0), np.where(same, np.sin(ang), 0.0)], axis=1)
    return jnp.asarray(cs.astype(np.float32)).astype(BF16)


def _position_dft_mats(t):
    g = math.gcd(t, 64)
    kk = jnp.arange(t, dtype=I32)

    def table(m):
        ph = ((kk[:, None] * m[None, :]) % t).astype(F32) * (2.0 * math.pi / t)
        return jnp.cos(ph), jnp.sin(ph)

    ch, sh = table(jnp.arange(t // g, dtype=I32) * g)
    cl, sl = table(jnp.arange(g, dtype=I32))
    scale = 1.0 / math.sqrt(t * FNET_GROUP_W)
    cmat = (ch[:, :, None] * cl[:, None, :] - sh[:, :, None] * sl[:, None, :]).reshape(t, t) * scale
    smat = (sh[:, :, None] * cl[:, None, :] + ch[:, :, None] * sl[:, None, :]).reshape(t, t) * (-scale)
    return cmat.astype(BF16), smat.astype(BF16)


def _block_diag_gates(wa, wx):
    depth = wa.shape[0]

    def dense(wb):
        eye = jnp.eye(LRU_BLOCKS, dtype=wb.dtype)
        return jnp.einsum('lncd,nm->lncmd', wb, eye).reshape(depth, LRU_W, LRU_W)

    halves = []
    for hh in range(2):
        sl = slice(hh * LANES, (hh + 1) * LANES)
        cols = [dense(wmat[:, dr])[:, sl, sl] for dr in range(2) for wmat in (wa, wx)]
        halves.append(jnp.concatenate(cols, axis=-1))
    return jnp.stack(halves, axis=1).reshape(depth * 2, LANES, 4 * LANES).astype(BF16)


def _gate_bias(ba, bx):
    depth = ba.shape[0]
    halves = []
    for hh in range(2):
        sl = slice(hh * LANES, (hh + 1) * LANES)
        halves.append(jnp.concatenate([bvec[:, dr, sl] for dr in range(2) for bvec in (ba, bx)], axis=-1))
    return jnp.stack(halves, axis=1).reshape(depth * 2, 1, 4 * LANES)


def _split_halves(a):
    depth, r, _ = a.shape
    return a.reshape(depth, r, 2, LANES).transpose(0, 2, 1, 3).reshape(depth * 2, r, LANES)


def kernel(x, c, ctx, c_ctx, w_mod, b_mod, w_in, lam_q1, lam_k1, lam_q2, lam_k2, attn_norm_g, conv_w, conv_b, lru_wa, lru_ba, lru_wx, lru_bx, lru_lam, w_out, ln1_g, ln1_b, w_router, w_gate, w_up, w_down, ln2_g, ln2_b):
    b, n, d = x.shape
    cn = ctx.shape[1]
    depth = w_mod.shape[0]
    tt = n + cn
    tb = _largest_divisor(math.gcd(n, cn), (256, 128))
    assert n % tb == 0 and cn % tb == 0 and n % GRID_W == 0 and tt % LANES == 0
    nbl = n // tb
    cap_l = CAPACITY_FACTOR * n // N_EXPERTS
    cap_c = CAPACITY_FACTOR * cn // N_EXPERTS
    assert cap_l % BF16_ROWS == 0 and cap_c % BF16_ROWS == 0 and n % TOKEN_TILE == 0 and cn % TOKEN_TILE == 0
    alpha = (2 * depth) ** 0.25

    rows = -(-(b + 1) // SUBLANES) * SUBLANES
    cc = jnp.concatenate([c, c_ctx[None, :], jnp.zeros((rows - b - 1, d), F32)], axis=0)
    mod = _modulation(cc, w_mod, b_mod)

    cos_t, sin_t = _rope_tables(n, cn)
    cs = _channel_dft()
    dft_l = _position_dft_mats(n)
    dft_c = _position_dft_mats(cn)

    w_in_bf = w_in.astype(BF16)
    w_out_bf = w_out.astype(BF16)
    wr_t = jnp.swapaxes(w_router, 1, 2).astype(BF16)
    lamv = jnp.stack([lam_q1, lam_k1, lam_q2, lam_k2], axis=1).astype(F32)
    lam_init = np.array([0.8 - 0.6 * math.exp(-0.3 * l) for l in range(depth)], np.float32)
    lin = jnp.asarray(np.broadcast_to(lam_init[:, None, None], (depth, 1, LANES)).copy())
    gain = attn_norm_g.reshape(depth * N_HEADS, 1, HEAD_V)
    conv_w2 = _split_halves(conv_w)
    conv_b2 = _split_halves(conv_b[:, None, :])
    lam2 = _split_halves(lru_lam)
    wgate = _block_diag_gates(lru_wa, lru_wx)
    bgate = _gate_bias(lru_ba, lru_bx)
    ln1g, ln1b = ln1_g[:, None, :], ln1_b[:, None, :]
    ln2g, ln2b = ln2_g[:, None, :], ln2_b[:, None, :]

    xc = jnp.concatenate([x, ctx], axis=1)
    for l in range(depth):
        ml = mod[l]
        mod2 = jnp.stack([ml[:b], jnp.broadcast_to(ml[b][None, :], (b, 6 * d))], axis=1).reshape(2 * b, 1, 6 * d)
        qx, k, v, ux, gg, gc, gs = _in_proj(xc, mod2, w_in_bf, l, cos_t, sin_t, cs, nbl=nbl, tb=tb)
        keep_ctx = l < depth - 1
        tokens = tt if keep_ctx else n
        att_l, att_c = _attention(qx, k, v, lamv, lin, gain, l, n=n, tb=tb)
        y = _rglru(ux, gg, conv_w2, conv_b2, wgate, bgate, lam2, l, n=n, cn=cn, tb=tb)
        fy_l = _position_dft(*dft_l, gc, gs, row0=0)
        fy_c = _position_dft(*dft_c, gc, gs, row0=n) if keep_ctx else fy_l
        x1, h2, logits_t = _out_proj(att_l, att_c, y, fy_l, fy_c, xc, mod2, w_out_bf, ln1g, ln1b, wr_t, l,
                                     nbl=nbl, tb=tb, alpha=alpha, out_tokens=tokens)
        segments = ((0, n, cap_l, 0), (n, cn, cap_c, cap_l)) if keep_ctx else ((0, n, cap_l, 0),)
        slots = cap_l + cap_c if keep_ctx else cap_l
        win = min(SLOT_WINDOW, slots)
        pos, gate, starts = _route(logits_t, segments=segments, n_slots=slots)
        starts = starts.reshape(-1)
        xs = _gather(starts, h2, pos, slots=slots, win=win)
        ys = _expert_ffn(xs, w_gate, w_up, w_down, l, slots=slots)
        xc = _combine(starts, ys, pos, gate, x1, mod2, ln2g, ln2b, l, slots=slots, win=win,
                      n_lat_tiles=n // TOKEN_TILE, alpha=alpha, out_tokens=tokens)
    return xc
```
